```python
import jax, jax.numpy as jnp
from jax import lax
import numpy as np

D_MODEL = 1024
BATCH = 8
SEQ = 2048
DEPTH = 1

D_MIX = D_MODEL
D_LRU = D_MIX // 2
D_CONV = D_MIX - D_LRU
LRU_HEADS = 8
LRU_HEAD_DIM = D_LRU // LRU_HEADS
LRU_CONV_WIDTH = 4
LRU_C = 8.0
LRU_A_MIN = 0.9
LRU_A_MAX = 0.999
CONV_GROUPS = 8
CONV_GROUP_DIM = D_CONV // CONV_GROUPS
CONF_KERNEL = 31
PEER_HEADS = 8
PEER_N_KEYS = 128
PEER_N_EXPERTS = PEER_N_KEYS ** 2
PEER_D_QUERY = 256
PEER_HALF = PEER_D_QUERY // 2
PEER_TOPK = 16
PEER_BLOCK = 128
EPS = 1e-6

kernel_name = "hybrid_rglru_conformer_peer_encoder"


def rms_norm(x, g):
    xf = x.astype(jnp.float32)
    y = xf * lax.rsqrt(jnp.mean(xf * xf, axis=-1, keepdims=True) + EPS)
    return (y * g.astype(jnp.float32)).astype(x.dtype)


def group_layer_norm(x, g, b):
    B_, S_, C = x.shape
    xg = x.astype(jnp.float32).reshape(B_, S_, CONV_GROUPS, CONV_GROUP_DIM)
    mu = jnp.mean(xg, axis=-1, keepdims=True)
    var = jnp.mean(jnp.square(xg - mu), axis=-1, keepdims=True)
    y = ((xg - mu) * lax.rsqrt(var + EPS)).reshape(B_, S_, C)
    return (y * g.astype(jnp.float32) + b.astype(jnp.float32)).astype(x.dtype)


def depthwise_conv(x, w, b, pad):
    C = x.shape[-1]
    y = lax.conv_general_dilated(
        x, w[:, None, :].astype(x.dtype), window_strides=(1,), padding=[pad],
        dimension_numbers=("NWC", "WIO", "NWC"), feature_group_count=C)
    return y + b.astype(x.dtype)


def block_diag_linear(x, w, b):
    B_, S_, _ = x.shape
    xh = x.reshape(B_, S_, LRU_HEADS, LRU_HEAD_DIM)
    y = jnp.einsum("bshi,hij->bshj", xh, w.astype(jnp.float32))
    return y.reshape(B_, S_, D_LRU) + b.astype(jnp.float32)


def _linear_recurrence(left, right):
    a_l, b_l = left
    a_r, b_r = right
    return a_l * a_r, a_r * b_l + b_r


def rg_lru(x, w_r, b_r, w_i, b_i, lam, reverse):
    xf = x.astype(jnp.float32)
    r = jax.nn.sigmoid(block_diag_linear(xf, w_r, b_r))
    i = jax.nn.sigmoid(block_diag_linear(xf, w_i, b_i))
    log_a = LRU_C * r * jax.nn.log_sigmoid(lam.astype(jnp.float32))
    a = jnp.exp(log_a)
    u = jnp.sqrt(-jnp.expm1(2.0 * log_a)) * (i * xf)
    _, h = lax.associative_scan(_linear_recurrence, (a, u), reverse=reverse, axis=1)
    return h


def hybrid_mixer(n, w_in, lru_conv_w, lru_conv_b, lru_w_rg, lru_b_rg, lru_w_ig, lru_b_ig,
                 lru_lambda, conf_conv_w, conf_conv_b, conf_norm_g, conf_norm_b,
                 beta_lru, beta_conv, w_out):
    z = jnp.einsum("bsd,de->bse", n, w_in.astype(n.dtype))
    x_lru, g_lru, a_glu, b_glu = jnp.split(
        z, [D_LRU, 2 * D_LRU, 2 * D_LRU + D_CONV], axis=-1)

    lpad = LRU_CONV_WIDTH // 2
    xc = depthwise_conv(x_lru, lru_conv_w, lru_conv_b, (lpad, LRU_CONV_WIDTH - 1 - lpad))
    h_fwd = rg_lru(xc, lru_w_rg[0], lru_b_rg[0], lru_w_ig[0], lru_b_ig[0], lru_lambda[0], False)
    h_bwd = rg_lru(xc, lru_w_rg[1], lru_b_rg[1], lru_w_ig[1], lru_b_ig[1], lru_lambda[1], True)
    y_lru = ((h_fwd + h_bwd) * jax.nn.gelu(g_lru.astype(jnp.float32))).astype(n.dtype)

    glu = a_glu * jax.nn.sigmoid(b_glu)
    c = depthwise_conv(glu, conf_conv_w, conf_conv_b, (CONF_KERNEL // 2, CONF_KERNEL // 2))
    y_conv = jax.nn.silu(group_layer_norm(c, conf_norm_g, conf_norm_b))

    y = jnp.concatenate([rms_norm(y_lru, beta_lru), rms_norm(y_conv, beta_conv)], axis=-1)
    return jnp.einsum("bse,ed->bsd", y, w_out.astype(y.dtype))


def peer_ffn(n, w_q, sub_keys, expert_u, expert_v):
    B_, S_, D = n.shape
    n_blocks = (B_ * S_) // PEER_BLOCK
    xb = n.reshape(n_blocks, PEER_BLOCK, D)

    def retrieve(xt):
        q = jnp.einsum("td,de->te", xt, w_q.astype(xt.dtype))
        q = q.reshape(PEER_BLOCK, PEER_HEADS, 2, PEER_HALF).astype(jnp.float32)
        s = jnp.einsum("thpc,hpkc->thpk", q, sub_keys.astype(jnp.float32))
        s1, i1 = lax.top_k(s[:, :, 0], PEER_TOPK)
        s2, i2 = lax.top_k(s[:, :, 1], PEER_TOPK)
        cand_s = (s1[..., :, None] + s2[..., None, :]).reshape(PEER_BLOCK, PEER_HEADS, PEER_TOPK * PEER_TOPK)
        cand_i = (i1[..., :, None] * PEER_N_KEYS + i2[..., None, :]).reshape(PEER_BLOCK, PEER_HEADS, PEER_TOPK * PEER_TOPK)
        top_s, pos = lax.top_k(cand_s, PEER_TOPK)
        idx = jnp.take_along_axis(cand_i, pos, axis=-1)
        gate = jax.nn.softmax(top_s, axis=-1)
        u = expert_u[idx]
        v = expert_v[idx]
        act = jax.nn.gelu(jnp.einsum("thkd,td->thk", u, xt))
        return jnp.einsum("thk,thkd->td", (gate * act).astype(v.dtype), v)

    out = lax.map(retrieve, xb)
    return out.reshape(B_, S_, D).astype(n.dtype)


def setup_inputs(seed: int = 0) -> dict:
    key = jax.random.key(seed)
    ks = jax.random.split(key, 24)
    L = DEPTH
    nrm = lambda k, shape, scale: jax.random.normal(k, shape, jnp.float32) * scale
    gain = lambda k, shape: 1.0 + 0.02 * jax.random.normal(k, shape, jnp.float32)
    a0 = jax.random.uniform(ks[9], (L, 2, D_LRU), jnp.float32, LRU_A_MIN, LRU_A_MAX)
    p = a0 ** (1.0 / LRU_C)
    lru_lambda = jnp.log(p) - jnp.log1p(-p)
    return {
        "x": nrm(ks[0], (BATCH, SEQ, D_MODEL), 1.0),
        "mix_norm_g": gain(ks[1], (L, D_MODEL)),
        "w_in": nrm(ks[2], (L, D_MODEL, 2 * D_LRU + 2 * D_CONV), D_MODEL ** -0.5),
        "lru_conv_w": nrm(ks[3], (L, LRU_CONV_WIDTH, D_LRU), LRU_CONV_WIDTH ** -0.5),
        "lru_conv_b": nrm(ks[4], (L, D_LRU), 0.02),
        "lru_w_rg": nrm(ks[5], (L, 2, LRU_HEADS, LRU_HEAD_DIM, LRU_HEAD_DIM), LRU_HEAD_DIM ** -0.5),
        "lru_b_rg": nrm(ks[6], (L, 2, D_LRU), 0.02),
        "lru_w_ig": nrm(ks[7], (L, 2, LRU_HEADS, LRU_HEAD_DIM, LRU_HEAD_DIM), LRU_HEAD_DIM ** -0.5),
        "lru_b_ig": nrm(ks[8], (L, 2, D_LRU), 0.02),
        "lru_lambda": lru_lambda,
        "conf_conv_w": nrm(ks[10], (L, CONF_KERNEL, D_CONV), CONF_KERNEL ** -0.5),
        "conf_conv_b": nrm(ks[11], (L, D_CONV), 0.02),
        "conf_norm_g": gain(ks[12], (L, D_CONV)),
        "conf_norm_b": nrm(ks[13], (L, D_CONV), 0.02),
        "beta_lru": gain(ks[14], (L, D_LRU)),
        "beta_conv": gain(ks[15], (L, D_CONV)),
        "w_out": nrm(ks[16], (L, D_MIX, D_MODEL), D_MIX ** -0.5),
        "ffn_norm_g": gain(ks[17], (L, D_MODEL)),
        "peer_w_q": nrm(ks[18], (L, D_MODEL, PEER_HEADS * PEER_D_QUERY), D_MODEL ** -0.5),
        "peer_sub_keys": nrm(ks[19], (L, PEER_HEADS, 2, PEER_N_KEYS, PEER_HALF), PEER_HALF ** -0.5),
        "peer_u": nrm(ks[20], (L, PEER_N_EXPERTS, D_MODEL), D_MODEL ** -0.5),
        "peer_v": nrm(ks[21], (L, PEER_N_EXPERTS, D_MODEL), PEER_HEADS ** -0.5),
        "final_norm_g": gain(ks[22], (D_MODEL,)),
    }


def reference(x, mix_norm_g, w_in, lru_conv_w, lru_conv_b, lru_w_rg, lru_b_rg, lru_w_ig,
              lru_b_ig, lru_lambda, conf_conv_w, conf_conv_b, conf_norm_g, conf_norm_b,
              beta_lru, beta_conv, w_out, ffn_norm_g, peer_w_q, peer_sub_keys, peer_u,
              peer_v, final_norm_g):
    h = x
    for l in range(DEPTH):
        h = h + hybrid_mixer(
            rms_norm(h, mix_norm_g[l]), w_in[l], lru_conv_w[l], lru_conv_b[l],
            lru_w_rg[l], lru_b_rg[l], lru_w_ig[l], lru_b_ig[l], lru_lambda[l],
            conf_conv_w[l], conf_conv_b[l], conf_norm_g[l], conf_norm_b[l],
            beta_lru[l], beta_conv[l], w_out[l])
        h = h + peer_ffn(rms_norm(h, ffn_norm_g[l]), peer_w_q[l], peer_sub_keys[l],
                         peer_u[l], peer_v[l])
    return rms_norm(h, final_norm_g)
```

```python
import functools

import jax
import jax.numpy as jnp
from jax import lax
from jax.experimental import pallas as pl
from jax.experimental.pallas import tpu as pltpu

F32 = jnp.float32
BF16 = jnp.bfloat16
HIGHEST = lax.Precision.HIGHEST

SUBLANES = 8
LANES = 128
MIB = 1024 * 1024

EPS = 1e-6
LRU_C = 8.0
LRU_HEADS = 8
LRU_CONV_WIDTH = 4
CONV_GROUPS = 8
CONF_KERNEL = 31
PEER_HEADS = 8
PEER_N_KEYS = 128
PEER_TOPK = 16

ROW_TILE = 512
MIX_CH = 256
LRU_ROWS = 256
CONV_ROWS = 128
SCORE_TOK = 256
PEER_TOK = 512
PEER_EXP = 1024


def _params(semantics, vmem_mib):
    return pltpu.CompilerParams(dimension_semantics=semantics,
                                vmem_limit_bytes=vmem_mib * MIB)


def _rms(x, g):
    return x * lax.rsqrt(jnp.mean(x * x, axis=-1, keepdims=True) + EPS) * g


def _gelu_tanh(x):
    c = 0.7978845608028654
    return x * (0.5 * (1.0 + jnp.tanh(c * (x + 0.044715 * (x * x * x)))))


def _sigmoid(x):
    return 1.0 / (1.0 + jnp.exp(-x))


def _log_sigmoid(x):
    return -(jnp.maximum(-x, 0.0) + jnp.log(1.0 + jnp.exp(-jnp.abs(x))))


def _transpose_cast_kernel(x_ref, o_ref):
    o_ref[...] = x_ref[...].T.astype(o_ref.dtype)


def _transpose_cast(x, dtype, tile=512):
    r, c = x.shape
    return pl.pallas_call(
        _transpose_cast_kernel,
        grid=(r // tile,),
        in_specs=[pl.BlockSpec((tile, c), lambda i: (i, 0))],
        out_specs=pl.BlockSpec((c, tile), lambda i: (0, i)),
        out_shape=jax.ShapeDtypeStruct((c, r), dtype),
        compiler_params=_params(("parallel",), 24),
        name="transpose_cast",
    )(x)


def _cast_kernel(x_ref, o_ref):
    o_ref[...] = x_ref[...].astype(o_ref.dtype)


def _cast(x, dtype, tile=512):
    r, c = x.shape
    return pl.pallas_call(
        _cast_kernel,
        grid=(r // tile,),
        in_specs=[pl.BlockSpec((tile, c), lambda i: (i, 0))],
        out_specs=pl.BlockSpec((tile, c), lambda i: (i, 0)),
        out_shape=jax.ShapeDtypeStruct((r, c), dtype),
        compiler_params=_params(("parallel",), 24),
        name="cast",
    )(x)


def _inproj_kernel(x_ref, g_ref, w_ref, z_ref):
    n = _rms(x_ref[...], g_ref[...])
    z_ref[...] = jnp.dot(n.astype(BF16), w_ref[...], preferred_element_type=F32)


def _inproj(x2, g, w_bf):
    t, d = x2.shape
    e = w_bf.shape[1]
    return pl.pallas_call(
        _inproj_kernel,
        grid=(t // ROW_TILE,),
        in_specs=[pl.BlockSpec((ROW_TILE, d), lambda i: (i, 0)),
                  pl.BlockSpec((1, d), lambda i: (0, 0)),
                  pl.BlockSpec((d, e), lambda i: (0, 0))],
        out_specs=pl.BlockSpec((ROW_TILE, e), lambda i: (i, 0)),
        out_shape=jax.ShapeDtypeStruct((t, e), F32),
        compiler_params=_params(("parallel",), 40),
        name="inproj",
    )(x2, g, w_bf)


def _scan_in_place(a_ref, u_ref, seq, ch, reverse):
    groups = seq // SUBLANES
    row = lax.broadcasted_iota(jnp.int32, (SUBLANES, ch), 0)

    def body(g, carry):
        gi = groups - 1 - g if reverse else g
        r0 = pl.multiple_of(gi * SUBLANES, SUBLANES)
        a = a_ref[pl.ds(r0, SUBLANES), :]
        u = u_ref[pl.ds(r0, SUBLANES), :]
        for d in (1, 2, 4):
            if reverse:
                shift, keep = SUBLANES - d, row < SUBLANES - d
            else:
                shift, keep = d, row >= d
            a_nb = pltpu.roll(a, shift, 0)
            u_nb = pltpu.roll(u, shift, 0)
            u = u + a * jnp.where(keep, u_nb, 0.0)
            a = a * jnp.where(keep, a_nb, 1.0)
        h = u + a * carry
        u_ref[pl.ds(r0, SUBLANES), :] = h
        edge = h[0:1] if reverse else h[SUBLANES - 1:SUBLANES]
        return jnp.broadcast_to(edge, (SUBLANES, ch))

    lax.fori_loop(0, groups, body, jnp.zeros((SUBLANES, ch), F32), unroll=4)


def _lru_kernel(x_ref, gate_ref, cw_ref, cb_ref, wg_ref, bg_ref, lam_ref, y_ref,
                xpad_ref, af_ref, uf_ref, ab_ref, ub_ref, *, seq, ch):
    pad = SUBLANES
    zeros = jnp.zeros((pad, ch), F32)
    xpad_ref[pl.ds(0, pad), :] = zeros
    xpad_ref[pl.ds(seq + pad, pad), :] = zeros
    xpad_ref[pl.ds(pad, seq), :] = x_ref[...]

    cw = cw_ref[...]
    cb = cb_ref[...]
    bg = bg_ref[...]
    log_sig = _log_sigmoid(lam_ref[...])
    lpad = LRU_CONV_WIDTH // 2

    def gates(c, _):
        t0 = pl.multiple_of(c * LRU_ROWS, LRU_ROWS)
        win = xpad_ref[pl.ds(t0, LRU_ROWS + 2 * pad), :]
        xc = jnp.zeros((LRU_ROWS, ch), F32) + cb
        for k in range(LRU_CONV_WIDTH):
            off = pad - lpad + k
            xc = xc + cw[k:k + 1, :] * win[off:off + LRU_ROWS, :]
        pre = jnp.dot(xc, wg_ref[...], precision=HIGHEST,
                      preferred_element_type=F32) + bg
        for d, (a_ref, u_ref) in enumerate(((af_ref, uf_ref), (ab_ref, ub_ref))):
            r = _sigmoid(pre[:, (2 * d) * ch:(2 * d + 1) * ch])
            i = _sigmoid(pre[:, (2 * d + 1) * ch:(2 * d + 2) * ch])
            a = jnp.exp(LRU_C * r * log_sig[d:d + 1, :])
            a_ref[pl.ds(t0, LRU_ROWS), :] = a
            u_ref[pl.ds(t0, LRU_ROWS), :] = jnp.sqrt(1.0 - a * a) * (i * xc)
        return 0

    lax.fori_loop(0, seq // LRU_ROWS, gates, 0)

    _scan_in_place(af_ref, uf_ref, seq, ch, reverse=False)
    _scan_in_place(ab_ref, ub_ref, seq, ch, reverse=True)

    def finish(c, _):
        t0 = pl.multiple_of(c * LRU_ROWS, LRU_ROWS)
        h = uf_ref[pl.ds(t0, LRU_ROWS), :] + ub_ref[pl.ds(t0, LRU_ROWS), :]
        y_ref[pl.ds(t0, LRU_ROWS), :] = h * _gelu_tanh(gate_ref[pl.ds(t0, LRU_ROWS), :])
        return 0

    lax.fori_loop(0, seq // LRU_ROWS, finish, 0)


def _lru_branch(z3, conv_w, conv_b, wg, bg, lam, d_lru):
    b, s, _ = z3.shape
    ch = MIX_CH
    nc = d_lru // ch
    kern = functools.partial(_lru_kernel, seq=s, ch=ch)
    return pl.pallas_call(
        kern,
        grid=(b, nc),
        in_specs=[pl.BlockSpec((None, s, ch), lambda i, c: (i, 0, c)),
                  pl.BlockSpec((None, s, ch), lambda i, c: (i, 0, nc + c)),
                  pl.BlockSpec((LRU_CONV_WIDTH, ch), lambda i, c: (0, c)),
                  pl.BlockSpec((1, ch), lambda i, c: (0, c)),
                  pl.BlockSpec((None, ch, 4 * ch), lambda i, c: (c, 0, 0)),
                  pl.BlockSpec((None, 1, 4 * ch), lambda i, c: (c, 0, 0)),
                  pl.BlockSpec((2, ch), lambda i, c: (0, c))],
        out_specs=pl.BlockSpec((None, s, ch), lambda i, c: (i, 0, c)),
        out_shape=jax.ShapeDtypeStruct((b, s, d_lru), F32),
        scratch_shapes=[pltpu.VMEM((s + 2 * SUBLANES, ch), F32)]
        + [pltpu.VMEM((s, ch), F32)] * 4,
        compiler_params=_params(("parallel", "parallel"), 40),
        name="lru_branch",
    )(z3, z3, conv_w, conv_b, wg, bg, lam)


def _conf_kernel(a_ref, b_ref, cw_ref, cb_ref, ng_ref, nb_ref, avg_ref, y_ref,
                 gpad_ref, *, seq, ch):
    pad = 2 * SUBLANES
    half = CONF_KERNEL // 2
    zeros = jnp.zeros((pad, ch), F32)
    gpad_ref[pl.ds(0, pad), :] = zeros
    gpad_ref[pl.ds(seq + pad, pad), :] = zeros
    gpad_ref[pl.ds(pad, seq), :] = a_ref[...] * _sigmoid(b_ref[...])

    cw = cw_ref[...]
    cb = cb_ref[...]
    ng = ng_ref[...]
    nb = nb_ref[...]

    def chunk(c, _):
        t0 = pl.multiple_of(c * CONV_ROWS, CONV_ROWS)
        win = gpad_ref[pl.ds(t0, CONV_ROWS + 2 * pad), :]
        acc = jnp.zeros((CONV_ROWS, ch), F32) + cb
        for k in range(CONF_KERNEL):
            off = pad - half + k
            acc = acc + cw[k:k + 1, :] * win[off:off + CONV_ROWS, :]
        mu = jnp.dot(acc, avg_ref[...], precision=HIGHEST, preferred_element_type=F32)
        dev = acc - mu
        var = jnp.dot(dev * dev, avg_ref[...], precision=HIGHEST,
                      preferred_element_type=F32)
        y = dev * lax.rsqrt(var + EPS) * ng + nb
        y_ref[pl.ds(t0, CONV_ROWS), :] = y * _sigmoid(y)
        return 0

    lax.fori_loop(0, seq // CONV_ROWS, chunk, 0)


def _conf_branch(z3, conv_w, conv_b, norm_g, norm_b, avg, d_lru, d_conv):
    b, s, _ = z3.shape
    ch = MIX_CH
    nc = d_conv // ch
    base = 2 * d_lru // ch
    kern = functools.partial(_conf_kernel, seq=s, ch=ch)
    return pl.pallas_call(
        kern,
        grid=(b, nc),
        in_specs=[pl.BlockSpec((None, s, ch), lambda i, c: (i, 0, base + c)),
                  pl.BlockSpec((None, s, ch), lambda i, c: (i, 0, base + nc + c)),
                  pl.BlockSpec((CONF_KERNEL, ch), lambda i, c: (0, c)),
                  pl.BlockSpec((1, ch), lambda i, c: (0, c)),
                  pl.BlockSpec((1, ch), lambda i, c: (0, c)),
                  pl.BlockSpec((1, ch), lambda i, c: (0, c)),
                  pl.BlockSpec((ch, ch), lambda i, c: (0, 0))],
        out_specs=pl.BlockSpec((None, s, ch), lambda i, c: (i, 0, c)),
        out_shape=jax.ShapeDtypeStruct((b, s, d_conv), F32),
        scratch_shapes=[pltpu.VMEM((s + 4 * SUBLANES, ch), F32)],
        compiler_params=_params(("parallel", "parallel"), 40),
        name="conf_branch",
    )(z3, z3, conv_w, conv_b, norm_g, norm_b, avg)


def _outproj_kernel(yl_ref, yc_ref, x_ref, bl_ref, bc_ref, wl_ref, wc_ref, fg_ref,
                    h_ref, nt_ref, ntb_ref):
    yl = _rms(yl_ref[...], bl_ref[...]).astype(BF16)
    yc = _rms(yc_ref[...], bc_ref[...]).astype(BF16)
    h = (x_ref[...]
         + jnp.dot(yl, wl_ref[...], preferred_element_type=F32)
         + jnp.dot(yc, wc_ref[...], preferred_element_type=F32))
    h_ref[...] = h
    nt = _rms(h, fg_ref[...]).T
    nt_ref[...] = nt
    ntb_ref[...] = nt.astype(BF16)


def _outproj(yl, yc, x2, beta_l, beta_c, wl_bf, wc_bf, ffn_g):
    t, d = x2.shape
    dl = yl.shape[1]
    dc = yc.shape[1]
    row = lambda i: (i, 0)
    fixed = lambda i: (0, 0)
    return pl.pallas_call(
        _outproj_kernel,
        grid=(t // ROW_TILE,),
        in_specs=[pl.BlockSpec((ROW_TILE, dl), row),
                  pl.BlockSpec((ROW_TILE, dc), row),
                  pl.BlockSpec((ROW_TILE, d), row),
                  pl.BlockSpec((1, dl), fixed),
                  pl.BlockSpec((1, dc), fixed),
                  pl.BlockSpec((dl, d), fixed),
                  pl.BlockSpec((dc, d), fixed),
                  pl.BlockSpec((1, d), fixed)],
        out_specs=[pl.BlockSpec((ROW_TILE, d), row),
                   pl.BlockSpec((d, ROW_TILE), lambda i: (0, i)),
                   pl.BlockSpec((d, ROW_TILE), lambda i: (0, i))],
        out_shape=[jax.ShapeDtypeStruct((t, d), F32),
                   jax.ShapeDtypeStruct((d, t), F32),
                   jax.ShapeDtypeStruct((d, t), BF16)],
        compiler_params=_params(("parallel",), 40),
        name="outproj",
    )(yl, yc, x2, beta_l, beta_c, wl_bf, wc_bf, ffn_g)


def _sort_network(n):
    pairs = []
    p = 1
    while p < n:
        k = p
        while k >= 1:
            for j in range(k % p, n - k, 2 * k):
                for i in range(min(k, n - j - k)):
                    if (i + j) // (2 * p) == (i + j + k) // (2 * p):
                        pairs.append((i + j, i + j + k))
            k //= 2
        p *= 2
    return pairs


def _pruned_network(n_pow2, n_live, n_out):
    pairs = [(i, j) for i, j in _sort_network(n_pow2) if j < n_live]
    needed = set(range(n_out))
    kept = []
    for i, j in reversed(pairs):
        if i in needed or j in needed:
            kept.append((i, j))
            needed.update((i, j))
    return kept[::-1]


def _apply_network(vals, pairs):
    vals = list(vals)
    for i, j in pairs:
        hi = jnp.maximum(vals[i], vals[j])
        lo = jnp.minimum(vals[i], vals[j])
        vals[i], vals[j] = hi, lo
    return vals


def _top16_over_keys(s):
    k = PEER_TOPK
    blocks = [s[SUBLANES * v:SUBLANES * (v + 1), :] for v in range(PEER_N_KEYS // SUBLANES)]
    top = _apply_network(blocks, _sort_network(len(blocks)))
    for d in (1, 2, 4):
        top = [jnp.maximum(top[i], pltpu.roll(top[k - 1 - i], d, 0)) for i in range(k)]
        stride = k // 2
        while stride >= 1:
            pairs = [(i, i + stride) for i in range(k) if not i & stride]
            top = _apply_network(top, pairs)
            stride //= 2
    return top


def _score_kernel(nt_ref, wq_ref, keys_ref, s1_ref, s2_ref, g1_ref, e2_ref, tau_ref,
                  top_ref, z_ref, *, tok):
    k = PEER_TOPK
    nt = nt_ref[...]

    def per_half(hp, _):
        h = hp // 2
        p = hp % 2
        r0 = pl.multiple_of(hp * PEER_N_KEYS, PEER_N_KEYS)
        q = jnp.dot(wq_ref[pl.ds(r0, PEER_N_KEYS), :], nt, precision=HIGHEST,
                    preferred_element_type=F32)
        s = jnp.dot(keys_ref[hp], q, precision=HIGHEST,
                    preferred_element_type=F32)

        @pl.when(p == 0)
        def _():
            s1_ref[h] = s

        @pl.when(p == 1)
        def _():
            s2_ref[h] = s

        top = _top16_over_keys(s)
        for i in range(k):
            top_ref[p, i, pl.ds(h, 1), :] = top[i][0:1, :]
        return 0

    lax.fori_loop(0, 2 * PEER_HEADS, per_half, 0)

    first = [top_ref[0, i] for i in range(k)]
    second = [top_ref[1, i] for i in range(k)]
    cands = [first[a] + second[b] for a in range(k) for b in range(k)
             if (a + 1) * (b + 1) <= k]
    n_pow2 = 1
    while n_pow2 < len(cands):
        n_pow2 *= 2
    best = _apply_network(cands, _pruned_network(n_pow2, len(cands), k))[:k]
    tau_ref[...] = best[k - 1]
    z = jnp.zeros_like(best[0])
    for c in best:
        z = z + jnp.exp(c - best[0])
    z_ref[...] = z

    def gates(h, _):
        inv_z = 1.0 / z_ref[pl.ds(h, 1), :]
        m1 = top_ref[0, 0, pl.ds(h, 1), :]
        m2 = top_ref[1, 0, pl.ds(h, 1), :]
        g1_ref[h] = jnp.exp(s1_ref[h] - m1) * inv_z
        e2_ref[h] = jnp.exp(s2_ref[h] - m2)
        return 0

    lax.fori_loop(0, PEER_HEADS, gates, 0)


def _peer_scores(nt, wq_t, keys):
    d, t = nt.shape
    tok = SCORE_TOK
    kern = functools.partial(_score_kernel, tok=tok)
    big = jax.ShapeDtypeStruct((PEER_HEADS, PEER_N_KEYS, t), F32)
    big_spec = pl.BlockSpec((PEER_HEADS, PEER_N_KEYS, tok), lambda i: (0, 0, i))
    return pl.pallas_call(
        kern,
        grid=(t // tok,),
        in_specs=[pl.BlockSpec((d, tok), lambda i: (0, i)),
                  pl.BlockSpec(wq_t.shape, lambda i: (0, 0)),
                  pl.BlockSpec(keys.shape, lambda i: (0, 0, 0))],
        out_specs=[big_spec, big_spec, big_spec, big_spec,
                   pl.BlockSpec((PEER_HEADS, tok), lambda i: (0, i))],
        out_shape=[big, big, big, big, jax.ShapeDtypeStruct((PEER_HEADS, t), F32)],
        scratch_shapes=[pltpu.VMEM((2, PEER_TOPK, PEER_HEADS, tok), F32),
                        pltpu.VMEM((PEER_HEADS, tok), F32)],
        compiler_params=_params(("parallel",), 48),
        name="peer_scores",
    )(nt, wq_t, keys)


def _peer_kernel(nt_ref, u_ref, vt_ref, s1_ref, s2_ref, g1_ref, e2_ref, tau_ref, out_ref,
                 act_ref, a_ref, acc_ref, *, tok, n_exp):
    kstep = pl.program_id(1)
    rows_per_step = n_exp // PEER_N_KEYS

    @pl.when(kstep == 0)
    def _():
        acc_ref[...] = jnp.zeros_like(acc_ref)

    act_ref[...] = jnp.dot(u_ref[...], nt_ref[...], preferred_element_type=F32)

    i0 = pl.multiple_of(kstep * rows_per_step, rows_per_step)

    def lane_chunk(lc, _):
        c0 = pl.multiple_of(lc * LANES, LANES)
        cols = pl.ds(c0, LANES)
        tau = tau_ref[:, cols]
        s1 = [s1_ref[h, pl.ds(i0, rows_per_step), cols] for h in range(PEER_HEADS)]
        g1 = [g1_ref[h, pl.ds(i0, rows_per_step), cols] for h in range(PEER_HEADS)]
        for ii in range(rows_per_step):
            w = jnp.zeros((PEER_N_KEYS, LANES), F32)
            for h in range(PEER_HEADS):
                keep = (s2_ref[h, :, cols] + s1[h][ii:ii + 1, :]) >= tau[h:h + 1, :]
                w = w + jnp.where(keep, e2_ref[h, :, cols] * g1[h][ii:ii + 1, :], 0.0)
            rows = pl.ds(ii * PEER_N_KEYS, PEER_N_KEYS)
            a_ref[rows, cols] = (w * _gelu_tanh(act_ref[rows, cols])).astype(BF16)
        return 0

    lax.fori_loop(0, tok // LANES, lane_chunk, 0)

    acc_ref[...] += jnp.dot(vt_ref[...], a_ref[...], preferred_element_type=F32)

    @pl.when(kstep == pl.num_programs(1) - 1)
    def _():
        out_ref[...] = acc_ref[...]


def _peer_dense(nt_bf, u_bf, vt_bf, s1, s2, g1, e2, tau):
    d, t = nt_bf.shape
    n_experts = u_bf.shape[0]
    tok, n_exp = PEER_TOK, PEER_EXP
    kern = functools.partial(_peer_kernel, tok=tok, n_exp=n_exp)
    big_spec = pl.BlockSpec((PEER_HEADS, PEER_N_KEYS, tok), lambda i, k: (0, 0, i))
    return pl.pallas_call(
        kern,
        grid=(t // tok, n_experts // n_exp),
        in_specs=[pl.BlockSpec((d, tok), lambda i, k: (0, i)),
                  pl.BlockSpec((n_exp, d), lambda i, k: (k, 0)),
                  pl.BlockSpec((d, n_exp), lambda i, k: (0, k)),
                  big_spec, big_spec, big_spec, big_spec,
                  pl.BlockSpec((PEER_HEADS, tok), lambda i, k: (0, i))],
        out_specs=pl.BlockSpec((d, tok), lambda i, k: (0, i)),
        out_shape=jax.ShapeDtypeStruct((d, t), F32),
        scratch_shapes=[pltpu.VMEM((n_exp, tok), F32),
                        pltpu.VMEM((n_exp, tok), BF16),
                        pltpu.VMEM((d, tok), F32)],
        compiler_params=_params(("parallel", "arbitrary"), 48),
        name="peer_dense",
    )(nt_bf, u_bf, vt_bf, s1, s2, g1, e2, tau)


def _final_kernel(h_ref, pt_ref, g_ref, o_ref):
    o_ref[...] = _rms(h_ref[...] + pt_ref[...].T, g_ref[...])


def _final(h, peer_t, g):
    t, d = h.shape
    return pl.pallas_call(
        _final_kernel,
        grid=(t // ROW_TILE,),
        in_specs=[pl.BlockSpec((ROW_TILE, d), lambda i: (i, 0)),
                  pl.BlockSpec((d, ROW_TILE), lambda i: (0, i)),
                  pl.BlockSpec((1, d), lambda i: (0, 0))],
        out_specs=pl.BlockSpec((ROW_TILE, d), lambda i: (i, 0)),
        out_shape=jax.ShapeDtypeStruct((t, d), F32),
        compiler_params=_params(("parallel",), 24),
        name="final_norm",
    )(h, peer_t, g)


def _block_diag_chunks(w, ch):
    heads, hd, _ = w.shape
    per = ch // hd
    w4 = w.reshape(heads // per, per, hd, hd)
    eye = jnp.eye(per, dtype=w.dtype)
    return jnp.einsum("chij,hg->chigj", w4, eye).reshape(heads // per, ch, ch)


def _layer(h2, batch, seq, mix_norm_g, w_in, lru_conv_w, lru_conv_b, lru_w_rg, lru_b_rg,
           lru_w_ig, lru_b_ig, lru_lambda, conf_conv_w, conf_conv_b, conf_norm_g,
           conf_norm_b, beta_lru, beta_conv, w_out, ffn_norm_g, peer_w_q, peer_sub_keys,
           peer_u, peer_v):
    t, d = h2.shape
    d_lru = lru_conv_w.shape[1]
    d_conv = conf_conv_w.shape[1]
    row = lambda v: v.reshape(1, -1)

    z = _inproj(h2, row(mix_norm_g), w_in.astype(BF16))
    z3 = z.reshape(batch, seq, z.shape[1])

    ch = MIX_CH
    nc = d_lru // ch
    wg = jnp.concatenate([_block_diag_chunks(lru_w_rg[0], ch), _block_diag_chunks(lru_w_ig[0], ch),
                          _block_diag_chunks(lru_w_rg[1], ch), _block_diag_chunks(lru_w_ig[1], ch)],
                         axis=-1)
    bg = jnp.concatenate([lru_b_rg[0].reshape(nc, 1, ch), lru_b_ig[0].reshape(nc, 1, ch),
                          lru_b_rg[1].reshape(nc, 1, ch), lru_b_ig[1].reshape(nc, 1, ch)],
                         axis=-1)
    y_lru = _lru_branch(z3, lru_conv_w, row(lru_conv_b), wg, bg, lru_lambda, d_lru)

    gdim = d_conv // CONV_GROUPS
    grp = jnp.arange(ch) // gdim
    avg = (grp[:, None] == grp[None, :]).astype(F32) / gdim
    y_conv = _conf_branch(z3, conf_conv_w, row(conf_conv_b), row(conf_norm_g),
                          row(conf_norm_b), avg, d_lru, d_conv)

    w_out_bf = w_out.astype(BF16)
    h2, nt, nt_bf = _outproj(y_lru.reshape(t, d_lru), y_conv.reshape(t, d_conv), h2,
                             row(beta_lru), row(beta_conv), w_out_bf[:d_lru], w_out_bf[d_lru:],
                             row(ffn_norm_g))

    keys = peer_sub_keys.reshape(2 * PEER_HEADS, PEER_N_KEYS, -1)
    s1, s2, g1, e2, tau = _peer_scores(nt, peer_w_q.T, keys)
    peer_t = _peer_dense(nt_bf, _cast(peer_u, BF16), _transpose_cast(peer_v, BF16),
                         s1, s2, g1, e2, tau)
    return h2, peer_t


def kernel(x, mix_norm_g, w_in, lru_conv_w, lru_conv_b, lru_w_rg, lru_b_rg, lru_w_ig, lru_b_ig, lru_lambda, conf_conv_w, conf_conv_b, conf_norm_g, conf_norm_b, beta_lru, beta_conv, w_out, ffn_norm_g, peer_w_q, peer_sub_keys, peer_u, peer_v, final_norm_g):
    batch, seq, d = x.shape
    depth = w_in.shape[0]
    h2 = x.reshape(batch * seq, d)
    peer_t = None
    for l in range(depth):
        if peer_t is not None:
            h2 = h2 + peer_t.T
        h2, peer_t = _layer(
            h2, batch, seq, mix_norm_g[l], w_in[l], lru_conv_w[l], lru_conv_b[l], lru_w_rg[l],
            lru_b_rg[l], lru_w_ig[l], lru_b_ig[l], lru_lambda[l], conf_conv_w[l], conf_conv_b[l],
            conf_norm_g[l], conf_norm_b[l], beta_lru[l], beta_conv[l], w_out[l], ffn_norm_g[l],
            peer_w_q[l], peer_sub_keys[l], peer_u[l], peer_v[l])
    out = _final(h2, peer_t, final_norm_g.reshape(1, -1))
    return out.reshape(batch, seq, d)
```

```python
import functools

import jax
import jax.numpy as jnp
from jax import lax
from jax.experimental import pallas as pl
from jax.experimental.pallas import tpu as pltpu

F32 = jnp.float32
BF16 = jnp.bfloat16
HIGHEST = lax.Precision.HIGHEST

SUBLANES = 8
LANES = 128
MIB = 1024 * 1024

EPS = 1e-6
LRU_C = 8.0
LRU_HEADS = 8
LRU_CONV_WIDTH = 4
CONV_GROUPS = 8
CONF_KERNEL = 31
PEER_HEADS = 8
PEER_N_KEYS = 128
PEER_TOPK = 16

ROW_TILE = 512
MIX_CH = 256
LRU_ROWS = 256
CONV_ROWS = 128
SCORE_TOK = 256
PEER_TOK = 512
PEER_EXP = 1024
W_ROWS = 32


def _params(semantics, vmem_mib):
    return pltpu.CompilerParams(dimension_semantics=semantics,
                                vmem_limit_bytes=vmem_mib * MIB)


def _rms(x, g):
    return x * lax.rsqrt(jnp.mean(x * x, axis=-1, keepdims=True) + EPS) * g


def _gelu_tanh(x):
    c = 0.7978845608028654
    return x * (0.5 * (1.0 + jnp.tanh(c * (x + 0.044715 * (x * x * x)))))


def _sigmoid(x):
    return 1.0 / (1.0 + jnp.exp(-x))


def _log_sigmoid(x):
    return -(jnp.maximum(-x, 0.0) + jnp.log(1.0 + jnp.exp(-jnp.abs(x))))


def _transpose_cast_kernel(x_ref, o_ref):
    o_ref[...] = x_ref[...].T.astype(o_ref.dtype)


def _transpose_cast(x, dtype, tile=512):
    r, c = x.shape
    return pl.pallas_call(
        _transpose_cast_kernel,
        grid=(r // tile,),
        in_specs=[pl.BlockSpec((tile, c), lambda i: (i, 0))],
        out_specs=pl.BlockSpec((c, tile), lambda i: (0, i)),
        out_shape=jax.ShapeDtypeStruct((c, r), dtype),
        compiler_params=_params(("parallel",), 24),
        name="transpose_cast",
    )(x)


def _cast_kernel(x_ref, o_ref):
    o_ref[...] = x_ref[...].astype(o_ref.dtype)


def _cast(x, dtype, tile=512):
    r, c = x.shape
    return pl.pallas_call(
        _cast_kernel,
        grid=(r // tile,),
        in_specs=[pl.BlockSpec((tile, c), lambda i: (i, 0))],
        out_specs=pl.BlockSpec((tile, c), lambda i: (i, 0)),
        out_shape=jax.ShapeDtypeStruct((r, c), dtype),
        compiler_params=_params(("parallel",), 24),
        name="cast",
    )(x)


def _inproj_kernel(x_ref, g_ref, w_ref, z_ref):
    n = _rms(x_ref[...], g_ref[...])
    z_ref[...] = jnp.dot(n.astype(BF16), w_ref[...], preferred_element_type=F32)


def _inproj(x2, g, w_bf):
    t, d = x2.shape
    e = w_bf.shape[1]
    return pl.pallas_call(
        _inproj_kernel,
        grid=(t // ROW_TILE,),
        in_specs=[pl.BlockSpec((ROW_TILE, d), lambda i: (i, 0)),
                  pl.BlockSpec((1, d), lambda i: (0, 0)),
                  pl.BlockSpec((d, e), lambda i: (0, 0))],
        out_specs=pl.BlockSpec((ROW_TILE, e), lambda i: (i, 0)),
        out_shape=jax.ShapeDtypeStruct((t, e), F32),
        compiler_params=_params(("parallel",), 40),
        name="inproj",
    )(x2, g, w_bf)


def _scan_in_place(a_ref, u_ref, seq, ch, reverse):
    groups = seq // SUBLANES
    row = lax.broadcasted_iota(jnp.int32, (SUBLANES, ch), 0)

    def body(g, carry):
        gi = groups - 1 - g if reverse else g
        r0 = pl.multiple_of(gi * SUBLANES, SUBLANES)
        a = a_ref[pl.ds(r0, SUBLANES), :]
        u = u_ref[pl.ds(r0, SUBLANES), :]
        for d in (1, 2, 4):
            if reverse:
                shift, keep = SUBLANES - d, row < SUBLANES - d
            else:
                shift, keep = d, row >= d
            a_nb = pltpu.roll(a, shift, 0)
            u_nb = pltpu.roll(u, shift, 0)
            u = u + a * jnp.where(keep, u_nb, 0.0)
            a = a * jnp.where(keep, a_nb, 1.0)
        h = u + a * carry
        u_ref[pl.ds(r0, SUBLANES), :] = h
        edge = h[0:1] if reverse else h[SUBLANES - 1:SUBLANES]
        return jnp.broadcast_to(edge, (SUBLANES, ch))

    lax.fori_loop(0, groups, body, jnp.zeros((SUBLANES, ch), F32), unroll=4)


def _lru_kernel(x_ref, gate_ref, cw_ref, cb_ref, wg_ref, bg_ref, lam_ref, y_ref,
                xpad_ref, af_ref, uf_ref, ab_ref, ub_ref, *, seq, ch):
    pad = SUBLANES
    zeros = jnp.zeros((pad, ch), F32)
    xpad_ref[pl.ds(0, pad), :] = zeros
    xpad_ref[pl.ds(seq + pad, pad), :] = zeros
    xpad_ref[pl.ds(pad, seq), :] = x_ref[...]

    cw = cw_ref[...]
    cb = cb_ref[...]
    bg = bg_ref[...]
    log_sig = _log_sigmoid(lam_ref[...])
    lpad = LRU_CONV_WIDTH // 2

    def gates(c, _):
        t0 = pl.multiple_of(c * LRU_ROWS, LRU_ROWS)
        win = xpad_ref[pl.ds(t0, LRU_ROWS + 2 * pad), :]
        xc = jnp.zeros((LRU_ROWS, ch), F32) + cb
        for k in range(LRU_CONV_WIDTH):
            off = pad - lpad + k
            xc = xc + cw[k:k + 1, :] * win[off:off + LRU_ROWS, :]
        pre = jnp.dot(xc, wg_ref[...], precision=HIGHEST,
                      preferred_element_type=F32) + bg
        for d, (a_ref, u_ref) in enumerate(((af_ref, uf_ref), (ab_ref, ub_ref))):
            r = _sigmoid(pre[:, (2 * d) * ch:(2 * d + 1) * ch])
            i = _sigmoid(pre[:, (2 * d + 1) * ch:(2 * d + 2) * ch])
            a = jnp.exp(LRU_C * r * log_sig[d:d + 1, :])
            a_ref[pl.ds(t0, LRU_ROWS), :] = a
            u_ref[pl.ds(t0, LRU_ROWS), :] = jnp.sqrt(1.0 - a * a) * (i * xc)
        return 0

    lax.fori_loop(0, seq // LRU_ROWS, gates, 0)

    _scan_in_place(af_ref, uf_ref, seq, ch, reverse=False)
    _scan_in_place(ab_ref, ub_ref, seq, ch, reverse=True)

    def finish(c, _):
        t0 = pl.multiple_of(c * LRU_ROWS, LRU_ROWS)
        h = uf_ref[pl.ds(t0, LRU_ROWS), :] + ub_ref[pl.ds(t0, LRU_ROWS), :]
        y_ref[pl.ds(t0, LRU_ROWS), :] = h * _gelu_tanh(gate_ref[pl.ds(t0, LRU_ROWS), :])
        return 0

    lax.fori_loop(0, seq // LRU_ROWS, finish, 0)


def _lru_branch(z3, conv_w, conv_b, wg, bg, lam, d_lru):
    b, s, _ = z3.shape
    ch = MIX_CH
    nc = d_lru // ch
    kern = functools.partial(_lru_kernel, seq=s, ch=ch)
    return pl.pallas_call(
        kern,
        grid=(b, nc),
        in_specs=[pl.BlockSpec((None, s, ch), lambda i, c: (i, 0, c)),
                  pl.BlockSpec((None, s, ch), lambda i, c: (i, 0, nc + c)),
                  pl.BlockSpec((LRU_CONV_WIDTH, ch), lambda i, c: (0, c)),
                  pl.BlockSpec((1, ch), lambda i, c: (0, c)),
                  pl.BlockSpec((None, ch, 4 * ch), lambda i, c: (c, 0, 0)),
                  pl.BlockSpec((None, 1, 4 * ch), lambda i, c: (c, 0, 0)),
                  pl.BlockSpec((2, ch), lambda i, c: (0, c))],
        out_specs=pl.BlockSpec((None, s, ch), lambda i, c: (i, 0, c)),
        out_shape=jax.ShapeDtypeStruct((b, s, d_lru), F32),
        scratch_shapes=[pltpu.VMEM((s + 2 * SUBLANES, ch), F32)]
        + [pltpu.VMEM((s, ch), F32)] * 4,
        compiler_params=_params(("parallel", "parallel"), 40),
        name="lru_branch",
    )(z3, z3, conv_w, conv_b, wg, bg, lam)


def _conf_kernel(a_ref, b_ref, cw_ref, cb_ref, ng_ref, nb_ref, avg_ref, y_ref,
                 gpad_ref, *, seq, ch):
    pad = 2 * SUBLANES
    half = CONF_KERNEL // 2
    zeros = jnp.zeros((pad, ch), F32)
    gpad_ref[pl.ds(0, pad), :] = zeros
    gpad_ref[pl.ds(seq + pad, pad), :] = zeros
    gpad_ref[pl.ds(pad, seq), :] = a_ref[...] * _sigmoid(b_ref[...])

    cw = cw_ref[...]
    cb = cb_ref[...]
    ng = ng_ref[...]
    nb = nb_ref[...]

    def chunk(c, _):
        t0 = pl.multiple_of(c * CONV_ROWS, CONV_ROWS)
        win = gpad_ref[pl.ds(t0, CONV_ROWS + 2 * pad), :]
        acc = jnp.zeros((CONV_ROWS, ch), F32) + cb
        for k in range(CONF_KERNEL):
            off = pad - half + k
            acc = acc + cw[k:k + 1, :] * win[off:off + CONV_ROWS, :]
        mu = jnp.dot(acc, avg_ref[...], precision=HIGHEST, preferred_element_type=F32)
        dev = acc - mu
        var = jnp.dot(dev * dev, avg_ref[...], precision=HIGHEST,
                      preferred_element_type=F32)
        y = dev * lax.rsqrt(var + EPS) * ng + nb
        y_ref[pl.ds(t0, CONV_ROWS), :] = y * _sigmoid(y)
        return 0

    lax.fori_loop(0, seq // CONV_ROWS, chunk, 0)


def _conf_branch(z3, conv_w, conv_b, norm_g, norm_b, avg, d_lru, d_conv):
    b, s, _ = z3.shape
    ch = MIX_CH
    nc = d_conv // ch
    base = 2 * d_lru // ch
    kern = functools.partial(_conf_kernel, seq=s, ch=ch)
    return pl.pallas_call(
        kern,
        grid=(b, nc),
        in_specs=[pl.BlockSpec((None, s, ch), lambda i, c: (i, 0, base + c)),
                  pl.BlockSpec((None, s, ch), lambda i, c: (i, 0, base + nc + c)),
                  pl.BlockSpec((CONF_KERNEL, ch), lambda i, c: (0, c)),
                  pl.BlockSpec((1, ch), lambda i, c: (0, c)),
                  pl.BlockSpec((1, ch), lambda i, c: (0, c)),
                  pl.BlockSpec((1, ch), lambda i, c: (0, c)),
                  pl.BlockSpec((ch, ch), lambda i, c: (0, 0))],
        out_specs=pl.BlockSpec((None, s, ch), lambda i, c: (i, 0, c)),
        out_shape=jax.ShapeDtypeStruct((b, s, d_conv), F32),
        scratch_shapes=[pltpu.VMEM((s + 4 * SUBLANES, ch), F32)],
        compiler_params=_params(("parallel", "parallel"), 40),
        name="conf_branch",
    )(z3, z3, conv_w, conv_b, norm_g, norm_b, avg)


def _outproj_kernel(yl_ref, yc_ref, x_ref, bl_ref, bc_ref, wl_ref, wc_ref, fg_ref,
                    h_ref, ntb_ref):
    yl = _rms(yl_ref[...], bl_ref[...]).astype(BF16)
    yc = _rms(yc_ref[...], bc_ref[...]).astype(BF16)
    h = (x_ref[...]
         + jnp.dot(yl, wl_ref[...], preferred_element_type=F32)
         + jnp.dot(yc, wc_ref[...], preferred_element_type=F32))
    h_ref[...] = h
    ntb_ref[...] = _rms(h, fg_ref[...]).T.astype(BF16)


def _outproj(yl, yc, x2, beta_l, beta_c, wl_bf, wc_bf, ffn_g):
    t, d = x2.shape
    dl = yl.shape[1]
    dc = yc.shape[1]
    row = lambda i: (i, 0)
    fixed = lambda i: (0, 0)
    return pl.pallas_call(
        _outproj_kernel,
        grid=(t // ROW_TILE,),
        in_specs=[pl.BlockSpec((ROW_TILE, dl), row),
                  pl.BlockSpec((ROW_TILE, dc), row),
                  pl.BlockSpec((ROW_TILE, d), row),
                  pl.BlockSpec((1, dl), fixed),
                  pl.BlockSpec((1, dc), fixed),
                  pl.BlockSpec((dl, d), fixed),
                  pl.BlockSpec((dc, d), fixed),
                  pl.BlockSpec((1, d), fixed)],
        out_specs=[pl.BlockSpec((ROW_TILE, d), row),
                   pl.BlockSpec((d, ROW_TILE), lambda i: (0, i))],
        out_shape=[jax.ShapeDtypeStruct((t, d), F32),
                   jax.ShapeDtypeStruct((d, t), BF16)],
        compiler_params=_params(("parallel",), 40),
        name="outproj",
    )(yl, yc, x2, beta_l, beta_c, wl_bf, wc_bf, ffn_g)


def _sort_network(n):
    pairs = []
    p = 1
    while p < n:
        k = p
        while k >= 1:
            for j in range(k % p, n - k, 2 * k):
                for i in range(min(k, n - j - k)):
                    if (i + j) // (2 * p) == (i + j + k) // (2 * p):
                        pairs.append((i + j, i + j + k))
            k //= 2
        p *= 2
    return pairs


def _pruned_network(n_pow2, n_live, n_out):
    pairs = [(i, j) for i, j in _sort_network(n_pow2) if j < n_live]
    needed = set(range(n_out))
    kept = []
    for i, j in reversed(pairs):
        if i in needed or j in needed:
            kept.append((i, j))
            needed.update((i, j))
    return kept[::-1]


def _apply_network(vals, pairs):
    vals = list(vals)
    for i, j in pairs:
        hi = jnp.maximum(vals[i], vals[j])
        lo = jnp.minimum(vals[i], vals[j])
        vals[i], vals[j] = hi, lo
    return vals


def _top16_over_keys(s):
    k = PEER_TOPK
    blocks = [s[SUBLANES * v:SUBLANES * (v + 1), :] for v in range(PEER_N_KEYS // SUBLANES)]
    top = _apply_network(blocks, _sort_network(len(blocks)))
    for d in (1, 2, 4):
        top = [jnp.maximum(top[i], pltpu.roll(top[k - 1 - i], d, 0)) for i in range(k)]
        stride = k // 2
        while stride >= 1:
            pairs = [(i, i + stride) for i in range(k) if not i & stride]
            top = _apply_network(top, pairs)
            stride //= 2
    return top


def _staircase(k):
    return [(a, b) for a in range(k) for b in range(k) if (a + 1) * (b + 1) <= k]


def _k_largest(cands, k):
    n_pow2 = 1
    while n_pow2 < len(cands):
        n_pow2 *= 2
    return _apply_network(cands, _pruned_network(n_pow2, len(cands), k))[:k]


def _score_kernel(nt_ref, wq_ref, keys_ref, g1_ref, e2_ref, thr_ref,
                  s_ref, top_ref, z_ref, *, tok):
    k = PEER_TOPK
    nt = nt_ref[...]

    def per_half(hp, _):
        h = hp // 2
        p = hp % 2
        r0 = pl.multiple_of(hp * PEER_N_KEYS, PEER_N_KEYS)
        q = jnp.dot(wq_ref[pl.ds(r0, PEER_N_KEYS), :], nt,
                    preferred_element_type=F32)
        s = jnp.dot(keys_ref[hp], q, precision=HIGHEST,
                    preferred_element_type=F32)
        s_ref[p, h] = s
        top = _top16_over_keys(s)
        for i in range(k):
            top_ref[p, i, pl.ds(h, 1), :] = top[i][0:1, :]
        return 0

    lax.fori_loop(0, 2 * PEER_HEADS, per_half, 0)

    first = [top_ref[0, i] for i in range(k)]
    second = [top_ref[1, i] for i in range(k)]
    best = _k_largest([first[a] + second[b] for a, b in _staircase(k)], k)
    z = jnp.zeros_like(best[0])
    for c in best:
        z = z + jnp.exp(c - best[0])
    z_ref[...] = z
    inv_z = 1.0 / z
    g1_top = [jnp.exp(first[a] - first[0]) * inv_z for a in range(k)]
    e2_top = [jnp.exp(second[b] - second[0]) for b in range(k)]
    thr_ref[...] = _k_largest([g1_top[a] * e2_top[b] for a, b in _staircase(k)], k)[k - 1]

    def gates(h, _):
        inv_zh = 1.0 / z_ref[pl.ds(h, 1), :]
        m1 = top_ref[0, 0, pl.ds(h, 1), :]
        m2 = top_ref[1, 0, pl.ds(h, 1), :]
        g1_ref[h] = jnp.exp(s_ref[0, h] - m1) * inv_zh
        e2_ref[h] = jnp.exp(s_ref[1, h] - m2)
        return 0

    lax.fori_loop(0, PEER_HEADS, gates, 0)


def _peer_scores(nt_bf, wq_t_bf, keys):
    d, t = nt_bf.shape
    tok = SCORE_TOK
    kern = functools.partial(_score_kernel, tok=tok)
    big = jax.ShapeDtypeStruct((PEER_HEADS, PEER_N_KEYS, t), F32)
    big_spec = pl.BlockSpec((PEER_HEADS, PEER_N_KEYS, tok), lambda i: (0, 0, i))
    return pl.pallas_call(
        kern,
        grid=(t // tok,),
        in_specs=[pl.BlockSpec((d, tok), lambda i: (0, i)),
                  pl.BlockSpec(wq_t_bf.shape, lambda i: (0, 0)),
                  pl.BlockSpec(keys.shape, lambda i: (0, 0, 0))],
        out_specs=[big_spec, big_spec,
                   pl.BlockSpec((PEER_HEADS, tok), lambda i: (0, i))],
        out_shape=[big, big, jax.ShapeDtypeStruct((PEER_HEADS, t), F32)],
        scratch_shapes=[pltpu.VMEM((2, PEER_HEADS, PEER_N_KEYS, tok), F32),
                        pltpu.VMEM((2, PEER_TOPK, PEER_HEADS, tok), F32),
                        pltpu.VMEM((PEER_HEADS, tok), F32)],
        compiler_params=_params(("parallel",), 32),
        name="peer_scores",
    )(nt_bf, wq_t_bf, keys)


def _gelu_times(x, w):
    k0 = -2.0 * 0.7978845608028654 * 1.4426950408889634
    k1 = k0 * 0.044715
    e = jnp.exp2(x * (k0 + k1 * (x * x)))
    return (x * w) / (1.0 + e)


def _peer_kernel(nt_ref, u_ref, vt_ref, g1_ref, e2_ref, thr_ref, out_ref,
                 act0_ref, act1_ref, a0_ref, a1_ref, *, tok, n_exp, k_steps):
    g = pl.program_id(0)
    rows_per_step = n_exp // PEER_N_KEYS

    @pl.when(g == 0)
    def _():
        act1_ref[...] = jnp.zeros_like(act1_ref)
        a0_ref[...] = jnp.zeros_like(a0_ref)

    @pl.when(jnp.logical_or(g < 2, lax.rem(jnp.maximum(g - 2, 0), k_steps) == 0))
    def _():
        out_ref[...] = jnp.zeros_like(out_ref)

    kb = lax.rem(jnp.maximum(g - 1, 0), k_steps)
    i0 = pl.multiple_of(kb * rows_per_step, rows_per_step)

    def body(act_w, act_r, a_w, a_r):
        d = out_ref.shape[0]
        n_slices = rows_per_step // 2
        for sl in range(n_slices):
            er = slice(sl * n_exp // n_slices, (sl + 1) * n_exp // n_slices)
            act_w[er, :] = jnp.dot(u_ref[er, :], nt_ref[...], preferred_element_type=F32)
            for ii in range(2 * sl, 2 * sl + 2):
                for lc in range(tok // LANES):
                    cols = slice(lc * LANES, (lc + 1) * LANES)
                    for jt in range(PEER_N_KEYS // W_ROWS):
                        keys = slice(jt * W_ROWS, (jt + 1) * W_ROWS)
                        rows = slice(ii * PEER_N_KEYS + jt * W_ROWS,
                                     ii * PEER_N_KEYS + (jt + 1) * W_ROWS)
                        w = jnp.zeros((W_ROWS, LANES), F32)
                        for h in range(PEER_HEADS):
                            g1row = g1_ref[h, pl.ds(i0, rows_per_step), cols][ii:ii + 1, :]
                            p = e2_ref[h, keys, cols] * g1row
                            w = w + jnp.where(p >= thr_ref[h:h + 1, cols], p, 0.0)
                        a_w[rows, cols] = _gelu_times(act_r[rows, cols], w).astype(BF16)
            dr = slice(sl * d // n_slices, (sl + 1) * d // n_slices)
            out_ref[dr, :] += jnp.dot(vt_ref[dr, :], a_r[...], preferred_element_type=F32)

    @pl.when(lax.rem(g, 2) == 0)
    def _():
        body(act0_ref, act1_ref, a1_ref, a0_ref)

    @pl.when(lax.rem(g, 2) == 1)
    def _():
        body(act1_ref, act0_ref, a0_ref, a1_ref)


def _peer_dense(nt_bf, u_bf, vt_bf, g1, e2, thr):
    d, t = nt_bf.shape
    n_experts = u_bf.shape[0]
    tok, n_exp = PEER_TOK, PEER_EXP
    n_tok = t // tok
    k_steps = n_experts // n_exp
    kern = functools.partial(_peer_kernel, tok=tok, n_exp=n_exp, k_steps=k_steps)

    def tok_tile(lag):
        return lambda g: jnp.clip((g - lag) // k_steps, 0, n_tok - 1)

    def exp_tile(lag):
        return lambda g: jnp.maximum(g - lag, 0) % k_steps

    big_spec = pl.BlockSpec((PEER_HEADS, PEER_N_KEYS, tok), lambda g: (0, 0, tok_tile(1)(g)))
    return pl.pallas_call(
        kern,
        grid=(n_tok * k_steps + 2,),
        in_specs=[pl.BlockSpec((d, tok), lambda g: (0, tok_tile(0)(g))),
                  pl.BlockSpec((n_exp, d), lambda g: (exp_tile(0)(g), 0)),
                  pl.BlockSpec((d, n_exp), lambda g: (0, exp_tile(2)(g))),
                  big_spec, big_spec,
                  pl.BlockSpec((PEER_HEADS, tok), lambda g: (0, tok_tile(1)(g)))],
        out_specs=pl.BlockSpec((d, tok), lambda g: (0, tok_tile(2)(g))),
        out_shape=jax.ShapeDtypeStruct((d, t), F32),
        scratch_shapes=[pltpu.VMEM((n_exp, tok), F32), pltpu.VMEM((n_exp, tok), F32),
                        pltpu.VMEM((n_exp, tok), BF16), pltpu.VMEM((n_exp, tok), BF16)],
        compiler_params=_params(("arbitrary",), 48),
        name="peer_dense",
    )(nt_bf, u_bf, vt_bf, g1, e2, thr)


def _final_kernel(h_ref, pt_ref, g_ref, o_ref):
    o_ref[...] = _rms(h_ref[...] + pt_ref[...].T, g_ref[...])


def _final(h, peer_t, g):
    t, d = h.shape
    return pl.pallas_call(
        _final_kernel,
        grid=(t // ROW_TILE,),
        in_specs=[pl.BlockSpec((ROW_TILE, d), lambda i: (i, 0)),
                  pl.BlockSpec((d, ROW_TILE), lambda i: (0, i)),
                  pl.BlockSpec((1, d), lambda i: (0, 0))],
        out_specs=pl.BlockSpec((ROW_TILE, d), lambda i: (i, 0)),
        out_shape=jax.ShapeDtypeStruct((t, d), F32),
        compiler_params=_params(("parallel",), 24),
        name="final_norm",
    )(h, peer_t, g)


def _block_diag_chunks(w, ch):
    heads, hd, _ = w.shape
    per = ch // hd
    w4 = w.reshape(heads // per, per, hd, hd)
    eye = jnp.eye(per, dtype=w.dtype)
    return jnp.einsum("chij,hg->chigj", w4, eye).reshape(heads // per, ch, ch)


def _layer(h2, batch, seq, mix_norm_g, w_in, lru_conv_w, lru_conv_b, lru_w_rg, lru_b_rg,
           lru_w_ig, lru_b_ig, lru_lambda, conf_conv_w, conf_conv_b, conf_norm_g,
           conf_norm_b, beta_lru, beta_conv, w_out, ffn_norm_g, peer_w_q, peer_sub_keys,
           peer_u, peer_v):
    t, d = h2.shape
    d_lru = lru_conv_w.shape[1]
    d_conv = conf_conv_w.shape[1]
    row = lambda v: v.reshape(1, -1)

    z = _inproj(h2, row(mix_norm_g), w_in.astype(BF16))
    z3 = z.reshape(batch, seq, z.shape[1])

    ch = MIX_CH
    nc = d_lru // ch
    wg = jnp.concatenate([_block_diag_chunks(lru_w_rg[0], ch), _block_diag_chunks(lru_w_ig[0], ch),
                          _block_diag_chunks(lru_w_rg[1], ch), _block_diag_chunks(lru_w_ig[1], ch)],
                         axis=-1)
    bg = jnp.concatenate([lru_b_rg[0].reshape(nc, 1, ch), lru_b_ig[0].reshape(nc, 1, ch),
                          lru_b_rg[1].reshape(nc, 1, ch), lru_b_ig[1].reshape(nc, 1, ch)],
                         axis=-1)
    y_lru = _lru_branch(z3, lru_conv_w, row(lru_conv_b), wg, bg, lru_lambda, d_lru)

    gdim = d_conv // CONV_GROUPS
    grp = jnp.arange(ch) // gdim
    avg = (grp[:, None] == grp[None, :]).astype(F32) / gdim
    y_conv = _conf_branch(z3, conf_conv_w, row(conf_conv_b), row(conf_norm_g),
                          row(conf_norm_b), avg, d_lru, d_conv)

    w_out_bf = w_out.astype(BF16)
    h2, nt_bf = _outproj(y_lru.reshape(t, d_lru), y_conv.reshape(t, d_conv), h2,
                             row(beta_lru), row(beta_conv), w_out_bf[:d_lru], w_out_bf[d_lru:],
                             row(ffn_norm_g))

    keys = peer_sub_keys.reshape(2 * PEER_HEADS, PEER_N_KEYS, -1)
    g1, e2, thr = _peer_scores(nt_bf, peer_w_q.T.astype(BF16), keys)
    peer_t = _peer_dense(nt_bf, _cast(peer_u, BF16), _transpose_cast(peer_v, BF16),
                         g1, e2, thr)
    return h2, peer_t


def kernel(x, mix_norm_g, w_in, lru_conv_w, lru_conv_b, lru_w_rg, lru_b_rg, lru_w_ig, lru_b_ig, lru_lambda, conf_conv_w, conf_conv_b, conf_norm_g, conf_norm_b, beta_lru, beta_conv, w_out, ffn_norm_g, peer_w_q, peer_sub_keys, peer_u, peer_v, final_norm_g):
    batch, seq, d = x.shape
    depth = w_in.shape[0]
    h2 = x.reshape(batch * seq, d)
    peer_t = None
    for l in range(depth):
        if peer_t is not None:
            h2 = h2 + peer_t.T
        h2, peer_t = _layer(
            h2, batch, seq, mix_norm_g[l], w_in[l], lru_conv_w[l], lru_conv_b[l], lru_w_rg[l],
            lru_b_rg[l], lru_w_ig[l], lru_b_ig[l], lru_lambda[l], conf_conv_w[l], conf_conv_b[l],
            conf_norm_g[l], conf_norm_b[l], beta_lru[l], beta_conv[l], w_out[l], ffn_norm_g[l],
            peer_w_q[l], peer_sub_keys[l], peer_u[l], peer_v[l])
    out = _final(h2, peer_t, final_norm_g.reshape(1, -1))
    return out.reshape(batch, seq, d)
```

```python
import functools

import jax
import jax.numpy as jnp
from jax import lax
from jax.experimental import pallas as pl
from jax.experimental.pallas import tpu as pltpu

F32 = jnp.float32
BF16 = jnp.bfloat16
HIGHEST = lax.Precision.HIGHEST

SUBLANES = 8
LANES = 128
MIB = 1024 * 1024

EPS = 1e-6
LRU_C = 8.0
LRU_HEADS = 8
LRU_CONV_WIDTH = 4
CONV_GROUPS = 8
CONF_KERNEL = 31
PEER_HEADS = 8
PEER_N_KEYS = 128
PEER_TOPK = 16

ROW_TILE = 512
MIX_CH = 256
LRU_ROWS = 256
CONV_ROWS = 128
SCORE_TOK = 512
PEER_TOK = 512
PEER_EXP = 1024
W_ROWS = 32
MM_SLICES = 8


def _params(semantics, vmem_mib):
    return pltpu.CompilerParams(dimension_semantics=semantics,
                                vmem_limit_bytes=vmem_mib * MIB)


def _rms(x, g):
    return x * lax.rsqrt(jnp.mean(x * x, axis=-1, keepdims=True) + EPS) * g


def _gelu_tanh(x):
    c = 0.7978845608028654
    return x * (0.5 * (1.0 + jnp.tanh(c * (x + 0.044715 * (x * x * x)))))


def _sigmoid(x):
    return 1.0 / (1.0 + jnp.exp(-x))


def _log_sigmoid(x):
    return -(jnp.maximum(-x, 0.0) + jnp.log(1.0 + jnp.exp(-jnp.abs(x))))


def _transpose_cast_kernel(x_ref, o_ref):
    o_ref[...] = x_ref[...].T.astype(o_ref.dtype)


def _transpose_cast(x, dtype, tile=512):
    r, c = x.shape
    return pl.pallas_call(
        _transpose_cast_kernel,
        grid=(r // tile,),
        in_specs=[pl.BlockSpec((tile, c), lambda i: (i, 0))],
        out_specs=pl.BlockSpec((c, tile), lambda i: (0, i)),
        out_shape=jax.ShapeDtypeStruct((c, r), dtype),
        compiler_params=_params(("parallel",), 24),
        name="transpose_cast",
    )(x)


def _cast_kernel(x_ref, o_ref):
    o_ref[...] = x_ref[...].astype(o_ref.dtype)


def _cast(x, dtype, tile=512):
    r, c = x.shape
    return pl.pallas_call(
        _cast_kernel,
        grid=(r // tile,),
        in_specs=[pl.BlockSpec((tile, c), lambda i: (i, 0))],
        out_specs=pl.BlockSpec((tile, c), lambda i: (i, 0)),
        out_shape=jax.ShapeDtypeStruct((r, c), dtype),
        compiler_params=_params(("parallel",), 24),
        name="cast",
    )(x)


def _inproj_kernel(x_ref, g_ref, w_ref, z_ref):
    n = _rms(x_ref[...], g_ref[...])
    z_ref[...] = jnp.dot(n.astype(BF16), w_ref[...], preferred_element_type=F32)


def _inproj(x2, g, w_bf):
    t, d = x2.shape
    e = w_bf.shape[1]
    return pl.pallas_call(
        _inproj_kernel,
        grid=(t // ROW_TILE,),
        in_specs=[pl.BlockSpec((ROW_TILE, d), lambda i: (i, 0)),
                  pl.BlockSpec((1, d), lambda i: (0, 0)),
                  pl.BlockSpec((d, e), lambda i: (0, 0))],
        out_specs=pl.BlockSpec((ROW_TILE, e), lambda i: (i, 0)),
        out_shape=jax.ShapeDtypeStruct((t, e), F32),
        compiler_params=_params(("parallel",), 40),
        name="inproj",
    )(x2, g, w_bf)


def _scan_in_place(a_ref, u_ref, seq, ch, reverse):
    groups = seq // SUBLANES
    row = lax.broadcasted_iota(jnp.int32, (SUBLANES, ch), 0)

    def body(g, carry):
        gi = groups - 1 - g if reverse else g
        r0 = pl.multiple_of(gi * SUBLANES, SUBLANES)
        a = a_ref[pl.ds(r0, SUBLANES), :]
        u = u_ref[pl.ds(r0, SUBLANES), :]
        for d in (1, 2, 4):
            if reverse:
                shift, keep = SUBLANES - d, row < SUBLANES - d
            else:
                shift, keep = d, row >= d
            a_nb = pltpu.roll(a, shift, 0)
            u_nb = pltpu.roll(u, shift, 0)
            u = u + a * jnp.where(keep, u_nb, 0.0)
            a = a * jnp.where(keep, a_nb, 1.0)
        h = u + a * carry
        u_ref[pl.ds(r0, SUBLANES), :] = h
        edge = h[0:1] if reverse else h[SUBLANES - 1:SUBLANES]
        return jnp.broadcast_to(edge, (SUBLANES, ch))

    lax.fori_loop(0, groups, body, jnp.zeros((SUBLANES, ch), F32), unroll=4)


def _lru_kernel(x_ref, gate_ref, cw_ref, cb_ref, wg_ref, bg_ref, lam_ref, y_ref,
                xpad_ref, af_ref, uf_ref, ab_ref, ub_ref, *, seq, ch):
    pad = SUBLANES
    zeros = jnp.zeros((pad, ch), F32)
    xpad_ref[pl.ds(0, pad), :] = zeros
    xpad_ref[pl.ds(seq + pad, pad), :] = zeros
    xpad_ref[pl.ds(pad, seq), :] = x_ref[...]

    cw = cw_ref[...]
    cb = cb_ref[...]
    bg = bg_ref[...]
    log_sig = _log_sigmoid(lam_ref[...])
    lpad = LRU_CONV_WIDTH // 2

    def gates(c, _):
        t0 = pl.multiple_of(c * LRU_ROWS, LRU_ROWS)
        win = xpad_ref[pl.ds(t0, LRU_ROWS + 2 * pad), :]
        xc = jnp.zeros((LRU_ROWS, ch), F32) + cb
        for k in range(LRU_CONV_WIDTH):
            off = pad - lpad + k
            xc = xc + cw[k:k + 1, :] * win[off:off + LRU_ROWS, :]
        pre = jnp.dot(xc, wg_ref[...], precision=HIGHEST,
                      preferred_element_type=F32) + bg
        for d, (a_ref, u_ref) in enumerate(((af_ref, uf_ref), (ab_ref, ub_ref))):
            r = _sigmoid(pre[:, (2 * d) * ch:(2 * d + 1) * ch])
            i = _sigmoid(pre[:, (2 * d + 1) * ch:(2 * d + 2) * ch])
            a = jnp.exp(LRU_C * r * log_sig[d:d + 1, :])
            a_ref[pl.ds(t0, LRU_ROWS), :] = a
            u_ref[pl.ds(t0, LRU_ROWS), :] = jnp.sqrt(1.0 - a * a) * (i * xc)
        return 0

    lax.fori_loop(0, seq // LRU_ROWS, gates, 0)

    _scan_in_place(af_ref, uf_ref, seq, ch, reverse=False)
    _scan_in_place(ab_ref, ub_ref, seq, ch, reverse=True)

    def finish(c, _):
        t0 = pl.multiple_of(c * LRU_ROWS, LRU_ROWS)
        h = uf_ref[pl.ds(t0, LRU_ROWS), :] + ub_ref[pl.ds(t0, LRU_ROWS), :]
        y_ref[pl.ds(t0, LRU_ROWS), :] = h * _gelu_tanh(gate_ref[pl.ds(t0, LRU_ROWS), :])
        return 0

    lax.fori_loop(0, seq // LRU_ROWS, finish, 0)


def _lru_branch(z3, conv_w, conv_b, wg, bg, lam, d_lru):
    b, s, _ = z3.shape
    ch = MIX_CH
    nc = d_lru // ch
    kern = functools.partial(_lru_kernel, seq=s, ch=ch)
    return pl.pallas_call(
        kern,
        grid=(b, nc),
        in_specs=[pl.BlockSpec((None, s, ch), lambda i, c: (i, 0, c)),
                  pl.BlockSpec((None, s, ch), lambda i, c: (i, 0, nc + c)),
                  pl.BlockSpec((LRU_CONV_WIDTH, ch), lambda i, c: (0, c)),
                  pl.BlockSpec((1, ch), lambda i, c: (0, c)),
                  pl.BlockSpec((None, ch, 4 * ch), lambda i, c: (c, 0, 0)),
                  pl.BlockSpec((None, 1, 4 * ch), lambda i, c: (c, 0, 0)),
                  pl.BlockSpec((2, ch), lambda i, c: (0, c))],
        out_specs=pl.BlockSpec((None, s, ch), lambda i, c: (i, 0, c)),
        out_shape=jax.ShapeDtypeStruct((b, s, d_lru), F32),
        scratch_shapes=[pltpu.VMEM((s + 2 * SUBLANES, ch), F32)]
        + [pltpu.VMEM((s, ch), F32)] * 4,
        compiler_params=_params(("parallel", "parallel"), 40),
        name="lru_branch",
    )(z3, z3, conv_w, conv_b, wg, bg, lam)


def _conf_kernel(a_ref, b_ref, cw_ref, cb_ref, ng_ref, nb_ref, avg_ref, y_ref,
                 gpad_ref, *, seq, ch):
    pad = 2 * SUBLANES
    half = CONF_KERNEL // 2
    zeros = jnp.zeros((pad, ch), F32)
    gpad_ref[pl.ds(0, pad), :] = zeros
    gpad_ref[pl.ds(seq + pad, pad), :] = zeros
    gpad_ref[pl.ds(pad, seq), :] = a_ref[...] * _sigmoid(b_ref[...])

    cw = cw_ref[...]
    cb = cb_ref[...]
    ng = ng_ref[...]
    nb = nb_ref[...]

    def chunk(c, _):
        t0 = pl.multiple_of(c * CONV_ROWS, CONV_ROWS)
        win = gpad_ref[pl.ds(t0, CONV_ROWS + 2 * pad), :]
        acc = jnp.zeros((CONV_ROWS, ch), F32) + cb
        for k in range(CONF_KERNEL):
            off = pad - half + k
            acc = acc + cw[k:k + 1, :] * win[off:off + CONV_ROWS, :]
        mu = jnp.dot(acc, avg_ref[...], precision=HIGHEST, preferred_element_type=F32)
        dev = acc - mu
        var = jnp.dot(dev * dev, avg_ref[...], precision=HIGHEST,
                      preferred_element_type=F32)
        y = dev * lax.rsqrt(var + EPS) * ng + nb
        y_ref[pl.ds(t0, CONV_ROWS), :] = y * _sigmoid(y)
        return 0

    lax.fori_loop(0, seq // CONV_ROWS, chunk, 0)


def _conf_branch(z3, conv_w, conv_b, norm_g, norm_b, avg, d_lru, d_conv):
    b, s, _ = z3.shape
    ch = MIX_CH
    nc = d_conv // ch
    base = 2 * d_lru // ch
    kern = functools.partial(_conf_kernel, seq=s, ch=ch)
    return pl.pallas_call(
        kern,
        grid=(b, nc),
        in_specs=[pl.BlockSpec((None, s, ch), lambda i, c: (i, 0, base + c)),
                  pl.BlockSpec((None, s, ch), lambda i, c: (i, 0, base + nc + c)),
                  pl.BlockSpec((CONF_KERNEL, ch), lambda i, c: (0, c)),
                  pl.BlockSpec((1, ch), lambda i, c: (0, c)),
                  pl.BlockSpec((1, ch), lambda i, c: (0, c)),
                  pl.BlockSpec((1, ch), lambda i, c: (0, c)),
                  pl.BlockSpec((ch, ch), lambda i, c: (0, 0))],
        out_specs=pl.BlockSpec((None, s, ch), lambda i, c: (i, 0, c)),
        out_shape=jax.ShapeDtypeStruct((b, s, d_conv), F32),
        scratch_shapes=[pltpu.VMEM((s + 4 * SUBLANES, ch), F32)],
        compiler_params=_params(("parallel", "parallel"), 40),
        name="conf_branch",
    )(z3, z3, conv_w, conv_b, norm_g, norm_b, avg)


def _outproj_kernel(yl_ref, yc_ref, x_ref, bl_ref, bc_ref, wl_ref, wc_ref, fg_ref,
                    h_ref, ntb_ref):
    yl = _rms(yl_ref[...], bl_ref[...]).astype(BF16)
    yc = _rms(yc_ref[...], bc_ref[...]).astype(BF16)
    h = (x_ref[...]
         + jnp.dot(yl, wl_ref[...], preferred_element_type=F32)
         + jnp.dot(yc, wc_ref[...], preferred_element_type=F32))
    h_ref[...] = h
    ntb_ref[...] = _rms(h, fg_ref[...]).T.astype(BF16)


def _outproj(yl, yc, x2, beta_l, beta_c, wl_bf, wc_bf, ffn_g):
    t, d = x2.shape
    dl = yl.shape[1]
    dc = yc.shape[1]
    row = lambda i: (i, 0)
    fixed = lambda i: (0, 0)
    return pl.pallas_call(
        _outproj_kernel,
        grid=(t // ROW_TILE,),
        in_specs=[pl.BlockSpec((ROW_TILE, dl), row),
                  pl.BlockSpec((ROW_TILE, dc), row),
                  pl.BlockSpec((ROW_TILE, d), row),
                  pl.BlockSpec((1, dl), fixed),
                  pl.BlockSpec((1, dc), fixed),
                  pl.BlockSpec((dl, d), fixed),
                  pl.BlockSpec((dc, d), fixed),
                  pl.BlockSpec((1, d), fixed)],
        out_specs=[pl.BlockSpec((ROW_TILE, d), row),
                   pl.BlockSpec((d, ROW_TILE), lambda i: (0, i))],
        out_shape=[jax.ShapeDtypeStruct((t, d), F32),
                   jax.ShapeDtypeStruct((d, t), BF16)],
        compiler_params=_params(("parallel",), 40),
        name="outproj",
    )(yl, yc, x2, beta_l, beta_c, wl_bf, wc_bf, ffn_g)


def _sort_network(n):
    pairs = []
    p = 1
    while p < n:
        k = p
        while k >= 1:
            for j in range(k % p, n - k, 2 * k):
                for i in range(min(k, n - j - k)):
                    if (i + j) // (2 * p) == (i + j + k) // (2 * p):
                        pairs.append((i + j, i + j + k))
            k //= 2
        p *= 2
    return pairs


def _pruned_network(n_pow2, n_live, n_out):
    pairs = [(i, j) for i, j in _sort_network(n_pow2) if j < n_live]
    needed = set(range(n_out))
    kept = []
    for i, j in reversed(pairs):
        if i in needed or j in needed:
            kept.append((i, j))
            needed.update((i, j))
    return kept[::-1]


def _apply_network(vals, pairs):
    vals = list(vals)
    for i, j in pairs:
        hi = jnp.maximum(vals[i], vals[j])
        lo = jnp.minimum(vals[i], vals[j])
        vals[i], vals[j] = hi, lo
    return vals


def _top16_over_keys(s):
    k = PEER_TOPK
    blocks = [s[SUBLANES * v:SUBLANES * (v + 1), :] for v in range(PEER_N_KEYS // SUBLANES)]
    top = _apply_network(blocks, _sort_network(len(blocks)))
    for d in (1, 2, 4):
        top = [jnp.maximum(top[i], pltpu.roll(top[k - 1 - i], d, 0)) for i in range(k)]
        stride = k // 2
        while stride >= 1:
            pairs = [(i, i + stride) for i in range(k) if not i & stride]
            top = _apply_network(top, pairs)
            stride //= 2
    return top


def _staircase(k):
    return [(a, b) for a in range(k) for b in range(k) if (a + 1) * (b + 1) <= k]


def _k_largest(cands, k):
    n_pow2 = 1
    while n_pow2 < len(cands):
        n_pow2 *= 2
    return _apply_network(cands, _pruned_network(n_pow2, len(cands), k))[:k]


def _score_kernel(nt_ref, wq_ref, keys_ref, g1_ref, e2_ref, thr_ref,
                  q_ref, s_ref, top_ref, z_ref, *, tok):
    k = PEER_TOPK
    chunks = [(c, slice(c * LANES, (c + 1) * LANES)) for c in range(tok // LANES)]

    q_ref[...] = jnp.dot(wq_ref[...], nt_ref[...], preferred_element_type=F32)

    def per_half(hp, _):
        h = hp // 2
        p = hp % 2
        r0 = pl.multiple_of(hp * PEER_N_KEYS, PEER_N_KEYS)
        s = jnp.dot(keys_ref[hp], q_ref[pl.ds(r0, PEER_N_KEYS), :], precision=HIGHEST,
                    preferred_element_type=F32)
        s_ref[p, h] = s
        for c, cols in chunks:
            top = _top16_over_keys(s[:, cols])
            for i in range(k):
                top_ref[p, i, c, pl.ds(h, 1), :] = top[i][0:1, :]
        return 0

    lax.fori_loop(0, 2 * PEER_HEADS, per_half, 0)

    for c, cols in chunks:
        first = [top_ref[0, i, c] for i in range(k)]
        second = [top_ref[1, i, c] for i in range(k)]
        best = _k_largest([first[a] + second[b] for a, b in _staircase(k)], k)
        z = jnp.zeros_like(best[0])
        for v in best:
            z = z + jnp.exp(v - best[0])
        z_ref[c] = z
        inv_z = 1.0 / z
        g1_top = [jnp.exp(first[a] - first[0]) * inv_z for a in range(k)]
        e2_top = [jnp.exp(second[b] - second[0]) for b in range(k)]
        thr_ref[:, cols] = _k_largest([g1_top[a] * e2_top[b] for a, b in _staircase(k)], k)[k - 1]

    def per_head(h, _):
        for c, cols in chunks:
            inv_z = 1.0 / z_ref[c, pl.ds(h, 1), :]
            m1 = top_ref[0, 0, c, pl.ds(h, 1), :]
            m2 = top_ref[1, 0, c, pl.ds(h, 1), :]
            g1_ref[h, :, cols] = jnp.exp(s_ref[0, h, :, cols] - m1) * inv_z
            e2_ref[h, :, cols] = jnp.exp(s_ref[1, h, :, cols] - m2)
        return 0

    lax.fori_loop(0, PEER_HEADS, per_head, 0)


def _peer_scores(nt_bf, wq_t_bf, keys):
    d, t = nt_bf.shape
    tok = SCORE_TOK
    kern = functools.partial(_score_kernel, tok=tok)
    shape = (PEER_HEADS, PEER_N_KEYS, t)
    big_spec = pl.BlockSpec((PEER_HEADS, PEER_N_KEYS, tok), lambda i: (0, 0, i))
    return pl.pallas_call(
        kern,
        grid=(t // tok,),
        in_specs=[pl.BlockSpec((d, tok), lambda i: (0, i)),
                  pl.BlockSpec(wq_t_bf.shape, lambda i: (0, 0)),
                  pl.BlockSpec(keys.shape, lambda i: (0, 0, 0))],
        out_specs=[big_spec, big_spec,
                   pl.BlockSpec((PEER_HEADS, tok), lambda i: (0, i))],
        out_shape=[jax.ShapeDtypeStruct(shape, F32), jax.ShapeDtypeStruct(shape, F32),
                   jax.ShapeDtypeStruct((PEER_HEADS, t), F32)],
        scratch_shapes=[pltpu.VMEM((wq_t_bf.shape[0], tok), F32),
                        pltpu.VMEM((2, PEER_HEADS, PEER_N_KEYS, tok), F32),
                        pltpu.VMEM((2, PEER_TOPK, tok // LANES, PEER_HEADS, LANES), F32),
                        pltpu.VMEM((tok // LANES, PEER_HEADS, LANES), F32)],
        compiler_params=_params(("parallel",), 48),
        name="peer_scores",
    )(nt_bf, wq_t_bf, keys)


def _gelu_times(x, w):
    k0 = -2.0 * 0.7978845608028654 * 1.4426950408889634
    k1 = k0 * 0.044715
    e = jnp.exp2(x * (k0 + k1 * (x * x)))
    return (x * w) / (1.0 + e)


def _peer_kernel(nt_ref, u_ref, vt_ref, g1_ref, e2_ref, thr_ref, out_ref,
                 act0_ref, act1_ref, a0_ref, a1_ref, *, tok, n_exp, k_steps):
    g = pl.program_id(0)
    rows_per_step = n_exp // PEER_N_KEYS

    @pl.when(g == 0)
    def _():
        act1_ref[...] = jnp.zeros_like(act1_ref)
        a0_ref[...] = jnp.zeros_like(a0_ref)

    @pl.when(jnp.logical_or(g < 2, lax.rem(jnp.maximum(g - 2, 0), k_steps) == 0))
    def _():
        out_ref[...] = jnp.zeros_like(out_ref)

    kb = lax.rem(jnp.maximum(g - 1, 0), k_steps)
    i0 = pl.multiple_of(kb * rows_per_step, rows_per_step)

    def body(act_w, act_r, a_w, a_r):
        d = out_ref.shape[0]
        per_slice = rows_per_step // MM_SLICES
        for sl in range(MM_SLICES):
            er = slice(sl * n_exp // MM_SLICES, (sl + 1) * n_exp // MM_SLICES)
            act_w[er, :] = jnp.dot(u_ref[er, :], nt_ref[...], preferred_element_type=F32)
            for ii in range(per_slice * sl, per_slice * (sl + 1)):
                for lc in range(tok // LANES):
                    cols = slice(lc * LANES, (lc + 1) * LANES)
                    for jt in range(PEER_N_KEYS // W_ROWS):
                        keys = slice(jt * W_ROWS, (jt + 1) * W_ROWS)
                        rows = slice(ii * PEER_N_KEYS + jt * W_ROWS,
                                     ii * PEER_N_KEYS + (jt + 1) * W_ROWS)
                        w = jnp.zeros((W_ROWS, LANES), F32)
                        for h in range(PEER_HEADS):
                            g1row = g1_ref[h, pl.ds(i0, rows_per_step), cols][ii:ii + 1, :]
                            p = e2_ref[h, keys, cols] * g1row
                            w = w + jnp.where(p >= thr_ref[h:h + 1, cols], p, 0.0)
                        a_w[rows, cols] = _gelu_times(act_r[rows, cols], w).astype(BF16)
            dr = slice(sl * d // MM_SLICES, (sl + 1) * d // MM_SLICES)
            out_ref[dr, :] += jnp.dot(vt_ref[dr, :], a_r[...], preferred_element_type=F32)

    @pl.when(lax.rem(g, 2) == 0)
    def _():
        body(act0_ref, act1_ref, a1_ref, a0_ref)

    @pl.when(lax.rem(g, 2) == 1)
    def _():
        body(act1_ref, act0_ref, a0_ref, a1_ref)


def _peer_dense(nt_bf, u_bf, vt_bf, g1, e2, thr):
    d, t = nt_bf.shape
    n_experts = u_bf.shape[0]
    tok, n_exp = PEER_TOK, PEER_EXP
    n_tok = t // tok
    k_steps = n_experts // n_exp
    kern = functools.partial(_peer_kernel, tok=tok, n_exp=n_exp, k_steps=k_steps)

    def tok_tile(lag):
        return lambda g: jnp.clip((g - lag) // k_steps, 0, n_tok - 1)

    def exp_tile(lag):
        return lambda g: jnp.maximum(g - lag, 0) % k_steps

    big_spec = pl.BlockSpec((PEER_HEADS, PEER_N_KEYS, tok), lambda g: (0, 0, tok_tile(1)(g)))
    return pl.pallas_call(
        kern,
        grid=(n_tok * k_steps + 2,),
        in_specs=[pl.BlockSpec((d, tok), lambda g: (0, tok_tile(0)(g))),
                  pl.BlockSpec((n_exp, d), lambda g: (exp_tile(0)(g), 0)),
                  pl.BlockSpec((d, n_exp), lambda g: (0, exp_tile(2)(g))),
                  big_spec, big_spec,
                  pl.BlockSpec((PEER_HEADS, tok), lambda g: (0, tok_tile(1)(g)))],
        out_specs=pl.BlockSpec((d, tok), lambda g: (0, tok_tile(2)(g))),
        out_shape=jax.ShapeDtypeStruct((d, t), F32),
        scratch_shapes=[pltpu.VMEM((n_exp, tok), F32), pltpu.VMEM((n_exp, tok), F32),
                        pltpu.VMEM((n_exp, tok), BF16), pltpu.VMEM((n_exp, tok), BF16)],
        compiler_params=_params(("arbitrary",), 48),
        name="peer_dense",
    )(nt_bf, u_bf, vt_bf, g1, e2, thr)


def _final_kernel(h_ref, pt_ref, g_ref, o_ref):
    o_ref[...] = _rms(h_ref[...] + pt_ref[...].T, g_ref[...])


def _final(h, peer_t, g):
    t, d = h.shape
    return pl.pallas_call(
        _final_kernel,
        grid=(t // ROW_TILE,),
        in_specs=[pl.BlockSpec((ROW_TILE, d), lambda i: (i, 0)),
                  pl.BlockSpec((d, ROW_TILE), lambda i: (0, i)),
                  pl.BlockSpec((1, d), lambda i: (0, 0))],
        out_specs=pl.BlockSpec((ROW_TILE, d), lambda i: (i, 0)),
        out_shape=jax.ShapeDtypeStruct((t, d), F32),
        compiler_params=_params(("parallel",), 24),
        name="final_norm",
    )(h, peer_t, g)


def _block_diag_chunks(w, ch):
    heads, hd, _ = w.shape
    per = ch // hd
    w4 = w.reshape(heads // per, per, hd, hd)
    eye = jnp.eye(per, dtype=w.dtype)
    return jnp.einsum("chij,hg->chigj", w4, eye).reshape(heads // per, ch, ch)


def _layer(h2, batch, seq, mix_norm_g, w_in, lru_conv_w, lru_conv_b, lru_w_rg, lru_b_rg,
           lru_w_ig, lru_b_ig, lru_lambda, conf_conv_w, conf_conv_b, conf_norm_g,
           conf_norm_b, beta_lru, beta_conv, w_out, ffn_norm_g, peer_w_q, peer_sub_keys,
           peer_u, peer_v):
    t, d = h2.shape
    d_lru = lru_conv_w.shape[1]
    d_conv = conf_conv_w.shape[1]
    row = lambda v: v.reshape(1, -1)

    z = _inproj(h2, row(mix_norm_g), w_in.astype(BF16))
    z3 = z.reshape(batch, seq, z.shape[1])

    ch = MIX_CH
    nc = d_lru // ch
    wg = jnp.concatenate([_block_diag_chunks(lru_w_rg[0], ch), _block_diag_chunks(lru_w_ig[0], ch),
                          _block_diag_chunks(lru_w_rg[1], ch), _block_diag_chunks(lru_w_ig[1], ch)],
                         axis=-1)
    bg = jnp.concatenate([lru_b_rg[0].reshape(nc, 1, ch), lru_b_ig[0].reshape(nc, 1, ch),
                          lru_b_rg[1].reshape(nc, 1, ch), lru_b_ig[1].reshape(nc, 1, ch)],
                         axis=-1)
    y_lru = _lru_branch(z3, lru_conv_w, row(lru_conv_b), wg, bg, lru_lambda, d_lru)

    gdim = d_conv // CONV_GROUPS
    grp = jnp.arange(ch) // gdim
    avg = (grp[:, None] == grp[None, :]).astype(F32) / gdim
    y_conv = _conf_branch(z3, conf_conv_w, row(conf_conv_b), row(conf_norm_g),
                          row(conf_norm_b), avg, d_lru, d_conv)

    w_out_bf = w_out.astype(BF16)
    h2, nt_bf = _outproj(y_lru.reshape(t, d_lru), y_conv.reshape(t, d_conv), h2,
                             row(beta_lru), row(beta_conv), w_out_bf[:d_lru], w_out_bf[d_lru:],
                             row(ffn_norm_g))

    keys = peer_sub_keys.reshape(2 * PEER_HEADS, PEER_N_KEYS, -1)
    g1, e2, thr = _peer_scores(nt_bf, peer_w_q.T.astype(BF16), keys)
    peer_t = _peer_dense(nt_bf, _cast(peer_u, BF16), _transpose_cast(peer_v, BF16),
                         g1, e2, thr)
    return h2, peer_t


def kernel(x, mix_norm_g, w_in, lru_conv_w, lru_conv_b, lru_w_rg, lru_b_rg, lru_w_ig, lru_b_ig, lru_lambda, conf_conv_w, conf_conv_b, conf_norm_g, conf_norm_b, beta_lru, beta_conv, w_out, ffn_norm_g, peer_w_q, peer_sub_keys, peer_u, peer_v, final_norm_g):
    batch, seq, d = x.shape
    depth = w_in.shape[0]
    h2 = x.reshape(batch * seq, d)
    peer_t = None
    for l in range(depth):
        if peer_t is not None:
            h2 = h2 + peer_t.T
        h2, peer_t = _layer(
            h2, batch, seq, mix_norm_g[l], w_in[l], lru_conv_w[l], lru_conv_b[l], lru_w_rg[l],
            lru_b_rg[l], lru_w_ig[l], lru_b_ig[l], lru_lambda[l], conf_conv_w[l], conf_conv_b[l],
            conf_norm_g[l], conf_norm_b[l], beta_lru[l], beta_conv[l], w_out[l], ffn_norm_g[l],
            peer_w_q[l], peer_sub_keys[l], peer_u[l], peer_v[l])
    out = _final(h2, peer_t, final_norm_g.reshape(1, -1))
    return out.reshape(batch, seq, d)
```

```python
import functools

import jax
import jax.numpy as jnp
from jax import lax
from jax.experimental import pallas as pl
from jax.experimental.pallas import tpu as pltpu

F32 = jnp.float32
BF16 = jnp.bfloat16
HIGHEST = lax.Precision.HIGHEST

SUBLANES = 8
LANES = 128
MIB = 1024 * 1024

EPS = 1e-6
LRU_C = 8.0
LRU_HEADS = 8
LRU_CONV_WIDTH = 4
CONV_GROUPS = 8
CONF_KERNEL = 31
PEER_HEADS = 8
PEER_N_KEYS = 128
PEER_TOPK = 16

ROW_TILE = 512
MIX_CH = 256
LRU_ROWS = 128
CONV_ROWS = 128
SCORE_TOK = 512
PEER_TOK = 512
PEER_EXP = 1024
W_ROWS = 32
MM_SLICES = 4


def _params(semantics, vmem_mib):
    return pltpu.CompilerParams(dimension_semantics=semantics,
                                vmem_limit_bytes=vmem_mib * MIB)


def _rms(x, g):
    return x * lax.rsqrt(jnp.mean(x * x, axis=-1, keepdims=True) + EPS) * g


def _gelu_tanh(x):
    c = 0.7978845608028654
    return x * (0.5 * (1.0 + jnp.tanh(c * (x + 0.044715 * (x * x * x)))))


def _sigmoid(x):
    return 1.0 / (1.0 + jnp.exp(-x))


def _log_sigmoid(x):
    return -(jnp.maximum(-x, 0.0) + jnp.log(1.0 + jnp.exp(-jnp.abs(x))))


def _transpose_cast_kernel(x_ref, o_ref):
    o_ref[...] = x_ref[...].T.astype(o_ref.dtype)


def _transpose_cast(x, dtype, tile=512):
    r, c = x.shape
    return pl.pallas_call(
        _transpose_cast_kernel,
        grid=(r // tile,),
        in_specs=[pl.BlockSpec((tile, c), lambda i: (i, 0))],
        out_specs=pl.BlockSpec((c, tile), lambda i: (0, i)),
        out_shape=jax.ShapeDtypeStruct((c, r), dtype),
        compiler_params=_params(("parallel",), 24),
        name="transpose_cast",
    )(x)


def _cast_kernel(x_ref, o_ref):
    o_ref[...] = x_ref[...].astype(o_ref.dtype)


def _cast(x, dtype, tile=512):
    r, c = x.shape
    return pl.pallas_call(
        _cast_kernel,
        grid=(r // tile,),
        in_specs=[pl.BlockSpec((tile, c), lambda i: (i, 0))],
        out_specs=pl.BlockSpec((tile, c), lambda i: (i, 0)),
        out_shape=jax.ShapeDtypeStruct((r, c), dtype),
        compiler_params=_params(("parallel",), 24),
        name="cast",
    )(x)


def _inproj_kernel(x_ref, g_ref, w_ref, z_ref):
    n = _rms(x_ref[...], g_ref[...])
    z_ref[...] = jnp.dot(n.astype(BF16), w_ref[...], preferred_element_type=F32)


def _inproj(x2, g, w_bf):
    t, d = x2.shape
    e = w_bf.shape[1]
    return pl.pallas_call(
        _inproj_kernel,
        grid=(t // ROW_TILE,),
        in_specs=[pl.BlockSpec((ROW_TILE, d), lambda i: (i, 0)),
                  pl.BlockSpec((1, d), lambda i: (0, 0)),
                  pl.BlockSpec((d, e), lambda i: (0, 0))],
        out_specs=pl.BlockSpec((ROW_TILE, e), lambda i: (i, 0)),
        out_shape=jax.ShapeDtypeStruct((t, e), F32),
        compiler_params=_params(("parallel",), 40),
        name="inproj",
    )(x2, g, w_bf)


def _scan_group(a_ref, u_ref, r0, carry, row, reverse):
    a = a_ref[pl.ds(r0, SUBLANES), :]
    u = u_ref[pl.ds(r0, SUBLANES), :]
    for d in (1, 2, 4):
        if reverse:
            shift, keep = SUBLANES - d, row < SUBLANES - d
        else:
            shift, keep = d, row >= d
        a_nb = pltpu.roll(a, shift, 0)
        u_nb = pltpu.roll(u, shift, 0)
        u = u + a * jnp.where(keep, u_nb, 0.0)
        a = a * jnp.where(keep, a_nb, 1.0)
    h = u + a * carry
    u_ref[pl.ds(r0, SUBLANES), :] = h
    edge = h[0:1] if reverse else h[SUBLANES - 1:SUBLANES]
    return jnp.broadcast_to(edge, h.shape)


def _scans_in_place(af_ref, uf_ref, ab_ref, ub_ref, seq, ch):
    groups = seq // SUBLANES
    row = lax.broadcasted_iota(jnp.int32, (SUBLANES, ch), 0)

    def body(g, carries):
        cf, cb = carries
        rf = pl.multiple_of(g * SUBLANES, SUBLANES)
        rb = pl.multiple_of((groups - 1 - g) * SUBLANES, SUBLANES)
        return (_scan_group(af_ref, uf_ref, rf, cf, row, False),
                _scan_group(ab_ref, ub_ref, rb, cb, row, True))

    zero = jnp.zeros((SUBLANES, ch), F32)
    lax.fori_loop(0, groups, body, (zero, zero), unroll=4)


def _lru_kernel(x_ref, gate_ref, cw_ref, cb_ref, wg_ref, bg_ref, lam_ref, y_ref,
                xpad_ref, af_ref, uf_ref, ab_ref, ub_ref, *, seq, ch):
    pad = SUBLANES
    zeros = jnp.zeros((pad, ch), F32)
    xpad_ref[pl.ds(0, pad), :] = zeros
    xpad_ref[pl.ds(seq + pad, pad), :] = zeros
    xpad_ref[pl.ds(pad, seq), :] = x_ref[...]

    cw = cw_ref[...]
    cb = cb_ref[...]
    bg = bg_ref[...]
    log_sig = _log_sigmoid(lam_ref[...])
    lpad = LRU_CONV_WIDTH // 2

    def gates(c, _):
        t0 = pl.multiple_of(c * LRU_ROWS, LRU_ROWS)
        win = xpad_ref[pl.ds(t0, LRU_ROWS + 2 * pad), :]
        xc = jnp.zeros((LRU_ROWS, ch), F32) + cb
        for k in range(LRU_CONV_WIDTH):
            off = pad - lpad + k
            xc = xc + cw[k:k + 1, :] * win[off:off + LRU_ROWS, :]
        pre = jnp.dot(xc.astype(BF16), wg_ref[...], preferred_element_type=F32) + bg
        for d, (a_ref, u_ref) in enumerate(((af_ref, uf_ref), (ab_ref, ub_ref))):
            r = _sigmoid(pre[:, (2 * d) * ch:(2 * d + 1) * ch])
            i = _sigmoid(pre[:, (2 * d + 1) * ch:(2 * d + 2) * ch])
            a = jnp.exp(LRU_C * r * log_sig[d:d + 1, :])
            a_ref[pl.ds(t0, LRU_ROWS), :] = a
            u_ref[pl.ds(t0, LRU_ROWS), :] = jnp.sqrt(1.0 - a * a) * (i * xc)
        return 0

    lax.fori_loop(0, seq // LRU_ROWS, gates, 0)

    _scans_in_place(af_ref, uf_ref, ab_ref, ub_ref, seq, ch)

    def finish(c, _):
        t0 = pl.multiple_of(c * LRU_ROWS, LRU_ROWS)
        h = uf_ref[pl.ds(t0, LRU_ROWS), :] + ub_ref[pl.ds(t0, LRU_ROWS), :]
        y_ref[pl.ds(t0, LRU_ROWS), :] = h * _gelu_tanh(gate_ref[pl.ds(t0, LRU_ROWS), :])
        return 0

    lax.fori_loop(0, seq // LRU_ROWS, finish, 0)


def _lru_branch(z3, conv_w, conv_b, wg, bg, lam, d_lru):
    b, s, _ = z3.shape
    ch = MIX_CH
    nc = d_lru // ch
    kern = functools.partial(_lru_kernel, seq=s, ch=ch)
    return pl.pallas_call(
        kern,
        grid=(b, nc),
        in_specs=[pl.BlockSpec((None, s, ch), lambda i, c: (i, 0, c)),
                  pl.BlockSpec((None, s, ch), lambda i, c: (i, 0, nc + c)),
                  pl.BlockSpec((LRU_CONV_WIDTH, ch), lambda i, c: (0, c)),
                  pl.BlockSpec((1, ch), lambda i, c: (0, c)),
                  pl.BlockSpec((None, ch, 4 * ch), lambda i, c: (c, 0, 0)),
                  pl.BlockSpec((None, 1, 4 * ch), lambda i, c: (c, 0, 0)),
                  pl.BlockSpec((2, ch), lambda i, c: (0, c))],
        out_specs=pl.BlockSpec((None, s, ch), lambda i, c: (i, 0, c)),
        out_shape=jax.ShapeDtypeStruct((b, s, d_lru), F32),
        scratch_shapes=[pltpu.VMEM((s + 2 * SUBLANES, ch), F32)]
        + [pltpu.VMEM((s, ch), F32)] * 4,
        compiler_params=_params(("parallel", "parallel"), 40),
        name="lru_branch",
    )(z3, z3, conv_w, conv_b, wg, bg, lam)


def _conf_kernel(a_ref, b_ref, cw_ref, cb_ref, ng_ref, nb_ref, avg_ref, y_ref,
                 gpad_ref, shift_ref, *, seq, ch):
    pad = 2 * SUBLANES
    half = CONF_KERNEL // 2
    zeros = jnp.zeros((pad, ch), F32)
    gpad_ref[pl.ds(0, pad), :] = zeros
    gpad_ref[pl.ds(seq + pad, pad), :] = zeros
    gpad_ref[pl.ds(pad, seq), :] = a_ref[...] * _sigmoid(b_ref[...])

    avg = avg_ref[...]
    span = CONV_ROWS + pad + SUBLANES

    def group_mean(v):
        hi = v.astype(BF16)
        r1 = v - hi.astype(F32)
        mid = r1.astype(BF16)
        lo = (r1 - mid.astype(F32)).astype(BF16)
        return (jnp.dot(hi, avg, preferred_element_type=F32)
                + jnp.dot(mid, avg, preferred_element_type=F32)
                + jnp.dot(lo, avg, preferred_element_type=F32))

    def chunk(c, _):
        t0 = pl.multiple_of(c * CONV_ROWS, CONV_ROWS)
        for lc in range(ch // LANES):
            cols = slice(lc * LANES, (lc + 1) * LANES)
            cw = cw_ref[:, cols]
            win = gpad_ref[pl.ds(t0, CONV_ROWS + 2 * pad), cols]
            acc = jnp.zeros((CONV_ROWS, LANES), F32) + cb_ref[:, cols]
            for b in range(SUBLANES):
                shift_ref[b] = win[b:b + span, :]
            for b in range(SUBLANES):
                for a in range(span // SUBLANES - CONV_ROWS // SUBLANES + 1):
                    k = SUBLANES * a + b - (pad - half)
                    if 0 <= k < CONF_KERNEL:
                        rows = slice(SUBLANES * a, SUBLANES * a + CONV_ROWS)
                        acc = acc + cw[k:k + 1, :] * shift_ref[b, rows, :]
            dev = acc - group_mean(acc)
            var = group_mean(dev * dev)
            y = dev * lax.rsqrt(var + EPS) * ng_ref[:, cols] + nb_ref[:, cols]
            y_ref[pl.ds(t0, CONV_ROWS), cols] = y * _sigmoid(y)
        return 0

    lax.fori_loop(0, seq // CONV_ROWS, chunk, 0)


def _conf_branch(z3, conv_w, conv_b, norm_g, norm_b, avg, d_lru, d_conv):
    b, s, _ = z3.shape
    ch = MIX_CH
    nc = d_conv // ch
    base = 2 * d_lru // ch
    kern = functools.partial(_conf_kernel, seq=s, ch=ch)
    return pl.pallas_call(
        kern,
        grid=(b, nc),
        in_specs=[pl.BlockSpec((None, s, ch), lambda i, c: (i, 0, base + c)),
                  pl.BlockSpec((None, s, ch), lambda i, c: (i, 0, base + nc + c)),
                  pl.BlockSpec((CONF_KERNEL, ch), lambda i, c: (0, c)),
                  pl.BlockSpec((1, ch), lambda i, c: (0, c)),
                  pl.BlockSpec((1, ch), lambda i, c: (0, c)),
                  pl.BlockSpec((1, ch), lambda i, c: (0, c)),
                  pl.BlockSpec((LANES, LANES), lambda i, c: (0, 0))],
        out_specs=pl.BlockSpec((None, s, ch), lambda i, c: (i, 0, c)),
        out_shape=jax.ShapeDtypeStruct((b, s, d_conv), F32),
        scratch_shapes=[pltpu.VMEM((s + 4 * SUBLANES, ch), F32),
                        pltpu.VMEM((SUBLANES, CONV_ROWS + 3 * SUBLANES, LANES), F32)],
        compiler_params=_params(("parallel", "parallel"), 40),
        name="conf_branch",
    )(z3, z3, conv_w, conv_b, norm_g, norm_b, avg)


def _outproj_kernel(yl_ref, yc_ref, x_ref, bl_ref, bc_ref, wl_ref, wc_ref, fg_ref,
                    h_ref, ntb_ref):
    yl = _rms(yl_ref[...], bl_ref[...]).astype(BF16)
    yc = _rms(yc_ref[...], bc_ref[...]).astype(BF16)
    h = (x_ref[...]
         + jnp.dot(yl, wl_ref[...], preferred_element_type=F32)
         + jnp.dot(yc, wc_ref[...], preferred_element_type=F32))
    h_ref[...] = h
    ntb_ref[...] = _rms(h, fg_ref[...]).T.astype(BF16)


def _outproj(yl, yc, x2, beta_l, beta_c, wl_bf, wc_bf, ffn_g):
    t, d = x2.shape
    dl = yl.shape[1]
    dc = yc.shape[1]
    row = lambda i: (i, 0)
    fixed = lambda i: (0, 0)
    return pl.pallas_call(
        _outproj_kernel,
        grid=(t // ROW_TILE,),
        in_specs=[pl.BlockSpec((ROW_TILE, dl), row),
                  pl.BlockSpec((ROW_TILE, dc), row),
                  pl.BlockSpec((ROW_TILE, d), row),
                  pl.BlockSpec((1, dl), fixed),
                  pl.BlockSpec((1, dc), fixed),
                  pl.BlockSpec((dl, d), fixed),
                  pl.BlockSpec((dc, d), fixed),
                  pl.BlockSpec((1, d), fixed)],
        out_specs=[pl.BlockSpec((ROW_TILE, d), row),
                   pl.BlockSpec((d, ROW_TILE), lambda i: (0, i))],
        out_shape=[jax.ShapeDtypeStruct((t, d), F32),
                   jax.ShapeDtypeStruct((d, t), BF16)],
        compiler_params=_params(("parallel",), 40),
        name="outproj",
    )(yl, yc, x2, beta_l, beta_c, wl_bf, wc_bf, ffn_g)


def _sort_network(n):
    pairs = []
    p = 1
    while p < n:
        k = p
        while k >= 1:
            for j in range(k % p, n - k, 2 * k):
                for i in range(min(k, n - j - k)):
                    if (i + j) // (2 * p) == (i + j + k) // (2 * p):
                        pairs.append((i + j, i + j + k))
            k //= 2
        p *= 2
    return pairs


def _pruned_network(n_pow2, n_live, n_out):
    pairs = [(i, j) for i, j in _sort_network(n_pow2) if j < n_live]
    needed = set(range(n_out))
    kept = []
    for i, j in reversed(pairs):
        if i in needed or j in needed:
            kept.append((i, j))
            needed.update((i, j))
    return kept[::-1]


def _apply_network(vals, pairs):
    vals = list(vals)
    for i, j in pairs:
        hi = jnp.maximum(vals[i], vals[j])
        lo = jnp.minimum(vals[i], vals[j])
        vals[i], vals[j] = hi, lo
    return vals


def _top16_over_keys(s):
    k = PEER_TOPK
    blocks = [s[SUBLANES * v:SUBLANES * (v + 1), :] for v in range(PEER_N_KEYS // SUBLANES)]
    top = _apply_network(blocks, _sort_network(len(blocks)))
    for d in (1, 2, 4):
        top = [jnp.maximum(top[i], pltpu.roll(top[k - 1 - i], d, 0)) for i in range(k)]
        stride = k // 2
        while stride >= 1:
            pairs = [(i, i + stride) for i in range(k) if not i & stride]
            top = _apply_network(top, pairs)
            stride //= 2
    return top


def _staircase(k):
    return [(a, b) for a in range(k) for b in range(k) if (a + 1) * (b + 1) <= k]


def _k_largest(cands, k):
    n_pow2 = 1
    while n_pow2 < len(cands):
        n_pow2 *= 2
    return _apply_network(cands, _pruned_network(n_pow2, len(cands), k))[:k]


def _score_kernel(nt_ref, wq_ref, keys_ref, g1_ref, e2_ref, thr_ref,
                  q_ref, s_ref, top_ref, z_ref, *, tok):
    k = PEER_TOPK
    chunks = [(c, slice(c * LANES, (c + 1) * LANES)) for c in range(tok // LANES)]

    q_ref[...] = jnp.dot(wq_ref[...], nt_ref[...], preferred_element_type=F32)

    def per_half(hp, _):
        h = hp // 2
        p = hp % 2
        r0 = pl.multiple_of(hp * PEER_N_KEYS, PEER_N_KEYS)
        s = jnp.dot(keys_ref[hp], q_ref[pl.ds(r0, PEER_N_KEYS), :], precision=HIGHEST,
                    preferred_element_type=F32)
        s_ref[p, h] = s
        for c, cols in chunks:
            top = _top16_over_keys(s[:, cols])
            for i in range(k):
                top_ref[p, i, c, pl.ds(h, 1), :] = top[i][0:1, :]
        return 0

    lax.fori_loop(0, 2 * PEER_HEADS, per_half, 0)

    for c, cols in chunks:
        first = [top_ref[0, i, c] for i in range(k)]
        second = [top_ref[1, i, c] for i in range(k)]
        best = _k_largest([first[a] + second[b] for a, b in _staircase(k)], k)
        z = jnp.zeros_like(best[0])
        for v in best:
            z = z + jnp.exp(v - best[0])
        z_ref[c] = z
        inv_z = 1.0 / z
        g1_top = [jnp.exp(first[a] - first[0]) * inv_z for a in range(k)]
        e2_top = [jnp.exp(second[b] - second[0]) for b in range(k)]
        thr_ref[:, cols] = _k_largest([g1_top[a] * e2_top[b] for a, b in _staircase(k)], k)[k - 1]

    def per_head(h, _):
        for c, cols in chunks:
            inv_z = 1.0 / z_ref[c, pl.ds(h, 1), :]
            m1 = top_ref[0, 0, c, pl.ds(h, 1), :]
            m2 = top_ref[1, 0, c, pl.ds(h, 1), :]
            g1_ref[h, :, cols] = jnp.exp(s_ref[0, h, :, cols] - m1) * inv_z
            e2_ref[h, :, cols] = jnp.exp(s_ref[1, h, :, cols] - m2)
        return 0

    lax.fori_loop(0, PEER_HEADS, per_head, 0)


def _peer_scores(nt_bf, wq_t_bf, keys):
    d, t = nt_bf.shape
    tok = SCORE_TOK
    kern = functools.partial(_score_kernel, tok=tok)
    shape = (PEER_HEADS, PEER_N_KEYS, t)
    big_spec = pl.BlockSpec((PEER_HEADS, PEER_N_KEYS, tok), lambda i: (0, 0, i))
    return pl.pallas_call(
        kern,
        grid=(t // tok,),
        in_specs=[pl.BlockSpec((d, tok), lambda i: (0, i)),
                  pl.BlockSpec(wq_t_bf.shape, lambda i: (0, 0)),
                  pl.BlockSpec(keys.shape, lambda i: (0, 0, 0))],
        out_specs=[big_spec, big_spec,
                   pl.BlockSpec((PEER_HEADS, tok), lambda i: (0, i))],
        out_shape=[jax.ShapeDtypeStruct(shape, F32), jax.ShapeDtypeStruct(shape, F32),
                   jax.ShapeDtypeStruct((PEER_HEADS, t), F32)],
        scratch_shapes=[pltpu.VMEM((wq_t_bf.shape[0], tok), F32),
                        pltpu.VMEM((2, PEER_HEADS, PEER_N_KEYS, tok), F32),
                        pltpu.VMEM((2, PEER_TOPK, tok // LANES, PEER_HEADS, LANES), F32),
                        pltpu.VMEM((tok // LANES, PEER_HEADS, LANES), F32)],
        compiler_params=_params(("parallel",), 48),
        name="peer_scores",
    )(nt_bf, wq_t_bf, keys)


def _gelu_times(x, w):
    k0 = -2.0 * 0.7978845608028654 * 1.4426950408889634
    k1 = k0 * 0.044715
    e = jnp.exp2(x * (k0 + k1 * (x * x)))
    return (x * w) / (1.0 + e)


def _peer_kernel(nt_ref, u_ref, vt_ref, g1_ref, e2_ref, thr_ref, out_ref,
                 act0_ref, act1_ref, a0_ref, a1_ref, *, tok, n_exp, k_steps):
    g = pl.program_id(0)
    rows_per_step = n_exp // PEER_N_KEYS

    @pl.when(g == 0)
    def _():
        act1_ref[...] = jnp.zeros_like(act1_ref)
        a0_ref[...] = jnp.zeros_like(a0_ref)

    @pl.when(jnp.logical_or(g < 2, lax.rem(jnp.maximum(g - 2, 0), k_steps) == 0))
    def _():
        out_ref[...] = jnp.zeros_like(out_ref)

    kb = lax.rem(jnp.maximum(g - 1, 0), k_steps)
    i0 = pl.multiple_of(kb * rows_per_step, rows_per_step)

    def body(act_w, act_r, a_w, a_r):
        d = out_ref.shape[0]
        per_slice = rows_per_step // MM_SLICES
        for sl in range(MM_SLICES):
            for ii in range(per_slice * sl, per_slice * (sl + 1)):
                for lc in range(tok // LANES):
                    cols = slice(lc * LANES, (lc + 1) * LANES)
                    for jt in range(PEER_N_KEYS // W_ROWS):
                        keys = slice(jt * W_ROWS, (jt + 1) * W_ROWS)
                        rows = slice(ii * PEER_N_KEYS + jt * W_ROWS,
                                     ii * PEER_N_KEYS + (jt + 1) * W_ROWS)
                        w = jnp.zeros((W_ROWS, LANES), F32)
                        for h in range(PEER_HEADS):
                            g1row = g1_ref[h, pl.ds(i0, rows_per_step), cols][ii:ii + 1, :]
                            p = e2_ref[h, keys, cols] * g1row
                            w = w + jnp.where(p >= thr_ref[h:h + 1, cols], p, 0.0)
                        a_w[rows, cols] = _gelu_times(act_r[rows, cols], w).astype(BF16)
            er = slice(sl * n_exp // MM_SLICES, (sl + 1) * n_exp // MM_SLICES)
            act_w[er, :] = jnp.dot(u_ref[er, :], nt_ref[...], preferred_element_type=F32)
            dr = slice(sl * d // MM_SLICES, (sl + 1) * d // MM_SLICES)
            out_ref[dr, :] += jnp.dot(vt_ref[dr, :], a_r[...], preferred_element_type=F32)

    @pl.when(lax.rem(g, 2) == 0)
    def _():
        body(act0_ref, act1_ref, a1_ref, a0_ref)

    @pl.when(lax.rem(g, 2) == 1)
    def _():
        body(act1_ref, act0_ref, a0_ref, a1_ref)


def _peer_dense(nt_bf, u_bf, vt_bf, g1, e2, thr):
    d, t = nt_bf.shape
    n_experts = u_bf.shape[0]
    tok, n_exp = PEER_TOK, PEER_EXP
    n_tok = t // tok
    k_steps = n_experts // n_exp
    kern = functools.partial(_peer_kernel, tok=tok, n_exp=n_exp, k_steps=k_steps)

    def tok_tile(lag):
        return lambda g: jnp.clip((g - lag) // k_steps, 0, n_tok - 1)

    def exp_tile(lag):
        return lambda g: jnp.maximum(g - lag, 0) % k_steps

    big_spec = pl.BlockSpec((PEER_HEADS, PEER_N_KEYS, tok), lambda g: (0, 0, tok_tile(1)(g)))
    return pl.pallas_call(
        kern,
        grid=(n_tok * k_steps + 2,),
        in_specs=[pl.BlockSpec((d, tok), lambda g: (0, tok_tile(0)(g))),
                  pl.BlockSpec((n_exp, d), lambda g: (exp_tile(0)(g), 0)),
                  pl.BlockSpec((d, n_exp), lambda g: (0, exp_tile(2)(g))),
                  big_spec, big_spec,
                  pl.BlockSpec((PEER_HEADS, tok), lambda g: (0, tok_tile(1)(g)))],
        out_specs=pl.BlockSpec((d, tok), lambda g: (0, tok_tile(2)(g))),
        out_shape=jax.ShapeDtypeStruct((d, t), F32),
        scratch_shapes=[pltpu.VMEM((n_exp, tok), F32), pltpu.VMEM((n_exp, tok), F32),
                        pltpu.VMEM((n_exp, tok), BF16), pltpu.VMEM((n_exp, tok), BF16)],
        compiler_params=_params(("arbitrary",), 48),
        name="peer_dense",
    )(nt_bf, u_bf, vt_bf, g1, e2, thr)


def _final_kernel(h_ref, pt_ref, g_ref, o_ref):
    o_ref[...] = _rms(h_ref[...] + pt_ref[...].T, g_ref[...])


def _final(h, peer_t, g):
    t, d = h.shape
    return pl.pallas_call(
        _final_kernel,
        grid=(t // ROW_TILE,),
        in_specs=[pl.BlockSpec((ROW_TILE, d), lambda i: (i, 0)),
                  pl.BlockSpec((d, ROW_TILE), lambda i: (0, i)),
                  pl.BlockSpec((1, d), lambda i: (0, 0))],
        out_specs=pl.BlockSpec((ROW_TILE, d), lambda i: (i, 0)),
        out_shape=jax.ShapeDtypeStruct((t, d), F32),
        compiler_params=_params(("parallel",), 24),
        name="final_norm",
    )(h, peer_t, g)


def _block_diag_chunks(w, ch):
    heads, hd, _ = w.shape
    per = ch // hd
    w4 = w.reshape(heads // per, per, hd, hd)
    eye = jnp.eye(per, dtype=w.dtype)
    return jnp.einsum("chij,hg->chigj", w4, eye).reshape(heads // per, ch, ch)


def _layer(h2, batch, seq, mix_norm_g, w_in, lru_conv_w, lru_conv_b, lru_w_rg, lru_b_rg,
           lru_w_ig, lru_b_ig, lru_lambda, conf_conv_w, conf_conv_b, conf_norm_g,
           conf_norm_b, beta_lru, beta_conv, w_out, ffn_norm_g, peer_w_q, peer_sub_keys,
           peer_u, peer_v):
    t, d = h2.shape
    d_lru = lru_conv_w.shape[1]
    d_conv = conf_conv_w.shape[1]
    row = lambda v: v.reshape(1, -1)

    z = _inproj(h2, row(mix_norm_g), w_in.astype(BF16))
    z3 = z.reshape(batch, seq, z.shape[1])

    ch = MIX_CH
    nc = d_lru // ch
    wg = jnp.concatenate([_block_diag_chunks(lru_w_rg[0], ch), _block_diag_chunks(lru_w_ig[0], ch),
                          _block_diag_chunks(lru_w_rg[1], ch), _block_diag_chunks(lru_w_ig[1], ch)],
                         axis=-1)
    bg = jnp.concatenate([lru_b_rg[0].reshape(nc, 1, ch), lru_b_ig[0].reshape(nc, 1, ch),
                          lru_b_rg[1].reshape(nc, 1, ch), lru_b_ig[1].reshape(nc, 1, ch)],
                         axis=-1)
    y_lru = _lru_branch(z3, lru_conv_w, row(lru_conv_b), wg.astype(BF16), bg, lru_lambda, d_lru)

    gdim = d_conv // CONV_GROUPS
    grp = jnp.arange(LANES) // gdim
    avg = ((grp[:, None] == grp[None, :]).astype(F32) / gdim).astype(BF16)
    y_conv = _conf_branch(z3, conf_conv_w, row(conf_conv_b), row(conf_norm_g),
                          row(conf_norm_b), avg, d_lru, d_conv)

    w_out_bf = w_out.astype(BF16)
    h2, nt_bf = _outproj(y_lru.reshape(t, d_lru), y_conv.reshape(t, d_conv), h2,
                             row(beta_lru), row(beta_conv), w_out_bf[:d_lru], w_out_bf[d_lru:],
                             row(ffn_norm_g))

    keys = peer_sub_keys.reshape(2 * PEER_HEADS, PEER_N_KEYS, -1)
    g1, e2, thr = _peer_scores(nt_bf, peer_w_q.T.astype(BF16), keys)
    peer_t = _peer_dense(nt_bf, _cast(peer_u, BF16), _transpose_cast(peer_v, BF16),
                         g1, e2, thr)
    return h2, peer_t


def kernel(x, mix_norm_g, w_in, lru_conv_w, lru_conv_b, lru_w_rg, lru_b_rg, lru_w_ig, lru_b_ig, lru_lambda, conf_conv_w, conf_conv_b, conf_norm_g, conf_norm_b, beta_lru, beta_conv, w_out, ffn_norm_g, peer_w_q, peer_sub_keys, peer_u, peer_v, final_norm_g):
    batch, seq, d = x.shape
    depth = w_in.shape[0]
    h2 = x.reshape(batch * seq, d)
    peer_t = None
    for l in range(depth):
        if peer_t is not None:
            h2 = h2 + peer_t.T
        h2, peer_t = _layer(
            h2, batch, seq, mix_norm_g[l], w_in[l], lru_conv_w[l], lru_conv_b[l], lru_w_rg[l],
            lru_b_rg[l], lru_w_ig[l], lru_b_ig[l], lru_lambda[l], conf_conv_w[l], conf_conv_b[l],
            conf_norm_g[l], conf_norm_b[l], beta_lru[l], beta_conv[l], w_out[l], ffn_norm_g[l],
            peer_w_q[l], peer_sub_keys[l], peer_u[l], peer_v[l])
    out = _final(h2, peer_t, final_norm_g.reshape(1, -1))
    return out.reshape(batch, seq, d)
```

```python
import functools

import jax
import jax.numpy as jnp
from jax import lax
from jax.experimental import pallas as pl
from jax.experimental.pallas import tpu as pltpu

F32 = jnp.float32
BF16 = jnp.bfloat16
HIGHEST = lax.Precision.HIGHEST

SUBLANES = 8
LANES = 128
MIB = 1024 * 1024

EPS = 1e-6
LRU_C = 8.0
LRU_HEADS = 8
LRU_CONV_WIDTH = 4
CONV_GROUPS = 8
CONF_KERNEL = 31
PEER_HEADS = 8
PEER_N_KEYS = 128
PEER_TOPK = 16

ROW_TILE = 512
MIX_CH = 256
LRU_ROWS = 128
CONV_ROWS = 128
SCORE_TOK = 512
PEER_TOK = 512
PEER_EXP = 1024
W_ROWS = 32
MXU_COUNT = 2
MXU_TILE = 256
MM_ROWS = 64
POP_LAG_UNITS = 8


def _params(semantics, vmem_mib):
    return pltpu.CompilerParams(dimension_semantics=semantics,
                                vmem_limit_bytes=vmem_mib * MIB)


def _rms(x, g):
    return x * lax.rsqrt(jnp.mean(x * x, axis=-1, keepdims=True) + EPS) * g


def _gelu_tanh(x):
    c = 0.7978845608028654
    return x * (0.5 * (1.0 + jnp.tanh(c * (x + 0.044715 * (x * x * x)))))


def _sigmoid(x):
    return 1.0 / (1.0 + jnp.exp(-x))


def _log_sigmoid(x):
    return -(jnp.maximum(-x, 0.0) + jnp.log(1.0 + jnp.exp(-jnp.abs(x))))


def _transpose_cast_kernel(x_ref, o_ref):
    o_ref[...] = x_ref[...].T.astype(o_ref.dtype)


def _transpose_cast(x, dtype, tile=512):
    r, c = x.shape
    return pl.pallas_call(
        _transpose_cast_kernel,
        grid=(r // tile,),
        in_specs=[pl.BlockSpec((tile, c), lambda i: (i, 0))],
        out_specs=pl.BlockSpec((c, tile), lambda i: (0, i)),
        out_shape=jax.ShapeDtypeStruct((c, r), dtype),
        compiler_params=_params(("parallel",), 24),
        name="transpose_cast",
    )(x)


def _cast_kernel(x_ref, o_ref):
    o_ref[...] = x_ref[...].astype(o_ref.dtype)


def _cast(x, dtype, tile=512):
    r, c = x.shape
    return pl.pallas_call(
        _cast_kernel,
        grid=(r // tile,),
        in_specs=[pl.BlockSpec((tile, c), lambda i: (i, 0))],
        out_specs=pl.BlockSpec((tile, c), lambda i: (i, 0)),
        out_shape=jax.ShapeDtypeStruct((r, c), dtype),
        compiler_params=_params(("parallel",), 24),
        name="cast",
    )(x)


def _inproj_kernel(x_ref, g_ref, w_ref, z_ref):
    n = _rms(x_ref[...], g_ref[...])
    z_ref[...] = jnp.dot(n.astype(BF16), w_ref[...], preferred_element_type=F32)


def _inproj(x2, g, w_bf):
    t, d = x2.shape
    e = w_bf.shape[1]
    return pl.pallas_call(
        _inproj_kernel,
        grid=(t // ROW_TILE,),
        in_specs=[pl.BlockSpec((ROW_TILE, d), lambda i: (i, 0)),
                  pl.BlockSpec((1, d), lambda i: (0, 0)),
                  pl.BlockSpec((d, e), lambda i: (0, 0))],
        out_specs=pl.BlockSpec((ROW_TILE, e), lambda i: (i, 0)),
        out_shape=jax.ShapeDtypeStruct((t, e), F32),
        compiler_params=_params(("parallel",), 40),
        name="inproj",
    )(x2, g, w_bf)


def _scan_group(a_ref, u_ref, r0, carry, row, reverse):
    a = a_ref[pl.ds(r0, SUBLANES), :]
    u = u_ref[pl.ds(r0, SUBLANES), :]
    for d in (1, 2, 4):
        if reverse:
            shift, keep = SUBLANES - d, row < SUBLANES - d
        else:
            shift, keep = d, row >= d
        a_nb = pltpu.roll(a, shift, 0)
        u_nb = pltpu.roll(u, shift, 0)
        u = u + a * jnp.where(keep, u_nb, 0.0)
        a = a * jnp.where(keep, a_nb, 1.0)
    h = u + a * carry
    u_ref[pl.ds(r0, SUBLANES), :] = h
    edge = h[0:1] if reverse else h[SUBLANES - 1:SUBLANES]
    return jnp.broadcast_to(edge, h.shape)


def _scans_in_place(af_ref, uf_ref, ab_ref, ub_ref, seq, ch):
    groups = seq // SUBLANES
    row = lax.broadcasted_iota(jnp.int32, (SUBLANES, ch), 0)

    def body(g, carries):
        cf, cb = carries
        rf = pl.multiple_of(g * SUBLANES, SUBLANES)
        rb = pl.multiple_of((groups - 1 - g) * SUBLANES, SUBLANES)
        return (_scan_group(af_ref, uf_ref, rf, cf, row, False),
                _scan_group(ab_ref, ub_ref, rb, cb, row, True))

    zero = jnp.zeros((SUBLANES, ch), F32)
    lax.fori_loop(0, groups, body, (zero, zero), unroll=4)


def _lru_kernel(x_ref, gate_ref, cw_ref, cb_ref, wg_ref, bg_ref, lam_ref, y_ref,
                xpad_ref, af_ref, uf_ref, ab_ref, ub_ref, *, seq, ch):
    pad = SUBLANES
    zeros = jnp.zeros((pad, ch), F32)
    xpad_ref[pl.ds(0, pad), :] = zeros
    xpad_ref[pl.ds(seq + pad, pad), :] = zeros
    xpad_ref[pl.ds(pad, seq), :] = x_ref[...]

    cw = cw_ref[...]
    cb = cb_ref[...]
    bg = bg_ref[...]
    log_sig = _log_sigmoid(lam_ref[...])
    lpad = LRU_CONV_WIDTH // 2

    def gates(c, _):
        t0 = pl.multiple_of(c * LRU_ROWS, LRU_ROWS)
        win = xpad_ref[pl.ds(t0, LRU_ROWS + 2 * pad), :]
        xc = jnp.zeros((LRU_ROWS, ch), F32) + cb
        for k in range(LRU_CONV_WIDTH):
            off = pad - lpad + k
            xc = xc + cw[k:k + 1, :] * win[off:off + LRU_ROWS, :]
        pre = jnp.dot(xc.astype(BF16), wg_ref[...], preferred_element_type=F32) + bg
        for d, (a_ref, u_ref) in enumerate(((af_ref, uf_ref), (ab_ref, ub_ref))):
            r = _sigmoid(pre[:, (2 * d) * ch:(2 * d + 1) * ch])
            i = _sigmoid(pre[:, (2 * d + 1) * ch:(2 * d + 2) * ch])
            a = jnp.exp(LRU_C * r * log_sig[d:d + 1, :])
            a_ref[pl.ds(t0, LRU_ROWS), :] = a
            u_ref[pl.ds(t0, LRU_ROWS), :] = jnp.sqrt(1.0 - a * a) * (i * xc)
        return 0

    lax.fori_loop(0, seq // LRU_ROWS, gates, 0)

    _scans_in_place(af_ref, uf_ref, ab_ref, ub_ref, seq, ch)

    def finish(c, _):
        t0 = pl.multiple_of(c * LRU_ROWS, LRU_ROWS)
        h = uf_ref[pl.ds(t0, LRU_ROWS), :] + ub_ref[pl.ds(t0, LRU_ROWS), :]
        y_ref[pl.ds(t0, LRU_ROWS), :] = h * _gelu_tanh(gate_ref[pl.ds(t0, LRU_ROWS), :])
        return 0

    lax.fori_loop(0, seq // LRU_ROWS, finish, 0)


def _lru_branch(z3, conv_w, conv_b, wg, bg, lam, d_lru):
    b, s, _ = z3.shape
    ch = MIX_CH
    nc = d_lru // ch
    kern = functools.partial(_lru_kernel, seq=s, ch=ch)
    return pl.pallas_call(
        kern,
        grid=(b, nc),
        in_specs=[pl.BlockSpec((None, s, ch), lambda i, c: (i, 0, c)),
                  pl.BlockSpec((None, s, ch), lambda i, c: (i, 0, nc + c)),
                  pl.BlockSpec((LRU_CONV_WIDTH, ch), lambda i, c: (0, c)),
                  pl.BlockSpec((1, ch), lambda i, c: (0, c)),
                  pl.BlockSpec((None, ch, 4 * ch), lambda i, c: (c, 0, 0)),
                  pl.BlockSpec((None, 1, 4 * ch), lambda i, c: (c, 0, 0)),
                  pl.BlockSpec((2, ch), lambda i, c: (0, c))],
        out_specs=pl.BlockSpec((None, s, ch), lambda i, c: (i, 0, c)),
        out_shape=jax.ShapeDtypeStruct((b, s, d_lru), F32),
        scratch_shapes=[pltpu.VMEM((s + 2 * SUBLANES, ch), F32)]
        + [pltpu.VMEM((s, ch), F32)] * 4,
        compiler_params=_params(("parallel", "parallel"), 40),
        name="lru_branch",
    )(z3, z3, conv_w, conv_b, wg, bg, lam)


def _conf_kernel(a_ref, b_ref, cw_ref, cb_ref, ng_ref, nb_ref, avg_ref, y_ref,
                 gpad_ref, shift_ref, *, seq, ch):
    pad = 2 * SUBLANES
    half = CONF_KERNEL // 2
    zeros = jnp.zeros((pad, ch), F32)
    gpad_ref[pl.ds(0, pad), :] = zeros
    gpad_ref[pl.ds(seq + pad, pad), :] = zeros
    gpad_ref[pl.ds(pad, seq), :] = a_ref[...] * _sigmoid(b_ref[...])

    avg = avg_ref[...]
    span = CONV_ROWS + pad + SUBLANES

    def group_mean(v):
        hi = v.astype(BF16)
        r1 = v - hi.astype(F32)
        mid = r1.astype(BF16)
        lo = (r1 - mid.astype(F32)).astype(BF16)
        return (jnp.dot(hi, avg, preferred_element_type=F32)
                + jnp.dot(mid, avg, preferred_element_type=F32)
                + jnp.dot(lo, avg, preferred_element_type=F32))

    def chunk(c, _):
        t0 = pl.multiple_of(c * CONV_ROWS, CONV_ROWS)
        for lc in range(ch // LANES):
            cols = slice(lc * LANES, (lc + 1) * LANES)
            cw = cw_ref[:, cols]
            win = gpad_ref[pl.ds(t0, CONV_ROWS + 2 * pad), cols]
            acc = jnp.zeros((CONV_ROWS, LANES), F32) + cb_ref[:, cols]
            for b in range(SUBLANES):
                shift_ref[b] = win[b:b + span, :]
            for b in range(SUBLANES):
                for a in range(span // SUBLANES - CONV_ROWS // SUBLANES + 1):
                    k = SUBLANES * a + b - (pad - half)
                    if 0 <= k < CONF_KERNEL:
                        rows = slice(SUBLANES * a, SUBLANES * a + CONV_ROWS)
                        acc = acc + cw[k:k + 1, :] * shift_ref[b, rows, :]
            dev = acc - group_mean(acc)
            var = group_mean(dev * dev)
            y = dev * lax.rsqrt(var + EPS) * ng_ref[:, cols] + nb_ref[:, cols]
            y_ref[pl.ds(t0, CONV_ROWS), cols] = y * _sigmoid(y)
        return 0

    lax.fori_loop(0, seq // CONV_ROWS, chunk, 0)


def _conf_branch(z3, conv_w, conv_b, norm_g, norm_b, avg, d_lru, d_conv):
    b, s, _ = z3.shape
    ch = MIX_CH
    nc = d_conv // ch
    base = 2 * d_lru // ch
    kern = functools.partial(_conf_kernel, seq=s, ch=ch)
    return pl.pallas_call(
        kern,
        grid=(b, nc),
        in_specs=[pl.BlockSpec((None, s, ch), lambda i, c: (i, 0, base + c)),
                  pl.BlockSpec((None, s, ch), lambda i, c: (i, 0, base + nc + c)),
                  pl.BlockSpec((CONF_KERNEL, ch), lambda i, c: (0, c)),
                  pl.BlockSpec((1, ch), lambda i, c: (0, c)),
                  pl.BlockSpec((1, ch), lambda i, c: (0, c)),
                  pl.BlockSpec((1, ch), lambda i, c: (0, c)),
                  pl.BlockSpec((LANES, LANES), lambda i, c: (0, 0))],
        out_specs=pl.BlockSpec((None, s, ch), lambda i, c: (i, 0, c)),
        out_shape=jax.ShapeDtypeStruct((b, s, d_conv), F32),
        scratch_shapes=[pltpu.VMEM((s + 4 * SUBLANES, ch), F32),
                        pltpu.VMEM((SUBLANES, CONV_ROWS + 3 * SUBLANES, LANES), F32)],
        compiler_params=_params(("parallel", "parallel"), 40),
        name="conf_branch",
    )(z3, z3, conv_w, conv_b, norm_g, norm_b, avg)


def _outproj_kernel(yl_ref, yc_ref, x_ref, bl_ref, bc_ref, wl_ref, wc_ref, fg_ref,
                    h_ref, ntb_ref):
    yl = _rms(yl_ref[...], bl_ref[...]).astype(BF16)
    yc = _rms(yc_ref[...], bc_ref[...]).astype(BF16)
    h = (x_ref[...]
         + jnp.dot(yl, wl_ref[...], preferred_element_type=F32)
         + jnp.dot(yc, wc_ref[...], preferred_element_type=F32))
    h_ref[...] = h
    ntb_ref[...] = _rms(h, fg_ref[...]).T.astype(BF16)


def _outproj(yl, yc, x2, beta_l, beta_c, wl_bf, wc_bf, ffn_g):
    t, d = x2.shape
    dl = yl.shape[1]
    dc = yc.shape[1]
    row = lambda i: (i, 0)
    fixed = lambda i: (0, 0)
    return pl.pallas_call(
        _outproj_kernel,
        grid=(t // ROW_TILE,),
        in_specs=[pl.BlockSpec((ROW_TILE, dl), row),
                  pl.BlockSpec((ROW_TILE, dc), row),
                  pl.BlockSpec((ROW_TILE, d), row),
                  pl.BlockSpec((1, dl), fixed),
                  pl.BlockSpec((1, dc), fixed),
                  pl.BlockSpec((dl, d), fixed),
                  pl.BlockSpec((dc, d), fixed),
                  pl.BlockSpec((1, d), fixed)],
        out_specs=[pl.BlockSpec((ROW_TILE, d), row),
                   pl.BlockSpec((d, ROW_TILE), lambda i: (0, i))],
        out_shape=[jax.ShapeDtypeStruct((t, d), F32),
                   jax.ShapeDtypeStruct((d, t), BF16)],
        compiler_params=_params(("parallel",), 40),
        name="outproj",
    )(yl, yc, x2, beta_l, beta_c, wl_bf, wc_bf, ffn_g)


def _sort_network(n):
    pairs = []
    p = 1
    while p < n:
        k = p
        while k >= 1:
            for j in range(k % p, n - k, 2 * k):
                for i in range(min(k, n - j - k)):
                    if (i + j) // (2 * p) == (i + j + k) // (2 * p):
                        pairs.append((i + j, i + j + k))
            k //= 2
        p *= 2
    return pairs


def _pruned_network(n_pow2, n_live, n_out):
    pairs = [(i, j) for i, j in _sort_network(n_pow2) if j < n_live]
    needed = set(range(n_out))
    kept = []
    for i, j in reversed(pairs):
        if i in needed or j in needed:
            kept.append((i, j))
            needed.update((i, j))
    return kept[::-1]


def _apply_network(vals, pairs):
    vals = list(vals)
    for i, j in pairs:
        hi = jnp.maximum(vals[i], vals[j])
        lo = jnp.minimum(vals[i], vals[j])
        vals[i], vals[j] = hi, lo
    return vals


def _top16_over_keys(s):
    k = PEER_TOPK
    blocks = [s[SUBLANES * v:SUBLANES * (v + 1), :] for v in range(PEER_N_KEYS // SUBLANES)]
    top = _apply_network(blocks, _sort_network(len(blocks)))
    for d in (1, 2, 4):
        top = [jnp.maximum(top[i], pltpu.roll(top[k - 1 - i], d, 0)) for i in range(k)]
        stride = k // 2
        while stride >= 1:
            pairs = [(i, i + stride) for i in range(k) if not i & stride]
            top = _apply_network(top, pairs)
            stride //= 2
    return top


def _staircase(k):
    return [(a, b) for a in range(k) for b in range(k) if (a + 1) * (b + 1) <= k]


def _k_largest(cands, k):
    n_pow2 = 1
    while n_pow2 < len(cands):
        n_pow2 *= 2
    return _apply_network(cands, _pruned_network(n_pow2, len(cands), k))[:k]


def _score_kernel(nt_ref, wq_ref, keys_ref, g1_ref, e2_ref, thr_ref,
                  q_ref, s_ref, top_ref, z_ref, *, tok):
    k = PEER_TOPK
    chunks = [(c, slice(c * LANES, (c + 1) * LANES)) for c in range(tok // LANES)]

    q_ref[...] = jnp.dot(wq_ref[...], nt_ref[...], preferred_element_type=F32)

    def per_half(hp, _):
        h = hp // 2
        p = hp % 2
        r0 = pl.multiple_of(hp * PEER_N_KEYS, PEER_N_KEYS)
        s = jnp.dot(keys_ref[hp], q_ref[pl.ds(r0, PEER_N_KEYS), :], precision=HIGHEST,
                    preferred_element_type=F32)
        s_ref[p, h] = s
        for c, cols in chunks:
            top = _top16_over_keys(s[:, cols])
            for i in range(k):
                top_ref[p, i, c, pl.ds(h, 1), :] = top[i][0:1, :]
        return 0

    lax.fori_loop(0, 2 * PEER_HEADS, per_half, 0)

    for c, cols in chunks:
        first = [top_ref[0, i, c] for i in range(k)]
        second = [top_ref[1, i, c] for i in range(k)]
        best = _k_largest([first[a] + second[b] for a, b in _staircase(k)], k)
        z = jnp.zeros_like(best[0])
        for v in best:
            z = z + jnp.exp(v - best[0])
        z_ref[c] = z
        inv_z = 1.0 / z
        g1_top = [jnp.exp(first[a] - first[0]) * inv_z for a in range(k)]
        e2_top = [jnp.exp(second[b] - second[0]) for b in range(k)]
        thr_ref[:, cols] = _k_largest([g1_top[a] * e2_top[b] for a, b in _staircase(k)], k)[k - 1]

    def per_head(h, _):
        for c, cols in chunks:
            inv_z = 1.0 / z_ref[c, pl.ds(h, 1), :]
            m1 = top_ref[0, 0, c, pl.ds(h, 1), :]
            m2 = top_ref[1, 0, c, pl.ds(h, 1), :]
            g1_ref[h, :, cols] = jnp.exp(s_ref[0, h, :, cols] - m1) * inv_z
            e2_ref[h, :, cols] = jnp.exp(s_ref[1, h, :, cols] - m2)
        return 0

    lax.fori_loop(0, PEER_HEADS, per_head, 0)


def _peer_scores(nt_bf, wq_t_bf, keys):
    d, t = nt_bf.shape
    tok = SCORE_TOK
    kern = functools.partial(_score_kernel, tok=tok)
    shape = (PEER_HEADS, PEER_N_KEYS, t)
    big_spec = pl.BlockSpec((PEER_HEADS, PEER_N_KEYS, tok), lambda i: (0, 0, i))
    return pl.pallas_call(
        kern,
        grid=(t // tok,),
        in_specs=[pl.BlockSpec((d, tok), lambda i: (0, i)),
                  pl.BlockSpec(wq_t_bf.shape, lambda i: (0, 0)),
                  pl.BlockSpec(keys.shape, lambda i: (0, 0, 0))],
        out_specs=[big_spec, big_spec,
                   pl.BlockSpec((PEER_HEADS, tok), lambda i: (0, i))],
        out_shape=[jax.ShapeDtypeStruct(shape, F32), jax.ShapeDtypeStruct(shape, F32),
                   jax.ShapeDtypeStruct((PEER_HEADS, t), F32)],
        scratch_shapes=[pltpu.VMEM((wq_t_bf.shape[0], tok), F32),
                        pltpu.VMEM((2, PEER_HEADS, PEER_N_KEYS, tok), F32),
                        pltpu.VMEM((2, PEER_TOPK, tok // LANES, PEER_HEADS, LANES), F32),
                        pltpu.VMEM((tok // LANES, PEER_HEADS, LANES), F32)],
        compiler_params=_params(("parallel",), 48),
        name="peer_scores",
    )(nt_bf, wq_t_bf, keys)


def _gelu_times(x, w):
    k0 = -2.0 * 0.7978845608028654 * 1.4426950408889634
    k1 = k0 * 0.044715
    e = jnp.exp2(x * (k0 + k1 * (x * x)))
    return (x * w) / (1.0 + e)


def _peer_kernel(nt_ref, u_ref, vt_ref, g1_ref, e2_ref, thr_ref, out_ref,
                 act0_ref, act1_ref, a0_ref, a1_ref, *, tok, n_exp, k_steps):
    g = pl.program_id(0)
    rows_per_step = n_exp // PEER_N_KEYS

    @pl.when(g == 0)
    def _():
        act1_ref[...] = jnp.zeros_like(act1_ref)
        a0_ref[...] = jnp.zeros_like(a0_ref)

    @pl.when(jnp.logical_or(g < 2, lax.rem(jnp.maximum(g - 2, 0), k_steps) == 0))
    def _():
        out_ref[...] = jnp.zeros_like(out_ref)

    kb = lax.rem(jnp.maximum(g - 1, 0), k_steps)
    i0 = pl.multiple_of(kb * rows_per_step, rows_per_step)

    def body(act_w, act_r, a_w, a_r):
        d = out_ref.shape[0]
        assert tok == MXU_COUNT * MXU_TILE and d % MXU_TILE == 0 and n_exp % MXU_TILE == 0
        k_tiles = {0: d // MXU_TILE, 1: n_exp // MXU_TILE}
        m_slices = {0: n_exp // MXU_TILE, 1: d // MXU_TILE}
        chunks = MXU_TILE // MM_ROWS
        acc_entries = MXU_TILE // 4

        def rhs_tile(mm, k, q):
            src = nt_ref if mm == 0 else a_r
            return src[k * MXU_TILE:(k + 1) * MXU_TILE, q * MXU_TILE:(q + 1) * MXU_TILE]

        def unit(mm, s, k, c):
            lhs_ref = u_ref if mm == 0 else vt_ref
            base = (2 * mm + s % 2) * acc_entries
            r0 = s * MXU_TILE + c * MM_ROWS
            lhs = lhs_ref[r0:r0 + MM_ROWS, k * MXU_TILE:(k + 1) * MXU_TILE]
            reg = k % 2
            for q in range(MXU_COUNT):
                if c == 0 and k == 0:
                    pltpu.matmul_push_rhs(rhs_tile(mm, 0, q), staging_register=reg, mxu_index=q)
                pltpu.matmul_acc_lhs(base + c * (MM_ROWS // 4), lhs, mxu_index=q,
                                     load_staged_rhs=reg if c == 0 else None)
                if c == 0 and k + 1 < k_tiles[mm]:
                    pltpu.matmul_push_rhs(rhs_tile(mm, k + 1, q), staging_register=1 - reg,
                                          mxu_index=q)

        def drain(mm, s):
            base = (2 * mm + s % 2) * acc_entries
            rows = slice(s * MXU_TILE, (s + 1) * MXU_TILE)
            for q in range(MXU_COUNT):
                cols = slice(q * MXU_TILE, (q + 1) * MXU_TILE)
                res = pltpu.matmul_pop(base, (MXU_TILE, MXU_TILE), F32, q)
                if mm == 0:
                    act_w[rows, cols] = res
                else:
                    out_ref[rows, cols] += res

        units = [(mm, s, k, c) for s in range(max(m_slices.values())) for mm in (0, 1)
                 if s < m_slices[mm] for k in range(k_tiles[mm]) for c in range(chunks)]
        pending = []

        def issue(idx):
            if idx < len(units):
                mm, s, k, c = units[idx]
                unit(mm, s, k, c)
                if k == k_tiles[mm] - 1 and c == chunks - 1:
                    pending.append((idx + POP_LAG_UNITS, mm, s))
            while pending and pending[0][0] <= idx:
                _, mm, s = pending.pop(0)
                drain(mm, s)

        slot = 0
        for ii in range(rows_per_step):
            for lc in range(tok // LANES):
                cols = slice(lc * LANES, (lc + 1) * LANES)
                for jt in range(PEER_N_KEYS // W_ROWS):
                    keys = slice(jt * W_ROWS, (jt + 1) * W_ROWS)
                    rows = slice(ii * PEER_N_KEYS + jt * W_ROWS,
                                 ii * PEER_N_KEYS + (jt + 1) * W_ROWS)
                    w = jnp.zeros((W_ROWS, LANES), F32)
                    for h in range(PEER_HEADS):
                        g1row = g1_ref[h, pl.ds(i0, rows_per_step), cols][ii:ii + 1, :]
                        p = e2_ref[h, keys, cols] * g1row
                        w = w + jnp.where(p >= thr_ref[h:h + 1, cols], p, 0.0)
                    a_w[rows, cols] = _gelu_times(act_r[rows, cols], w).astype(BF16)
                    issue(slot)
                    slot += 1
        while slot < len(units) or pending:
            issue(slot)
            slot += 1

    @pl.when(lax.rem(g, 2) == 0)
    def _():
        body(act0_ref, act1_ref, a1_ref, a0_ref)

    @pl.when(lax.rem(g, 2) == 1)
    def _():
        body(act1_ref, act0_ref, a0_ref, a1_ref)


def _peer_dense(nt_bf, u_bf, vt_bf, g1, e2, thr):
    d, t = nt_bf.shape
    n_experts = u_bf.shape[0]
    tok, n_exp = PEER_TOK, PEER_EXP
    n_tok = t // tok
    k_steps = n_experts // n_exp
    kern = functools.partial(_peer_kernel, tok=tok, n_exp=n_exp, k_steps=k_steps)

    def tok_tile(lag):
        return lambda g: jnp.clip((g - lag) // k_steps, 0, n_tok - 1)

    def exp_tile(lag):
        return lambda g: jnp.maximum(g - lag, 0) % k_steps

    big_spec = pl.BlockSpec((PEER_HEADS, PEER_N_KEYS, tok), lambda g: (0, 0, tok_tile(1)(g)))
    return pl.pallas_call(
        kern,
        grid=(n_tok * k_steps + 2,),
        in_specs=[pl.BlockSpec((d, tok), lambda g: (0, tok_tile(0)(g))),
                  pl.BlockSpec((n_exp, d), lambda g: (exp_tile(0)(g), 0)),
                  pl.BlockSpec((d, n_exp), lambda g: (0, exp_tile(2)(g))),
                  big_spec, big_spec,
                  pl.BlockSpec((PEER_HEADS, tok), lambda g: (0, tok_tile(1)(g)))],
        out_specs=pl.BlockSpec((d, tok), lambda g: (0, tok_tile(2)(g))),
        out_shape=jax.ShapeDtypeStruct((d, t), F32),
        scratch_shapes=[pltpu.VMEM((n_exp, tok), F32), pltpu.VMEM((n_exp, tok), F32),
                        pltpu.VMEM((n_exp, tok), BF16), pltpu.VMEM((n_exp, tok), BF16)],
        compiler_params=_params(("arbitrary",), 48),
        name="peer_dense",
    )(nt_bf, u_bf, vt_bf, g1, e2, thr)


def _final_kernel(h_ref, pt_ref, g_ref, o_ref):
    o_ref[...] = _rms(h_ref[...] + pt_ref[...].T, g_ref[...])


def _final(h, peer_t, g):
    t, d = h.shape
    return pl.pallas_call(
        _final_kernel,
        grid=(t // ROW_TILE,),
        in_specs=[pl.BlockSpec((ROW_TILE, d), lambda i: (i, 0)),
                  pl.BlockSpec((d, ROW_TILE), lambda i: (0, i)),
                  pl.BlockSpec((1, d), lambda i: (0, 0))],
        out_specs=pl.BlockSpec((ROW_TILE, d), lambda i: (i, 0)),
        out_shape=jax.ShapeDtypeStruct((t, d), F32),
        compiler_params=_params(("parallel",), 24),
        name="final_norm",
    )(h, peer_t, g)


def _block_diag_chunks(w, ch):
    heads, hd, _ = w.shape
    per = ch // hd
    w4 = w.reshape(heads // per, per, hd, hd)
    eye = jnp.eye(per, dtype=w.dtype)
    return jnp.einsum("chij,hg->chigj", w4, eye).reshape(heads // per, ch, ch)


def _layer(h2, batch, seq, mix_norm_g, w_in, lru_conv_w, lru_conv_b, lru_w_rg, lru_b_rg,
           lru_w_ig, lru_b_ig, lru_lambda, conf_conv_w, conf_conv_b, conf_norm_g,
           conf_norm_b, beta_lru, beta_conv, w_out, ffn_norm_g, peer_w_q, peer_sub_keys,
           peer_u, peer_v):
    t, d = h2.shape
    d_lru = lru_conv_w.shape[1]
    d_conv = conf_conv_w.shape[1]
    row = lambda v: v.reshape(1, -1)

    z = _inproj(h2, row(mix_norm_g), w_in.astype(BF16))
    z3 = z.reshape(batch, seq, z.shape[1])

    ch = MIX_CH
    nc = d_lru // ch
    wg = jnp.concatenate([_block_diag_chunks(lru_w_rg[0], ch), _block_diag_chunks(lru_w_ig[0], ch),
                          _block_diag_chunks(lru_w_rg[1], ch), _block_diag_chunks(lru_w_ig[1], ch)],
                         axis=-1)
    bg = jnp.concatenate([lru_b_rg[0].reshape(nc, 1, ch), lru_b_ig[0].reshape(nc, 1, ch),
                          lru_b_rg[1].reshape(nc, 1, ch), lru_b_ig[1].reshape(nc, 1, ch)],
                         axis=-1)
    y_lru = _lru_branch(z3, lru_conv_w, row(lru_conv_b), wg.astype(BF16), bg, lru_lambda, d_lru)

    gdim = d_conv // CONV_GROUPS
    grp = jnp.arange(LANES) // gdim
    avg = ((grp[:, None] == grp[None, :]).astype(F32) / gdim).astype(BF16)
    y_conv = _conf_branch(z3, conf_conv_w, row(conf_conv_b), row(conf_norm_g),
                          row(conf_norm_b), avg, d_lru, d_conv)

    w_out_bf = w_out.astype(BF16)
    h2, nt_bf = _outproj(y_lru.reshape(t, d_lru), y_conv.reshape(t, d_conv), h2,
                             row(beta_lru), row(beta_conv), w_out_bf[:d_lru], w_out_bf[d_lru:],
                             row(ffn_norm_g))

    keys = peer_sub_keys.reshape(2 * PEER_HEADS, PEER_N_KEYS, -1)
    g1, e2, thr = _peer_scores(nt_bf, peer_w_q.T.astype(BF16), keys)
    peer_t = _peer_dense(nt_bf, _cast(peer_u, BF16), _transpose_cast(peer_v, BF16),
                         g1, e2, thr)
    return h2, peer_t


def kernel(x, mix_norm_g, w_in, lru_conv_w, lru_conv_b, lru_w_rg, lru_b_rg, lru_w_ig, lru_b_ig, lru_lambda, conf_conv_w, conf_conv_b, conf_norm_g, conf_norm_b, beta_lru, beta_conv, w_out, ffn_norm_g, peer_w_q, peer_sub_keys, peer_u, peer_v, final_norm_g):
    batch, seq, d = x.shape
    depth = w_in.shape[0]
    h2 = x.reshape(batch * seq, d)
    peer_t = None
    for l in range(depth):
        if peer_t is not None:
            h2 = h2 + peer_t.T
        h2, peer_t = _layer(
            h2, batch, seq, mix_norm_g[l], w_in[l], lru_conv_w[l], lru_conv_b[l], lru_w_rg[l],
            lru_b_rg[l], lru_w_ig[l], lru_b_ig[l], lru_lambda[l], conf_conv_w[l], conf_conv_b[l],
            conf_norm_g[l], conf_norm_b[l], beta_lru[l], beta_conv[l], w_out[l], ffn_norm_g[l],
            peer_w_q[l], peer_sub_keys[l], peer_u[l], peer_v[l])
    out = _final(h2, peer_t, final_norm_g.reshape(1, -1))
    return out.reshape(batch, seq, d)
```

```python
import functools

import jax
import jax.numpy as jnp
from jax import lax
from jax.experimental import pallas as pl
from jax.experimental.pallas import tpu as pltpu

F32 = jnp.float32
BF16 = jnp.bfloat16
HIGHEST = lax.Precision.HIGHEST

SUBLANES = 8
LANES = 128
MIB = 1024 * 1024

EPS = 1e-6
LRU_C = 8.0
LRU_HEADS = 8
LRU_CONV_WIDTH = 4
CONV_GROUPS = 8
CONF_KERNEL = 31
PEER_HEADS = 8
PEER_N_KEYS = 128
PEER_TOPK = 16

ROW_TILE = 512
MIX_CH = 256
LRU_ROWS = 128
CONV_ROWS = 128
SCORE_TOK = 512
PEER_TOK = 512
PEER_EXP = 1024
W_ROWS = 8
MXU_COUNT = 2
MXU_TILE = 256
MM_ROWS = 16
POP_LAG_UNITS = 28


def _params(semantics, vmem_mib):
    return pltpu.CompilerParams(dimension_semantics=semantics,
                                vmem_limit_bytes=vmem_mib * MIB)


def _rms(x, g):
    return x * lax.rsqrt(jnp.mean(x * x, axis=-1, keepdims=True) + EPS) * g


def _gelu_tanh(x):
    c = 0.7978845608028654
    return x * (0.5 * (1.0 + jnp.tanh(c * (x + 0.044715 * (x * x * x)))))


def _sigmoid(x):
    return 1.0 / (1.0 + jnp.exp(-x))


def _log_sigmoid(x):
    return -(jnp.maximum(-x, 0.0) + jnp.log(1.0 + jnp.exp(-jnp.abs(x))))


def _transpose_cast_kernel(x_ref, o_ref):
    o_ref[...] = x_ref[...].T.astype(o_ref.dtype)


def _transpose_cast(x, dtype, tile=512):
    r, c = x.shape
    return pl.pallas_call(
        _transpose_cast_kernel,
        grid=(r // tile,),
        in_specs=[pl.BlockSpec((tile, c), lambda i: (i, 0))],
        out_specs=pl.BlockSpec((c, tile), lambda i: (0, i)),
        out_shape=jax.ShapeDtypeStruct((c, r), dtype),
        compiler_params=_params(("parallel",), 24),
        name="transpose_cast",
    )(x)


def _cast_kernel(x_ref, o_ref):
    o_ref[...] = x_ref[...].astype(o_ref.dtype)


def _cast(x, dtype, tile=512):
    r, c = x.shape
    return pl.pallas_call(
        _cast_kernel,
        grid=(r // tile,),
        in_specs=[pl.BlockSpec((tile, c), lambda i: (i, 0))],
        out_specs=pl.BlockSpec((tile, c), lambda i: (i, 0)),
        out_shape=jax.ShapeDtypeStruct((r, c), dtype),
        compiler_params=_params(("parallel",), 24),
        name="cast",
    )(x)


def _inproj_kernel(x_ref, g_ref, w_ref, z_ref):
    n = _rms(x_ref[...], g_ref[...])
    z_ref[...] = jnp.dot(n.astype(BF16), w_ref[...], preferred_element_type=F32)


def _inproj(x2, g, w_bf):
    t, d = x2.shape
    e = w_bf.shape[1]
    return pl.pallas_call(
        _inproj_kernel,
        grid=(t // ROW_TILE,),
        in_specs=[pl.BlockSpec((ROW_TILE, d), lambda i: (i, 0)),
                  pl.BlockSpec((1, d), lambda i: (0, 0)),
                  pl.BlockSpec((d, e), lambda i: (0, 0))],
        out_specs=pl.BlockSpec((ROW_TILE, e), lambda i: (i, 0)),
        out_shape=jax.ShapeDtypeStruct((t, e), F32),
        compiler_params=_params(("parallel",), 40),
        name="inproj",
    )(x2, g, w_bf)


def _scan_group(a_ref, u_ref, r0, carry, row, reverse):
    a = a_ref[pl.ds(r0, SUBLANES), :]
    u = u_ref[pl.ds(r0, SUBLANES), :]
    for d in (1, 2, 4):
        if reverse:
            shift, keep = SUBLANES - d, row < SUBLANES - d
        else:
            shift, keep = d, row >= d
        a_nb = pltpu.roll(a, shift, 0)
        u_nb = pltpu.roll(u, shift, 0)
        u = u + a * jnp.where(keep, u_nb, 0.0)
        a = a * jnp.where(keep, a_nb, 1.0)
    h = u + a * carry
    u_ref[pl.ds(r0, SUBLANES), :] = h
    edge = h[0:1] if reverse else h[SUBLANES - 1:SUBLANES]
    return jnp.broadcast_to(edge, h.shape)


def _scans_in_place(af_ref, uf_ref, ab_ref, ub_ref, seq, ch):
    groups = seq // SUBLANES
    row = lax.broadcasted_iota(jnp.int32, (SUBLANES, ch), 0)

    def body(g, carries):
        cf, cb = carries
        rf = pl.multiple_of(g * SUBLANES, SUBLANES)
        rb = pl.multiple_of((groups - 1 - g) * SUBLANES, SUBLANES)
        return (_scan_group(af_ref, uf_ref, rf, cf, row, False),
                _scan_group(ab_ref, ub_ref, rb, cb, row, True))

    zero = jnp.zeros((SUBLANES, ch), F32)
    lax.fori_loop(0, groups, body, (zero, zero), unroll=4)


def _lru_kernel(x_ref, gate_ref, cw_ref, cb_ref, wg_ref, bg_ref, lam_ref, y_ref,
                xpad_ref, af_ref, uf_ref, ab_ref, ub_ref, *, seq, ch):
    pad = SUBLANES
    zeros = jnp.zeros((pad, ch), F32)
    xpad_ref[pl.ds(0, pad), :] = zeros
    xpad_ref[pl.ds(seq + pad, pad), :] = zeros
    xpad_ref[pl.ds(pad, seq), :] = x_ref[...]

    cw = cw_ref[...]
    cb = cb_ref[...]
    bg = bg_ref[...]
    log_sig = _log_sigmoid(lam_ref[...])
    lpad = LRU_CONV_WIDTH // 2

    def gates(c, _):
        t0 = pl.multiple_of(c * LRU_ROWS, LRU_ROWS)
        win = xpad_ref[pl.ds(t0, LRU_ROWS + 2 * pad), :]
        xc = jnp.zeros((LRU_ROWS, ch), F32) + cb
        for k in range(LRU_CONV_WIDTH):
            off = pad - lpad + k
            xc = xc + cw[k:k + 1, :] * win[off:off + LRU_ROWS, :]
        pre = jnp.dot(xc.astype(BF16), wg_ref[...], preferred_element_type=F32) + bg
        for d, (a_ref, u_ref) in enumerate(((af_ref, uf_ref), (ab_ref, ub_ref))):
            r = _sigmoid(pre[:, (2 * d) * ch:(2 * d + 1) * ch])
            i = _sigmoid(pre[:, (2 * d + 1) * ch:(2 * d + 2) * ch])
            a = jnp.exp(LRU_C * r * log_sig[d:d + 1, :])
            a_ref[pl.ds(t0, LRU_ROWS), :] = a
            u_ref[pl.ds(t0, LRU_ROWS), :] = jnp.sqrt(1.0 - a * a) * (i * xc)
        return 0

    lax.fori_loop(0, seq // LRU_ROWS, gates, 0)

    _scans_in_place(af_ref, uf_ref, ab_ref, ub_ref, seq, ch)

    def finish(c, _):
        t0 = pl.multiple_of(c * LRU_ROWS, LRU_ROWS)
        h = uf_ref[pl.ds(t0, LRU_ROWS), :] + ub_ref[pl.ds(t0, LRU_ROWS), :]
        y_ref[pl.ds(t0, LRU_ROWS), :] = h * _gelu_tanh(gate_ref[pl.ds(t0, LRU_ROWS), :])
        return 0

    lax.fori_loop(0, seq // LRU_ROWS, finish, 0)


def _lru_branch(z3, conv_w, conv_b, wg, bg, lam, d_lru):
    b, s, _ = z3.shape
    ch = MIX_CH
    nc = d_lru // ch
    kern = functools.partial(_lru_kernel, seq=s, ch=ch)
    return pl.pallas_call(
        kern,
        grid=(b, nc),
        in_specs=[pl.BlockSpec((None, s, ch), lambda i, c: (i, 0, c)),
                  pl.BlockSpec((None, s, ch), lambda i, c: (i, 0, nc + c)),
                  pl.BlockSpec((LRU_CONV_WIDTH, ch), lambda i, c: (0, c)),
                  pl.BlockSpec((1, ch), lambda i, c: (0, c)),
                  pl.BlockSpec((None, ch, 4 * ch), lambda i, c: (c, 0, 0)),
                  pl.BlockSpec((None, 1, 4 * ch), lambda i, c: (c, 0, 0)),
                  pl.BlockSpec((2, ch), lambda i, c: (0, c))],
        out_specs=pl.BlockSpec((None, s, ch), lambda i, c: (i, 0, c)),
        out_shape=jax.ShapeDtypeStruct((b, s, d_lru), F32),
        scratch_shapes=[pltpu.VMEM((s + 2 * SUBLANES, ch), F32)]
        + [pltpu.VMEM((s, ch), F32)] * 4,
        compiler_params=_params(("parallel", "parallel"), 40),
        name="lru_branch",
    )(z3, z3, conv_w, conv_b, wg, bg, lam)


def _conf_kernel(a_ref, b_ref, cw_ref, cb_ref, ng_ref, nb_ref, avg_ref, y_ref,
                 gpad_ref, shift_ref, *, seq, ch):
    pad = 2 * SUBLANES
    half = CONF_KERNEL // 2
    zeros = jnp.zeros((pad, ch), F32)
    gpad_ref[pl.ds(0, pad), :] = zeros
    gpad_ref[pl.ds(seq + pad, pad), :] = zeros
    gpad_ref[pl.ds(pad, seq), :] = a_ref[...] * _sigmoid(b_ref[...])

    avg = avg_ref[...]
    span = CONV_ROWS + pad + SUBLANES

    def group_mean(v):
        hi = v.astype(BF16)
        r1 = v - hi.astype(F32)
        mid = r1.astype(BF16)
        lo = (r1 - mid.astype(F32)).astype(BF16)
        return (jnp.dot(hi, avg, preferred_element_type=F32)
                + jnp.dot(mid, avg, preferred_element_type=F32)
                + jnp.dot(lo, avg, preferred_element_type=F32))

    def chunk(c, _):
        t0 = pl.multiple_of(c * CONV_ROWS, CONV_ROWS)
        for lc in range(ch // LANES):
            cols = slice(lc * LANES, (lc + 1) * LANES)
            cw = cw_ref[:, cols]
            win = gpad_ref[pl.ds(t0, CONV_ROWS + 2 * pad), cols]
            acc = jnp.zeros((CONV_ROWS, LANES), F32) + cb_ref[:, cols]
            for b in range(SUBLANES):
                shift_ref[b] = win[b:b + span, :]
            for b in range(SUBLANES):
                for a in range(span // SUBLANES - CONV_ROWS // SUBLANES + 1):
                    k = SUBLANES * a + b - (pad - half)
                    if 0 <= k < CONF_KERNEL:
                        rows = slice(SUBLANES * a, SUBLANES * a + CONV_ROWS)
                        acc = acc + cw[k:k + 1, :] * shift_ref[b, rows, :]
            dev = acc - group_mean(acc)
            var = group_mean(dev * dev)
            y = dev * lax.rsqrt(var + EPS) * ng_ref[:, cols] + nb_ref[:, cols]
            y_ref[pl.ds(t0, CONV_ROWS), cols] = y * _sigmoid(y)
        return 0

    lax.fori_loop(0, seq // CONV_ROWS, chunk, 0)


def _conf_branch(z3, conv_w, conv_b, norm_g, norm_b, avg, d_lru, d_conv):
    b, s, _ = z3.shape
    ch = MIX_CH
    nc = d_conv // ch
    base = 2 * d_lru // ch
    kern = functools.partial(_conf_kernel, seq=s, ch=ch)
    return pl.pallas_call(
        kern,
        grid=(b, nc),
        in_specs=[pl.BlockSpec((None, s, ch), lambda i, c: (i, 0, base + c)),
                  pl.BlockSpec((None, s, ch), lambda i, c: (i, 0, base + nc + c)),
                  pl.BlockSpec((CONF_KERNEL, ch), lambda i, c: (0, c)),
                  pl.BlockSpec((1, ch), lambda i, c: (0, c)),
                  pl.BlockSpec((1, ch), lambda i, c: (0, c)),
                  pl.BlockSpec((1, ch), lambda i, c: (0, c)),
                  pl.BlockSpec((LANES, LANES), lambda i, c: (0, 0))],
        out_specs=pl.BlockSpec((None, s, ch), lambda i, c: (i, 0, c)),
        out_shape=jax.ShapeDtypeStruct((b, s, d_conv), F32),
        scratch_shapes=[pltpu.VMEM((s + 4 * SUBLANES, ch), F32),
                        pltpu.VMEM((SUBLANES, CONV_ROWS + 3 * SUBLANES, LANES), F32)],
        compiler_params=_params(("parallel", "parallel"), 40),
        name="conf_branch",
    )(z3, z3, conv_w, conv_b, norm_g, norm_b, avg)


def _outproj_kernel(yl_ref, yc_ref, x_ref, bl_ref, bc_ref, wl_ref, wc_ref, fg_ref,
                    h_ref, ntb_ref):
    yl = _rms(yl_ref[...], bl_ref[...]).astype(BF16)
    yc = _rms(yc_ref[...], bc_ref[...]).astype(BF16)
    h = (x_ref[...]
         + jnp.dot(yl, wl_ref[...], preferred_element_type=F32)
         + jnp.dot(yc, wc_ref[...], preferred_element_type=F32))
    h_ref[...] = h
    ntb_ref[...] = _rms(h, fg_ref[...]).T.astype(BF16)


def _outproj(yl, yc, x2, beta_l, beta_c, wl_bf, wc_bf, ffn_g):
    t, d = x2.shape
    dl = yl.shape[1]
    dc = yc.shape[1]
    row = lambda i: (i, 0)
    fixed = lambda i: (0, 0)
    return pl.pallas_call(
        _outproj_kernel,
        grid=(t // ROW_TILE,),
        in_specs=[pl.BlockSpec((ROW_TILE, dl), row),
                  pl.BlockSpec((ROW_TILE, dc), row),
                  pl.BlockSpec((ROW_TILE, d), row),
                  pl.BlockSpec((1, dl), fixed),
                  pl.BlockSpec((1, dc), fixed),
                  pl.BlockSpec((dl, d), fixed),
                  pl.BlockSpec((dc, d), fixed),
                  pl.BlockSpec((1, d), fixed)],
        out_specs=[pl.BlockSpec((ROW_TILE, d), row),
                   pl.BlockSpec((d, ROW_TILE), lambda i: (0, i))],
        out_shape=[jax.ShapeDtypeStruct((t, d), F32),
                   jax.ShapeDtypeStruct((d, t), BF16)],
        compiler_params=_params(("parallel",), 40),
        name="outproj",
    )(yl, yc, x2, beta_l, beta_c, wl_bf, wc_bf, ffn_g)


def _sort_network(n):
    pairs = []
    p = 1
    while p < n:
        k = p
        while k >= 1:
            for j in range(k % p, n - k, 2 * k):
                for i in range(min(k, n - j - k)):
                    if (i + j) // (2 * p) == (i + j + k) // (2 * p):
                        pairs.append((i + j, i + j + k))
            k //= 2
        p *= 2
    return pairs


def _pruned_network(n_pow2, n_live, n_out):
    pairs = [(i, j) for i, j in _sort_network(n_pow2) if j < n_live]
    needed = set(range(n_out))
    kept = []
    for i, j in reversed(pairs):
        if i in needed or j in needed:
            kept.append((i, j))
            needed.update((i, j))
    return kept[::-1]


def _apply_network(vals, pairs):
    vals = list(vals)
    for i, j in pairs:
        hi = jnp.maximum(vals[i], vals[j])
        lo = jnp.minimum(vals[i], vals[j])
        vals[i], vals[j] = hi, lo
    return vals


def _top16_over_keys(s):
    k = PEER_TOPK
    blocks = [s[SUBLANES * v:SUBLANES * (v + 1), :] for v in range(PEER_N_KEYS // SUBLANES)]
    top = _apply_network(blocks, _sort_network(len(blocks)))
    for d in (1, 2, 4):
        top = [jnp.maximum(top[i], pltpu.roll(top[k - 1 - i], d, 0)) for i in range(k)]
        stride = k // 2
        while stride >= 1:
            pairs = [(i, i + stride) for i in range(k) if not i & stride]
            top = _apply_network(top, pairs)
            stride //= 2
    return top


def _staircase(k):
    return [(a, b) for a in range(k) for b in range(k) if (a + 1) * (b + 1) <= k]


def _k_largest(cands, k):
    n_pow2 = 1
    while n_pow2 < len(cands):
        n_pow2 *= 2
    return _apply_network(cands, _pruned_network(n_pow2, len(cands), k))[:k]


def _score_kernel(nt_ref, wq_ref, keys_ref, g1_ref, e2_ref, thr_ref,
                  q_ref, s_ref, top_ref, z_ref, *, tok):
    k = PEER_TOPK
    chunks = [(c, slice(c * LANES, (c + 1) * LANES)) for c in range(tok // LANES)]

    q_ref[...] = jnp.dot(wq_ref[...], nt_ref[...], preferred_element_type=F32)

    def per_half(hp, _):
        h = hp // 2
        p = hp % 2
        r0 = pl.multiple_of(hp * PEER_N_KEYS, PEER_N_KEYS)
        s = jnp.dot(keys_ref[hp], q_ref[pl.ds(r0, PEER_N_KEYS), :], precision=HIGHEST,
                    preferred_element_type=F32)
        s_ref[p, h] = s
        for c, cols in chunks:
            top = _top16_over_keys(s[:, cols])
            for i in range(k):
                top_ref[p, i, c, pl.ds(h, 1), :] = top[i][0:1, :]
        return 0

    lax.fori_loop(0, 2 * PEER_HEADS, per_half, 0)

    for c, cols in chunks:
        first = [top_ref[0, i, c] for i in range(k)]
        second = [top_ref[1, i, c] for i in range(k)]
        best = _k_largest([first[a] + second[b] for a, b in _staircase(k)], k)
        z = jnp.zeros_like(best[0])
        for v in best:
            z = z + jnp.exp(v - best[0])
        z_ref[c] = z
        inv_z = 1.0 / z
        g1_top = [jnp.exp(first[a] - first[0]) * inv_z for a in range(k)]
        e2_top = [jnp.exp(second[b] - second[0]) for b in range(k)]
        thr_ref[:, cols] = _k_largest([g1_top[a] * e2_top[b] for a, b in _staircase(k)], k)[k - 1]

    def per_head(h, _):
        for c, cols in chunks:
            inv_z = 1.0 / z_ref[c, pl.ds(h, 1), :]
            m1 = top_ref[0, 0, c, pl.ds(h, 1), :]
            m2 = top_ref[1, 0, c, pl.ds(h, 1), :]
            g1_ref[h, :, cols] = jnp.exp(s_ref[0, h, :, cols] - m1) * inv_z
            e2_ref[h, :, cols] = jnp.exp(s_ref[1, h, :, cols] - m2)
        return 0

    lax.fori_loop(0, PEER_HEADS, per_head, 0)


def _peer_scores(nt_bf, wq_t_bf, keys):
    d, t = nt_bf.shape
    tok = SCORE_TOK
    kern = functools.partial(_score_kernel, tok=tok)
    shape = (PEER_HEADS, PEER_N_KEYS, t)
    big_spec = pl.BlockSpec((PEER_HEADS, PEER_N_KEYS, tok), lambda i: (0, 0, i))
    return pl.pallas_call(
        kern,
        grid=(t // tok,),
        in_specs=[pl.BlockSpec((d, tok), lambda i: (0, i)),
                  pl.BlockSpec(wq_t_bf.shape, lambda i: (0, 0)),
                  pl.BlockSpec(keys.shape, lambda i: (0, 0, 0))],
        out_specs=[big_spec, big_spec,
                   pl.BlockSpec((PEER_HEADS, tok), lambda i: (0, i))],
        out_shape=[jax.ShapeDtypeStruct(shape, F32), jax.ShapeDtypeStruct(shape, F32),
                   jax.ShapeDtypeStruct((PEER_HEADS, t), F32)],
        scratch_shapes=[pltpu.VMEM((wq_t_bf.shape[0], tok), F32),
                        pltpu.VMEM((2, PEER_HEADS, PEER_N_KEYS, tok), F32),
                        pltpu.VMEM((2, PEER_TOPK, tok // LANES, PEER_HEADS, LANES), F32),
                        pltpu.VMEM((tok // LANES, PEER_HEADS, LANES), F32)],
        compiler_params=_params(("parallel",), 48),
        name="peer_scores",
    )(nt_bf, wq_t_bf, keys)


def _gelu_times(x, w):
    k0 = -2.0 * 0.7978845608028654 * 1.4426950408889634
    k1 = k0 * 0.044715
    e = jnp.exp2(x * (k0 + k1 * (x * x)))
    return (x * w) / (1.0 + e)


def _peer_kernel(nt_ref, u_ref, vt_ref, g1_ref, e2_ref, thr_ref, out_ref,
                 act0_ref, act1_ref, a0_ref, a1_ref, *, tok, n_exp, k_steps):
    g = pl.program_id(0)
    rows_per_step = n_exp // PEER_N_KEYS

    @pl.when(g == 0)
    def _():
        act1_ref[...] = jnp.zeros_like(act1_ref)
        a0_ref[...] = jnp.zeros_like(a0_ref)

    @pl.when(jnp.logical_or(g < 2, lax.rem(jnp.maximum(g - 2, 0), k_steps) == 0))
    def _():
        out_ref[...] = jnp.zeros_like(out_ref)

    kb = lax.rem(jnp.maximum(g - 1, 0), k_steps)
    i0 = pl.multiple_of(kb * rows_per_step, rows_per_step)

    def body(act_w, act_r, a_w, a_r):
        d = out_ref.shape[0]
        assert tok == MXU_COUNT * MXU_TILE and d % MXU_TILE == 0 and n_exp % MXU_TILE == 0
        k_tiles = {0: d // MXU_TILE, 1: n_exp // MXU_TILE}
        m_slices = {0: n_exp // MXU_TILE, 1: d // MXU_TILE}
        chunks = MXU_TILE // MM_ROWS
        acc_entries = MXU_TILE // 4

        def rhs_tile(mm, k, q):
            src = nt_ref if mm == 0 else a_r
            return src[k * MXU_TILE:(k + 1) * MXU_TILE, q * MXU_TILE:(q + 1) * MXU_TILE]

        def unit(mm, s, k, c):
            lhs_ref = u_ref if mm == 0 else vt_ref
            base = (2 * mm + s % 2) * acc_entries
            r0 = s * MXU_TILE + c * MM_ROWS
            lhs = lhs_ref[r0:r0 + MM_ROWS, k * MXU_TILE:(k + 1) * MXU_TILE]
            reg = k % 2
            for q in range(MXU_COUNT):
                if c == 0 and k == 0:
                    pltpu.matmul_push_rhs(rhs_tile(mm, 0, q), staging_register=reg, mxu_index=q)
                pltpu.matmul_acc_lhs(base + c * (MM_ROWS // 4), lhs, mxu_index=q,
                                     load_staged_rhs=reg if c == 0 else None)
                if c == 0 and k + 1 < k_tiles[mm]:
                    pltpu.matmul_push_rhs(rhs_tile(mm, k + 1, q), staging_register=1 - reg,
                                          mxu_index=q)

        def drain(mm, s):
            base = (2 * mm + s % 2) * acc_entries
            rows = slice(s * MXU_TILE, (s + 1) * MXU_TILE)
            for q in range(MXU_COUNT):
                cols = slice(q * MXU_TILE, (q + 1) * MXU_TILE)
                res = pltpu.matmul_pop(base, (MXU_TILE, MXU_TILE), F32, q)
                if mm == 0:
                    act_w[rows, cols] = res
                else:
                    out_ref[rows, cols] += res

        units = [(mm, s, k, c) for s in range(max(m_slices.values())) for mm in (0, 1)
                 if s < m_slices[mm] for k in range(k_tiles[mm]) for c in range(chunks)]
        pending = []

        def issue(idx):
            if idx < len(units):
                mm, s, k, c = units[idx]
                unit(mm, s, k, c)
                if k == k_tiles[mm] - 1 and c == chunks - 1:
                    pending.append((idx + POP_LAG_UNITS, mm, s))
            while pending and pending[0][0] <= idx:
                _, mm, s = pending.pop(0)
                drain(mm, s)

        slot = 0
        for ii in range(rows_per_step):
            for lc in range(tok // LANES):
                cols = slice(lc * LANES, (lc + 1) * LANES)
                g1b = [jnp.broadcast_to(
                    g1_ref[h, pl.ds(i0, rows_per_step), cols][ii:ii + 1, :], (W_ROWS, LANES))
                    for h in range(PEER_HEADS)]
                thrb = [jnp.broadcast_to(thr_ref[h:h + 1, cols], (W_ROWS, LANES))
                        for h in range(PEER_HEADS)]
                for jt in range(PEER_N_KEYS // W_ROWS):
                    keys = slice(jt * W_ROWS, (jt + 1) * W_ROWS)
                    rows = slice(ii * PEER_N_KEYS + jt * W_ROWS,
                                 ii * PEER_N_KEYS + (jt + 1) * W_ROWS)
                    w = jnp.zeros((W_ROWS, LANES), F32)
                    for h in range(PEER_HEADS):
                        p = e2_ref[h, keys, cols] * g1b[h]
                        w = w + jnp.where(p >= thrb[h], p, 0.0)
                    a_w[rows, cols] = _gelu_times(act_r[rows, cols], w).astype(BF16)
                    issue(slot)
                    slot += 1
        while slot < len(units) or pending:
            issue(slot)
            slot += 1

    @pl.when(lax.rem(g, 2) == 0)
    def _():
        body(act0_ref, act1_ref, a1_ref, a0_ref)

    @pl.when(lax.rem(g, 2) == 1)
    def _():
        body(act1_ref, act0_ref, a0_ref, a1_ref)


def _peer_dense(nt_bf, u_bf, vt_bf, g1, e2, thr):
    d, t = nt_bf.shape
    n_experts = u_bf.shape[0]
    tok, n_exp = PEER_TOK, PEER_EXP
    n_tok = t // tok
    k_steps = n_experts // n_exp
    kern = functools.partial(_peer_kernel, tok=tok, n_exp=n_exp, k_steps=k_steps)

    def tok_tile(lag):
        return lambda g: jnp.clip((g - lag) // k_steps, 0, n_tok - 1)

    def exp_tile(lag):
        return lambda g: jnp.maximum(g - lag, 0) % k_steps

    big_spec = pl.BlockSpec((PEER_HEADS, PEER_N_KEYS, tok), lambda g: (0, 0, tok_tile(1)(g)))
    return pl.pallas_call(
        kern,
        grid=(n_tok * k_steps + 2,),
        in_specs=[pl.BlockSpec((d, tok), lambda g: (0, tok_tile(0)(g))),
                  pl.BlockSpec((n_exp, d), lambda g: (exp_tile(0)(g), 0)),
                  pl.BlockSpec((d, n_exp), lambda g: (0, exp_tile(2)(g))),
                  big_spec, big_spec,
                  pl.BlockSpec((PEER_HEADS, tok), lambda g: (0, tok_tile(1)(g)))],
        out_specs=pl.BlockSpec((d, tok), lambda g: (0, tok_tile(2)(g))),
        out_shape=jax.ShapeDtypeStruct((d, t), F32),
        scratch_shapes=[pltpu.VMEM((n_exp, tok), F32), pltpu.VMEM((n_exp, tok), F32),
                        pltpu.VMEM((n_exp, tok), BF16), pltpu.VMEM((n_exp, tok), BF16)],
        compiler_params=_params(("arbitrary",), 48),
        name="peer_dense",
    )(nt_bf, u_bf, vt_bf, g1, e2, thr)


def _final_kernel(h_ref, pt_ref, g_ref, o_ref):
    o_ref[...] = _rms(h_ref[...] + pt_ref[...].T, g_ref[...])


def _final(h, peer_t, g):
    t, d = h.shape
    return pl.pallas_call(
        _final_kernel,
        grid=(t // ROW_TILE,),
        in_specs=[pl.BlockSpec((ROW_TILE, d), lambda i: (i, 0)),
                  pl.BlockSpec((d, ROW_TILE), lambda i: (0, i)),
                  pl.BlockSpec((1, d), lambda i: (0, 0))],
        out_specs=pl.BlockSpec((ROW_TILE, d), lambda i: (i, 0)),
        out_shape=jax.ShapeDtypeStruct((t, d), F32),
        compiler_params=_params(("parallel",), 24),
        name="final_norm",
    )(h, peer_t, g)


def _block_diag_chunks(w, ch):
    heads, hd, _ = w.shape
    per = ch // hd
    w4 = w.reshape(heads // per, per, hd, hd)
    eye = jnp.eye(per, dtype=w.dtype)
    return jnp.einsum("chij,hg->chigj", w4, eye).reshape(heads // per, ch, ch)


def _layer(h2, batch, seq, mix_norm_g, w_in, lru_conv_w, lru_conv_b, lru_w_rg, lru_b_rg,
           lru_w_ig, lru_b_ig, lru_lambda, conf_conv_w, conf_conv_b, conf_norm_g,
           conf_norm_b, beta_lru, beta_conv, w_out, ffn_norm_g, peer_w_q, peer_sub_keys,
           peer_u, peer_v):
    t, d = h2.shape
    d_lru = lru_conv_w.shape[1]
    d_conv = conf_conv_w.shape[1]
    row = lambda v: v.reshape(1, -1)

    z = _inproj(h2, row(mix_norm_g), w_in.astype(BF16))
    z3 = z.reshape(batch, seq, z.shape[1])

    ch = MIX_CH
    nc = d_lru // ch
    wg = jnp.concatenate([_block_diag_chunks(lru_w_rg[0], ch), _block_diag_chunks(lru_w_ig[0], ch),
                          _block_diag_chunks(lru_w_rg[1], ch), _block_diag_chunks(lru_w_ig[1], ch)],
                         axis=-1)
    bg = jnp.concatenate([lru_b_rg[0].reshape(nc, 1, ch), lru_b_ig[0].reshape(nc, 1, ch),
                          lru_b_rg[1].reshape(nc, 1, ch), lru_b_ig[1].reshape(nc, 1, ch)],
                         axis=-1)
    y_lru = _lru_branch(z3, lru_conv_w, row(lru_conv_b), wg.astype(BF16), bg, lru_lambda, d_lru)

    gdim = d_conv // CONV_GROUPS
    grp = jnp.arange(LANES) // gdim
    avg = ((grp[:, None] == grp[None, :]).astype(F32) / gdim).astype(BF16)
    y_conv = _conf_branch(z3, conf_conv_w, row(conf_conv_b), row(conf_norm_g),
                          row(conf_norm_b), avg, d_lru, d_conv)

    w_out_bf = w_out.astype(BF16)
    h2, nt_bf = _outproj(y_lru.reshape(t, d_lru), y_conv.reshape(t, d_conv), h2,
                             row(beta_lru), row(beta_conv), w_out_bf[:d_lru], w_out_bf[d_lru:],
                             row(ffn_norm_g))

    keys = peer_sub_keys.reshape(2 * PEER_HEADS, PEER_N_KEYS, -1)
    g1, e2, thr = _peer_scores(nt_bf, peer_w_q.T.astype(BF16), keys)
    peer_t = _peer_dense(nt_bf, _cast(peer_u, BF16), _transpose_cast(peer_v, BF16),
                         g1, e2, thr)
    return h2, peer_t


def kernel(x, mix_norm_g, w_in, lru_conv_w, lru_conv_b, lru_w_rg, lru_b_rg, lru_w_ig, lru_b_ig, lru_lambda, conf_conv_w, conf_conv_b, conf_norm_g, conf_norm_b, beta_lru, beta_conv, w_out, ffn_norm_g, peer_w_q, peer_sub_keys, peer_u, peer_v, final_norm_g):
    batch, seq, d = x.shape
    depth = w_in.shape[0]
    h2 = x.reshape(batch * seq, d)
    peer_t = None
    for l in range(depth):
        if peer_t is not None:
            h2 = h2 + peer_t.T
        h2, peer_t = _layer(
            h2, batch, seq, mix_norm_g[l], w_in[l], lru_conv_w[l], lru_conv_b[l], lru_w_rg[l],
            lru_b_rg[l], lru_w_ig[l], lru_b_ig[l], lru_lambda[l], conf_conv_w[l], conf_conv_b[l],
            conf_norm_g[l], conf_norm_b[l], beta_lru[l], beta_conv[l], w_out[l], ffn_norm_g[l],
            peer_w_q[l], peer_sub_keys[l], peer_u[l], peer_v[l])
    out = _final(h2, peer_t, final_norm_g.reshape(1, -1))
    return out.reshape(batch, seq, d)
```

```python
import functools

import jax
import jax.numpy as jnp
from jax import lax
from jax.experimental import pallas as pl
from jax.experimental.pallas import tpu as pltpu

F32 = jnp.float32
BF16 = jnp.bfloat16
HIGHEST = lax.Precision.HIGHEST

SUBLANES = 8
LANES = 128
MIB = 1024 * 1024

EPS = 1e-6
LRU_C = 8.0
LRU_HEADS = 8
LRU_CONV_WIDTH = 4
CONV_GROUPS = 8
CONF_KERNEL = 31
PEER_HEADS = 8
PEER_N_KEYS = 128
PEER_TOPK = 16

ROW_TILE = 512
MIX_CH = 256
LRU_ROWS = 128
CONV_ROWS = 128
SCORE_TOK = 512
PEER_TOK = 512
PEER_EXP = 1024
W_ROWS = 8
MXU_COUNT = 2
MXU_TILE = 256
MM_ROWS = 16
POP_LAG_UNITS = 28


def _params(semantics, vmem_mib):
    return pltpu.CompilerParams(dimension_semantics=semantics,
                                vmem_limit_bytes=vmem_mib * MIB)


def _rms(x, g):
    return x * lax.rsqrt(jnp.mean(x * x, axis=-1, keepdims=True) + EPS) * g


def _gelu_tanh(x):
    c = 0.7978845608028654
    return x * (0.5 * (1.0 + jnp.tanh(c * (x + 0.044715 * (x * x * x)))))


def _sigmoid(x):
    return 1.0 / (1.0 + jnp.exp(-x))


def _log_sigmoid(x):
    return -(jnp.maximum(-x, 0.0) + jnp.log(1.0 + jnp.exp(-jnp.abs(x))))


def _transpose_cast_kernel(x_ref, o_ref):
    o_ref[...] = x_ref[...].T.astype(o_ref.dtype)


def _transpose_cast(x, dtype, tile=512):
    r, c = x.shape
    return pl.pallas_call(
        _transpose_cast_kernel,
        grid=(r // tile,),
        in_specs=[pl.BlockSpec((tile, c), lambda i: (i, 0))],
        out_specs=pl.BlockSpec((c, tile), lambda i: (0, i)),
        out_shape=jax.ShapeDtypeStruct((c, r), dtype),
        compiler_params=_params(("parallel",), 24),
        name="transpose_cast",
    )(x)


def _cast_kernel(x_ref, o_ref):
    o_ref[...] = x_ref[...].astype(o_ref.dtype)


def _cast(x, dtype, tile=512):
    r, c = x.shape
    return pl.pallas_call(
        _cast_kernel,
        grid=(r // tile,),
        in_specs=[pl.BlockSpec((tile, c), lambda i: (i, 0))],
        out_specs=pl.BlockSpec((tile, c), lambda i: (i, 0)),
        out_shape=jax.ShapeDtypeStruct((r, c), dtype),
        compiler_params=_params(("parallel",), 24),
        name="cast",
    )(x)


def _inproj_kernel(x_ref, g_ref, w_ref, z_ref):
    n = _rms(x_ref[...], g_ref[...])
    z_ref[...] = jnp.dot(n.astype(BF16), w_ref[...], preferred_element_type=F32)


def _inproj(x2, g, w_bf):
    t, d = x2.shape
    e = w_bf.shape[1]
    return pl.pallas_call(
        _inproj_kernel,
        grid=(t // ROW_TILE,),
        in_specs=[pl.BlockSpec((ROW_TILE, d), lambda i: (i, 0)),
                  pl.BlockSpec((1, d), lambda i: (0, 0)),
                  pl.BlockSpec((d, e), lambda i: (0, 0))],
        out_specs=pl.BlockSpec((ROW_TILE, e), lambda i: (i, 0)),
        out_shape=jax.ShapeDtypeStruct((t, e), F32),
        compiler_params=_params(("parallel",), 40),
        name="inproj",
    )(x2, g, w_bf)


def _scan_group(a_ref, u_ref, r0, carry, row, reverse):
    a = a_ref[pl.ds(r0, SUBLANES), :]
    u = u_ref[pl.ds(r0, SUBLANES), :]
    for d in (1, 2, 4):
        if reverse:
            shift, keep = SUBLANES - d, row < SUBLANES - d
        else:
            shift, keep = d, row >= d
        a_nb = pltpu.roll(a, shift, 0)
        u_nb = pltpu.roll(u, shift, 0)
        u = u + a * jnp.where(keep, u_nb, 0.0)
        a = a * jnp.where(keep, a_nb, 1.0)
    h = u + a * carry
    u_ref[pl.ds(r0, SUBLANES), :] = h
    edge = h[0:1] if reverse else h[SUBLANES - 1:SUBLANES]
    return jnp.broadcast_to(edge, h.shape)


def _scans_in_place(af_ref, uf_ref, ab_ref, ub_ref, seq, ch):
    groups = seq // SUBLANES
    row = lax.broadcasted_iota(jnp.int32, (SUBLANES, ch), 0)

    def body(g, carries):
        cf, cb = carries
        rf = pl.multiple_of(g * SUBLANES, SUBLANES)
        rb = pl.multiple_of((groups - 1 - g) * SUBLANES, SUBLANES)
        return (_scan_group(af_ref, uf_ref, rf, cf, row, False),
                _scan_group(ab_ref, ub_ref, rb, cb, row, True))

    zero = jnp.zeros((SUBLANES, ch), F32)
    lax.fori_loop(0, groups, body, (zero, zero), unroll=4)


def _lru_kernel(x_ref, gate_ref, cw_ref, cb_ref, wg_ref, bg_ref, lam_ref, y_ref,
                xpad_ref, af_ref, uf_ref, ab_ref, ub_ref, *, seq, ch):
    pad = SUBLANES
    zeros = jnp.zeros((pad, ch), F32)
    xpad_ref[pl.ds(0, pad), :] = zeros
    xpad_ref[pl.ds(seq + pad, pad), :] = zeros
    xpad_ref[pl.ds(pad, seq), :] = x_ref[...]

    cw = cw_ref[...]
    cb = cb_ref[...]
    bg = bg_ref[...]
    log_sig = _log_sigmoid(lam_ref[...])
    lpad = LRU_CONV_WIDTH // 2

    def gates(c, _):
        t0 = pl.multiple_of(c * LRU_ROWS, LRU_ROWS)
        win = xpad_ref[pl.ds(t0, LRU_ROWS + 2 * pad), :]
        xc = jnp.zeros((LRU_ROWS, ch), F32) + cb
        for k in range(LRU_CONV_WIDTH):
            off = pad - lpad + k
            xc = xc + cw[k:k + 1, :] * win[off:off + LRU_ROWS, :]
        pre = jnp.dot(xc.astype(BF16), wg_ref[...], preferred_element_type=F32) + bg
        for d, (a_ref, u_ref) in enumerate(((af_ref, uf_ref), (ab_ref, ub_ref))):
            r = _sigmoid(pre[:, (2 * d) * ch:(2 * d + 1) * ch])
            i = _sigmoid(pre[:, (2 * d + 1) * ch:(2 * d + 2) * ch])
            a = jnp.exp(LRU_C * r * log_sig[d:d + 1, :])
            a_ref[pl.ds(t0, LRU_ROWS), :] = a
            u_ref[pl.ds(t0, LRU_ROWS), :] = jnp.sqrt(1.0 - a * a) * (i * xc)
        return 0

    lax.fori_loop(0, seq // LRU_ROWS, gates, 0)

    _scans_in_place(af_ref, uf_ref, ab_ref, ub_ref, seq, ch)

    def finish(c, _):
        t0 = pl.multiple_of(c * LRU_ROWS, LRU_ROWS)
        h = uf_ref[pl.ds(t0, LRU_ROWS), :] + ub_ref[pl.ds(t0, LRU_ROWS), :]
        y_ref[pl.ds(t0, LRU_ROWS), :] = h * _gelu_tanh(gate_ref[pl.ds(t0, LRU_ROWS), :])
        return 0

    lax.fori_loop(0, seq // LRU_ROWS, finish, 0)


def _lru_branch(z3, conv_w, conv_b, wg, bg, lam, d_lru):
    b, s, _ = z3.shape
    ch = MIX_CH
    nc = d_lru // ch
    kern = functools.partial(_lru_kernel, seq=s, ch=ch)
    return pl.pallas_call(
        kern,
        grid=(b, nc),
        in_specs=[pl.BlockSpec((None, s, ch), lambda i, c: (i, 0, c)),
                  pl.BlockSpec((None, s, ch), lambda i, c: (i, 0, nc + c)),
                  pl.BlockSpec((LRU_CONV_WIDTH, ch), lambda i, c: (0, c)),
                  pl.BlockSpec((1, ch), lambda i, c: (0, c)),
                  pl.BlockSpec((None, ch, 4 * ch), lambda i, c: (c, 0, 0)),
                  pl.BlockSpec((None, 1, 4 * ch), lambda i, c: (c, 0, 0)),
                  pl.BlockSpec((2, ch), lambda i, c: (0, c))],
        out_specs=pl.BlockSpec((None, s, ch), lambda i, c: (i, 0, c)),
        out_shape=jax.ShapeDtypeStruct((b, s, d_lru), F32),
        scratch_shapes=[pltpu.VMEM((s + 2 * SUBLANES, ch), F32)]
        + [pltpu.VMEM((s, ch), F32)] * 4,
        compiler_params=_params(("parallel", "parallel"), 40),
        name="lru_branch",
    )(z3, z3, conv_w, conv_b, wg, bg, lam)


def _conf_kernel(a_ref, b_ref, cw_ref, cb_ref, ng_ref, nb_ref, avg_ref, y_ref,
                 gpad_ref, shift_ref, *, seq, ch):
    pad = 2 * SUBLANES
    half = CONF_KERNEL // 2
    zeros = jnp.zeros((pad, ch), F32)
    gpad_ref[pl.ds(0, pad), :] = zeros
    gpad_ref[pl.ds(seq + pad, pad), :] = zeros
    gpad_ref[pl.ds(pad, seq), :] = a_ref[...] * _sigmoid(b_ref[...])

    avg = avg_ref[...]
    span = CONV_ROWS + pad + SUBLANES

    def group_mean(v):
        hi = v.astype(BF16)
        r1 = v - hi.astype(F32)
        mid = r1.astype(BF16)
        lo = (r1 - mid.astype(F32)).astype(BF16)
        return (jnp.dot(hi, avg, preferred_element_type=F32)
                + jnp.dot(mid, avg, preferred_element_type=F32)
                + jnp.dot(lo, avg, preferred_element_type=F32))

    def chunk(c, _):
        t0 = pl.multiple_of(c * CONV_ROWS, CONV_ROWS)
        for lc in range(ch // LANES):
            cols = slice(lc * LANES, (lc + 1) * LANES)
            cw = cw_ref[:, cols]
            win = gpad_ref[pl.ds(t0, CONV_ROWS + 2 * pad), cols]
            acc = jnp.zeros((CONV_ROWS, LANES), F32) + cb_ref[:, cols]
            for b in range(SUBLANES):
                shift_ref[b] = win[b:b + span, :]
            for b in range(SUBLANES):
                for a in range(span // SUBLANES - CONV_ROWS // SUBLANES + 1):
                    k = SUBLANES * a + b - (pad - half)
                    if 0 <= k < CONF_KERNEL:
                        rows = slice(SUBLANES * a, SUBLANES * a + CONV_ROWS)
                        acc = acc + cw[k:k + 1, :] * shift_ref[b, rows, :]
            dev = acc - group_mean(acc)
            var = group_mean(dev * dev)
            y = dev * lax.rsqrt(var + EPS) * ng_ref[:, cols] + nb_ref[:, cols]
            y_ref[pl.ds(t0, CONV_ROWS), cols] = y * _sigmoid(y)
        return 0

    lax.fori_loop(0, seq // CONV_ROWS, chunk, 0)


def _conf_branch(z3, conv_w, conv_b, norm_g, norm_b, avg, d_lru, d_conv):
    b, s, _ = z3.shape
    ch = MIX_CH
    nc = d_conv // ch
    base = 2 * d_lru // ch
    kern = functools.partial(_conf_kernel, seq=s, ch=ch)
    return pl.pallas_call(
        kern,
        grid=(b, nc),
        in_specs=[pl.BlockSpec((None, s, ch), lambda i, c: (i, 0, base + c)),
                  pl.BlockSpec((None, s, ch), lambda i, c: (i, 0, base + nc + c)),
                  pl.BlockSpec((CONF_KERNEL, ch), lambda i, c: (0, c)),
                  pl.BlockSpec((1, ch), lambda i, c: (0, c)),
                  pl.BlockSpec((1, ch), lambda i, c: (0, c)),
                  pl.BlockSpec((1, ch), lambda i, c: (0, c)),
                  pl.BlockSpec((LANES, LANES), lambda i, c: (0, 0))],
        out_specs=pl.BlockSpec((None, s, ch), lambda i, c: (i, 0, c)),
        out_shape=jax.ShapeDtypeStruct((b, s, d_conv), F32),
        scratch_shapes=[pltpu.VMEM((s + 4 * SUBLANES, ch), F32),
                        pltpu.VMEM((SUBLANES, CONV_ROWS + 3 * SUBLANES, LANES), F32)],
        compiler_params=_params(("parallel", "parallel"), 40),
        name="conf_branch",
    )(z3, z3, conv_w, conv_b, norm_g, norm_b, avg)


def _outproj_kernel(yl_ref, yc_ref, x_ref, bl_ref, bc_ref, wl_ref, wc_ref, fg_ref,
                    h_ref, ntb_ref):
    yl = _rms(yl_ref[...], bl_ref[...]).astype(BF16)
    yc = _rms(yc_ref[...], bc_ref[...]).astype(BF16)
    h = (x_ref[...]
         + jnp.dot(yl, wl_ref[...], preferred_element_type=F32)
         + jnp.dot(yc, wc_ref[...], preferred_element_type=F32))
    h_ref[...] = h
    ntb_ref[...] = _rms(h, fg_ref[...]).T.astype(BF16)


def _outproj(yl, yc, x2, beta_l, beta_c, wl_bf, wc_bf, ffn_g):
    t, d = x2.shape
    dl = yl.shape[1]
    dc = yc.shape[1]
    row = lambda i: (i, 0)
    fixed = lambda i: (0, 0)
    return pl.pallas_call(
        _outproj_kernel,
        grid=(t // ROW_TILE,),
        in_specs=[pl.BlockSpec((ROW_TILE, dl), row),
                  pl.BlockSpec((ROW_TILE, dc), row),
                  pl.BlockSpec((ROW_TILE, d), row),
                  pl.BlockSpec((1, dl), fixed),
                  pl.BlockSpec((1, dc), fixed),
                  pl.BlockSpec((dl, d), fixed),
                  pl.BlockSpec((dc, d), fixed),
                  pl.BlockSpec((1, d), fixed)],
        out_specs=[pl.BlockSpec((ROW_TILE, d), row),
                   pl.BlockSpec((d, ROW_TILE), lambda i: (0, i))],
        out_shape=[jax.ShapeDtypeStruct((t, d), F32),
                   jax.ShapeDtypeStruct((d, t), BF16)],
        compiler_params=_params(("parallel",), 40),
        name="outproj",
    )(yl, yc, x2, beta_l, beta_c, wl_bf, wc_bf, ffn_g)


def _sort_network(n):
    pairs = []
    p = 1
    while p < n:
        k = p
        while k >= 1:
            for j in range(k % p, n - k, 2 * k):
                for i in range(min(k, n - j - k)):
                    if (i + j) // (2 * p) == (i + j + k) // (2 * p):
                        pairs.append((i + j, i + j + k))
            k //= 2
        p *= 2
    return pairs


def _pruned_network(n_pow2, n_live, n_out):
    pairs = [(i, j) for i, j in _sort_network(n_pow2) if j < n_live]
    needed = set(range(n_out))
    kept = []
    for i, j in reversed(pairs):
        if i in needed or j in needed:
            kept.append((i, j))
            needed.update((i, j))
    return kept[::-1]


def _apply_network(vals, pairs):
    vals = list(vals)
    for i, j in pairs:
        hi = jnp.maximum(vals[i], vals[j])
        lo = jnp.minimum(vals[i], vals[j])
        vals[i], vals[j] = hi, lo
    return vals


def _top16_over_keys(s):
    k = PEER_TOPK
    blocks = [s[SUBLANES * v:SUBLANES * (v + 1), :] for v in range(PEER_N_KEYS // SUBLANES)]
    top = _apply_network(blocks, _sort_network(len(blocks)))
    for d in (1, 2, 4):
        top = [jnp.maximum(top[i], pltpu.roll(top[k - 1 - i], d, 0)) for i in range(k)]
        stride = k // 2
        while stride >= 1:
            pairs = [(i, i + stride) for i in range(k) if not i & stride]
            top = _apply_network(top, pairs)
            stride //= 2
    return top


def _staircase(k):
    return [(a, b) for a in range(k) for b in range(k) if (a + 1) * (b + 1) <= k]


def _k_largest(cands, k):
    n_pow2 = 1
    while n_pow2 < len(cands):
        n_pow2 *= 2
    return _apply_network(cands, _pruned_network(n_pow2, len(cands), k))[:k]


def _score_kernel(nt_ref, wq_ref, keys_ref, g1_ref, e2_ref, thr_ref,
                  q_ref, s_ref, top_ref, z_ref, *, tok):
    k = PEER_TOPK
    chunks = [(c, slice(c * LANES, (c + 1) * LANES)) for c in range(tok // LANES)]

    q_ref[...] = jnp.dot(wq_ref[...], nt_ref[...], preferred_element_type=F32)

    def per_half(hp, _):
        h = hp // 2
        p = hp % 2
        r0 = pl.multiple_of(hp * PEER_N_KEYS, PEER_N_KEYS)
        s = jnp.dot(keys_ref[hp], q_ref[pl.ds(r0, PEER_N_KEYS), :], precision=HIGHEST,
                    preferred_element_type=F32)
        s_ref[p, h] = s
        for c, cols in chunks:
            top = _top16_over_keys(s[:, cols])
            for i in range(k):
                top_ref[p, i, c, pl.ds(h, 1), :] = top[i][0:1, :]
        return 0

    lax.fori_loop(0, 2 * PEER_HEADS, per_half, 0, unroll=2)

    for c, cols in chunks:
        first = [top_ref[0, i, c] for i in range(k)]
        second = [top_ref[1, i, c] for i in range(k)]
        best = _k_largest([first[a] + second[b] for a, b in _staircase(k)], k)
        z = jnp.zeros_like(best[0])
        for v in best:
            z = z + jnp.exp(v - best[0])
        z_ref[c] = z
        inv_z = 1.0 / z
        g1_top = [jnp.exp(first[a] - first[0]) * inv_z for a in range(k)]
        e2_top = [jnp.exp(second[b] - second[0]) for b in range(k)]
        thr_ref[:, cols] = _k_largest([g1_top[a] * e2_top[b] for a, b in _staircase(k)], k)[k - 1]

    def per_head(h, _):
        for c, cols in chunks:
            inv_z = 1.0 / z_ref[c, pl.ds(h, 1), :]
            m1 = top_ref[0, 0, c, pl.ds(h, 1), :]
            m2 = top_ref[1, 0, c, pl.ds(h, 1), :]
            g1_ref[h, :, cols] = jnp.exp(s_ref[0, h, :, cols] - m1) * inv_z
            e2_ref[h, :, cols] = jnp.exp(s_ref[1, h, :, cols] - m2)
        return 0

    lax.fori_loop(0, PEER_HEADS, per_head, 0)


def _peer_scores(nt_bf, wq_t_bf, keys):
    d, t = nt_bf.shape
    tok = SCORE_TOK
    kern = functools.partial(_score_kernel, tok=tok)
    shape = (PEER_HEADS, PEER_N_KEYS, t)
    big_spec = pl.BlockSpec((PEER_HEADS, PEER_N_KEYS, tok), lambda i: (0, 0, i))
    return pl.pallas_call(
        kern,
        grid=(t // tok,),
        in_specs=[pl.BlockSpec((d, tok), lambda i: (0, i)),
                  pl.BlockSpec(wq_t_bf.shape, lambda i: (0, 0)),
                  pl.BlockSpec(keys.shape, lambda i: (0, 0, 0))],
        out_specs=[big_spec, big_spec,
                   pl.BlockSpec((PEER_HEADS, tok), lambda i: (0, i))],
        out_shape=[jax.ShapeDtypeStruct(shape, F32), jax.ShapeDtypeStruct(shape, F32),
                   jax.ShapeDtypeStruct((PEER_HEADS, t), F32)],
        scratch_shapes=[pltpu.VMEM((wq_t_bf.shape[0], tok), F32),
                        pltpu.VMEM((2, PEER_HEADS, PEER_N_KEYS, tok), F32),
                        pltpu.VMEM((2, PEER_TOPK, tok // LANES, PEER_HEADS, LANES), F32),
                        pltpu.VMEM((tok // LANES, PEER_HEADS, LANES), F32)],
        compiler_params=_params(("parallel",), 48),
        name="peer_scores",
    )(nt_bf, wq_t_bf, keys)


def _gelu_times(x, w):
    k0 = -2.0 * 0.7978845608028654 * 1.4426950408889634
    k1 = k0 * 0.044715
    e = jnp.exp2(x * (k0 + k1 * (x * x)))
    return (x * w) / (1.0 + e)


def _peer_kernel(nt_ref, u_ref, vt_ref, g1_ref, e2_ref, thr_ref, out_ref,
                 act0_ref, act1_ref, a0_ref, a1_ref, *, tok, n_exp, k_steps):
    g = pl.program_id(0)
    rows_per_step = n_exp // PEER_N_KEYS

    @pl.when(g == 0)
    def _():
        act1_ref[...] = jnp.zeros_like(act1_ref)
        a0_ref[...] = jnp.zeros_like(a0_ref)

    @pl.when(jnp.logical_or(g < 2, lax.rem(jnp.maximum(g - 2, 0), k_steps) == 0))
    def _():
        out_ref[...] = jnp.zeros_like(out_ref)

    kb = lax.rem(jnp.maximum(g - 1, 0), k_steps)
    i0 = pl.multiple_of(kb * rows_per_step, rows_per_step)

    def body(act_w, act_r, a_w, a_r):
        d = out_ref.shape[0]
        assert tok == MXU_COUNT * MXU_TILE and d % MXU_TILE == 0 and n_exp % MXU_TILE == 0
        k_tiles = {0: d // MXU_TILE, 1: n_exp // MXU_TILE}
        m_slices = {0: n_exp // MXU_TILE, 1: d // MXU_TILE}
        chunks = MXU_TILE // MM_ROWS
        acc_entries = MXU_TILE // 4

        def rhs_tile(mm, k, q):
            src = nt_ref if mm == 0 else a_r
            return src[k * MXU_TILE:(k + 1) * MXU_TILE, q * MXU_TILE:(q + 1) * MXU_TILE]

        def unit(mm, s, k, c):
            lhs_ref = u_ref if mm == 0 else vt_ref
            base = (2 * mm + s % 2) * acc_entries
            r0 = s * MXU_TILE + c * MM_ROWS
            lhs = lhs_ref[r0:r0 + MM_ROWS, k * MXU_TILE:(k + 1) * MXU_TILE]
            reg = k % 2
            for q in range(MXU_COUNT):
                if c == 0 and k == 0:
                    pltpu.matmul_push_rhs(rhs_tile(mm, 0, q), staging_register=reg, mxu_index=q)
                pltpu.matmul_acc_lhs(base + c * (MM_ROWS // 4), lhs, mxu_index=q,
                                     load_staged_rhs=reg if c == 0 else None)
                if c == 0 and k + 1 < k_tiles[mm]:
                    pltpu.matmul_push_rhs(rhs_tile(mm, k + 1, q), staging_register=1 - reg,
                                          mxu_index=q)

        def drain(mm, s):
            base = (2 * mm + s % 2) * acc_entries
            rows = slice(s * MXU_TILE, (s + 1) * MXU_TILE)
            for q in range(MXU_COUNT):
                cols = slice(q * MXU_TILE, (q + 1) * MXU_TILE)
                res = pltpu.matmul_pop(base, (MXU_TILE, MXU_TILE), F32, q)
                if mm == 0:
                    act_w[rows, cols] = res
                else:
                    out_ref[rows, cols] += res

        units = [(mm, s, k, c) for s in range(max(m_slices.values())) for mm in (0, 1)
                 if s < m_slices[mm] for k in range(k_tiles[mm]) for c in range(chunks)]
        pending = []

        def issue(idx):
            if idx < len(units):
                mm, s, k, c = units[idx]
                unit(mm, s, k, c)
                if k == k_tiles[mm] - 1 and c == chunks - 1:
                    pending.append((idx + POP_LAG_UNITS, mm, s))
            while pending and pending[0][0] <= idx:
                _, mm, s = pending.pop(0)
                drain(mm, s)

        slot = 0
        for ii in range(rows_per_step):
            for lc in range(tok // LANES):
                cols = slice(lc * LANES, (lc + 1) * LANES)
                g1b = [jnp.broadcast_to(
                    g1_ref[h, pl.ds(i0, rows_per_step), cols][ii:ii + 1, :], (W_ROWS, LANES))
                    for h in range(PEER_HEADS)]
                thrb = [jnp.broadcast_to(thr_ref[h:h + 1, cols], (W_ROWS, LANES))
                        for h in range(PEER_HEADS)]
                for jt in range(PEER_N_KEYS // W_ROWS):
                    keys = slice(jt * W_ROWS, (jt + 1) * W_ROWS)
                    rows = slice(ii * PEER_N_KEYS + jt * W_ROWS,
                                 ii * PEER_N_KEYS + (jt + 1) * W_ROWS)
                    parts = []
                    for h in range(PEER_HEADS):
                        p = e2_ref[h, keys, cols] * g1b[h]
                        parts.append(jnp.where(p >= thrb[h], p, 0.0))
                    while len(parts) > 1:
                        parts = [parts[i] + parts[i + 1] for i in range(0, len(parts), 2)]
                    a_w[rows, cols] = _gelu_times(act_r[rows, cols], parts[0]).astype(BF16)
                    issue(slot)
                    slot += 1
        while slot < len(units) or pending:
            issue(slot)
            slot += 1

    @pl.when(lax.rem(g, 2) == 0)
    def _():
        body(act0_ref, act1_ref, a1_ref, a0_ref)

    @pl.when(lax.rem(g, 2) == 1)
    def _():
        body(act1_ref, act0_ref, a0_ref, a1_ref)


def _peer_dense(nt_bf, u_bf, vt_bf, g1, e2, thr):
    d, t = nt_bf.shape
    n_experts = u_bf.shape[0]
    tok, n_exp = PEER_TOK, PEER_EXP
    n_tok = t // tok
    k_steps = n_experts // n_exp
    kern = functools.partial(_peer_kernel, tok=tok, n_exp=n_exp, k_steps=k_steps)

    def tok_tile(lag):
        return lambda g: jnp.clip((g - lag) // k_steps, 0, n_tok - 1)

    def exp_tile(lag):
        return lambda g: jnp.maximum(g - lag, 0) % k_steps

    big_spec = pl.BlockSpec((PEER_HEADS, PEER_N_KEYS, tok), lambda g: (0, 0, tok_tile(1)(g)))
    return pl.pallas_call(
        kern,
        grid=(n_tok * k_steps + 2,),
        in_specs=[pl.BlockSpec((d, tok), lambda g: (0, tok_tile(0)(g))),
                  pl.BlockSpec((n_exp, d), lambda g: (exp_tile(0)(g), 0)),
                  pl.BlockSpec((d, n_exp), lambda g: (0, exp_tile(2)(g))),
                  big_spec, big_spec,
                  pl.BlockSpec((PEER_HEADS, tok), lambda g: (0, tok_tile(1)(g)))],
        out_specs=pl.BlockSpec((d, tok), lambda g: (0, tok_tile(2)(g))),
        out_shape=jax.ShapeDtypeStruct((d, t), F32),
        scratch_shapes=[pltpu.VMEM((n_exp, tok), F32), pltpu.VMEM((n_exp, tok), F32),
                        pltpu.VMEM((n_exp, tok), BF16), pltpu.VMEM((n_exp, tok), BF16)],
        compiler_params=_params(("arbitrary",), 48),
        name="peer_dense",
    )(nt_bf, u_bf, vt_bf, g1, e2, thr)


def _final_kernel(h_ref, pt_ref, g_ref, o_ref):
    o_ref[...] = _rms(h_ref[...] + pt_ref[...].T, g_ref[...])


def _final(h, peer_t, g):
    t, d = h.shape
    return pl.pallas_call(
        _final_kernel,
        grid=(t // ROW_TILE,),
        in_specs=[pl.BlockSpec((ROW_TILE, d), lambda i: (i, 0)),
                  pl.BlockSpec((d, ROW_TILE), lambda i: (0, i)),
                  pl.BlockSpec((1, d), lambda i: (0, 0))],
        out_specs=pl.BlockSpec((ROW_TILE, d), lambda i: (i, 0)),
        out_shape=jax.ShapeDtypeStruct((t, d), F32),
        compiler_params=_params(("parallel",), 24),
        name="final_norm",
    )(h, peer_t, g)


def _block_diag_chunks(w, ch):
    heads, hd, _ = w.shape
    per = ch // hd
    w4 = w.reshape(heads // per, per, hd, hd)
    eye = jnp.eye(per, dtype=w.dtype)
    return jnp.einsum("chij,hg->chigj", w4, eye).reshape(heads // per, ch, ch)


def _layer(h2, batch, seq, mix_norm_g, w_in, lru_conv_w, lru_conv_b, lru_w_rg, lru_b_rg,
           lru_w_ig, lru_b_ig, lru_lambda, conf_conv_w, conf_conv_b, conf_norm_g,
           conf_norm_b, beta_lru, beta_conv, w_out, ffn_norm_g, peer_w_q, peer_sub_keys,
           peer_u, peer_v):
    t, d = h2.shape
    d_lru = lru_conv_w.shape[1]
    d_conv = conf_conv_w.shape[1]
    row = lambda v: v.reshape(1, -1)

    z = _inproj(h2, row(mix_norm_g), w_in.astype(BF16))
    z3 = z.reshape(batch, seq, z.shape[1])

    ch = MIX_CH
    nc = d_lru // ch
    wg = jnp.concatenate([_block_diag_chunks(lru_w_rg[0], ch), _block_diag_chunks(lru_w_ig[0], ch),
                          _block_diag_chunks(lru_w_rg[1], ch), _block_diag_chunks(lru_w_ig[1], ch)],
                         axis=-1)
    bg = jnp.concatenate([lru_b_rg[0].reshape(nc, 1, ch), lru_b_ig[0].reshape(nc, 1, ch),
                          lru_b_rg[1].reshape(nc, 1, ch), lru_b_ig[1].reshape(nc, 1, ch)],
                         axis=-1)
    y_lru = _lru_branch(z3, lru_conv_w, row(lru_conv_b), wg.astype(BF16), bg, lru_lambda, d_lru)

    gdim = d_conv // CONV_GROUPS
    grp = jnp.arange(LANES) // gdim
    avg = ((grp[:, None] == grp[None, :]).astype(F32) / gdim).astype(BF16)
    y_conv = _conf_branch(z3, conf_conv_w, row(conf_conv_b), row(conf_norm_g),
                          row(conf_norm_b), avg, d_lru, d_conv)

    w_out_bf = w_out.astype(BF16)
    h2, nt_bf = _outproj(y_lru.reshape(t, d_lru), y_conv.reshape(t, d_conv), h2,
                             row(beta_lru), row(beta_conv), w_out_bf[:d_lru], w_out_bf[d_lru:],
                             row(ffn_norm_g))

    keys = peer_sub_keys.reshape(2 * PEER_HEADS, PEER_N_KEYS, -1)
    g1, e2, thr = _peer_scores(nt_bf, peer_w_q.T.astype(BF16), keys)
    peer_t = _peer_dense(nt_bf, _cast(peer_u, BF16), _transpose_cast(peer_v, BF16),
                         g1, e2, thr)
    return h2, peer_t


def kernel(x, mix_norm_g, w_in, lru_conv_w, lru_conv_b, lru_w_rg, lru_b_rg, lru_w_ig, lru_b_ig, lru_lambda, conf_conv_w, conf_conv_b, conf_norm_g, conf_norm_b, beta_lru, beta_conv, w_out, ffn_norm_g, peer_w_q, peer_sub_keys, peer_u, peer_v, final_norm_g):
    batch, seq, d = x.shape
    depth = w_in.shape[0]
    h2 = x.reshape(batch * seq, d)
    peer_t = None
    for l in range(depth):
        if peer_t is not None:
            h2 = h2 + peer_t.T
        h2, peer_t = _layer(
            h2, batch, seq, mix_norm_g[l], w_in[l], lru_conv_w[l], lru_conv_b[l], lru_w_rg[l],
            lru_b_rg[l], lru_w_ig[l], lru_b_ig[l], lru_lambda[l], conf_conv_w[l], conf_conv_b[l],
            conf_norm_g[l], conf_norm_b[l], beta_lru[l], beta_conv[l], w_out[l], ffn_norm_g[l],
            peer_w_q[l], peer_sub_keys[l], peer_u[l], peer_v[l])
    out = _final(h2, peer_t, final_norm_g.reshape(1, -1))
    return out.reshape(batch, seq, d)
```

```python
import functools

import jax
import jax.numpy as jnp
from jax import lax
from jax.experimental import pallas as pl
from jax.experimental.pallas import tpu as pltpu

F32 = jnp.float32
BF16 = jnp.bfloat16
HIGHEST = lax.Precision.HIGHEST

SUBLANES = 8
LANES = 128
MIB = 1024 * 1024

EPS = 1e-6
LRU_C = 8.0
LRU_HEADS = 8
LRU_CONV_WIDTH = 4
CONV_GROUPS = 8
CONF_KERNEL = 31
PEER_HEADS = 8
PEER_N_KEYS = 128
PEER_TOPK = 16

ROW_TILE = 512
MIX_CH = 256
LRU_ROWS = 128
CONV_ROWS = 128
NORM_ROWS = 128
NORM_BLOCKS = 4
SCORE_TOK = 512
PEER_TOK = 512
PEER_EXP = 1024
W_ROWS = 8
MXU_COUNT = 2
MXU_TILE = 256
MM_ROWS = 16
POP_LAG_UNITS = 28


def _params(semantics, vmem_mib):
    return pltpu.CompilerParams(dimension_semantics=semantics,
                                vmem_limit_bytes=vmem_mib * MIB)


def _rms(x, g):
    return x * lax.rsqrt(jnp.mean(x * x, axis=-1, keepdims=True) + EPS) * g


def _gelu_tanh(x):
    c = 0.7978845608028654
    return x * (0.5 * (1.0 + jnp.tanh(c * (x + 0.044715 * (x * x * x)))))


def _sigmoid(x):
    return 1.0 / (1.0 + jnp.exp(-x))


def _log_sigmoid(x):
    return -(jnp.maximum(-x, 0.0) + jnp.log(1.0 + jnp.exp(-jnp.abs(x))))


def _transpose_cast_kernel(x_ref, o_ref):
    o_ref[...] = x_ref[...].T.astype(o_ref.dtype)


def _transpose_cast(x, dtype, tile=512):
    r, c = x.shape
    return pl.pallas_call(
        _transpose_cast_kernel,
        grid=(r // tile,),
        in_specs=[pl.BlockSpec((tile, c), lambda i: (i, 0))],
        out_specs=pl.BlockSpec((c, tile), lambda i: (0, i)),
        out_shape=jax.ShapeDtypeStruct((c, r), dtype),
        compiler_params=_params(("parallel",), 24),
        name="transpose_cast",
    )(x)


def _cast_kernel(x_ref, o_ref):
    o_ref[...] = x_ref[...].astype(o_ref.dtype)


def _cast(x, dtype, tile=512):
    r, c = x.shape
    return pl.pallas_call(
        _cast_kernel,
        grid=(r // tile,),
        in_specs=[pl.BlockSpec((tile, c), lambda i: (i, 0))],
        out_specs=pl.BlockSpec((tile, c), lambda i: (i, 0)),
        out_shape=jax.ShapeDtypeStruct((r, c), dtype),
        compiler_params=_params(("parallel",), 24),
        name="cast",
    )(x)


def _inproj_kernel(x_ref, g_ref, w_ref, z_ref):
    n = _rms(x_ref[...], g_ref[...])
    z_ref[...] = jnp.dot(n.astype(BF16), w_ref[...], preferred_element_type=F32)


def _inproj(x2, g, w_bf):
    t, d = x2.shape
    e = w_bf.shape[1]
    return pl.pallas_call(
        _inproj_kernel,
        grid=(t // ROW_TILE,),
        in_specs=[pl.BlockSpec((ROW_TILE, d), lambda i: (i, 0)),
                  pl.BlockSpec((1, d), lambda i: (0, 0)),
                  pl.BlockSpec((d, e), lambda i: (0, 0))],
        out_specs=pl.BlockSpec((ROW_TILE, e), lambda i: (i, 0)),
        out_shape=jax.ShapeDtypeStruct((t, e), F32),
        compiler_params=_params(("parallel",), 40),
        name="inproj",
    )(x2, g, w_bf)


def _scan_group(a_ref, u_ref, r0, carry, row, reverse):
    a = a_ref[pl.ds(r0, SUBLANES), :]
    u = u_ref[pl.ds(r0, SUBLANES), :]
    for d in (1, 2, 4):
        if reverse:
            shift, keep = SUBLANES - d, row < SUBLANES - d
        else:
            shift, keep = d, row >= d
        a_nb = pltpu.roll(a, shift, 0)
        u_nb = pltpu.roll(u, shift, 0)
        u = u + a * jnp.where(keep, u_nb, 0.0)
        a = a * jnp.where(keep, a_nb, 1.0)
    h = u + a * carry
    u_ref[pl.ds(r0, SUBLANES), :] = h
    edge = h[0:1] if reverse else h[SUBLANES - 1:SUBLANES]
    return jnp.broadcast_to(edge, h.shape)


def _scans_in_place(af_ref, uf_ref, ab_ref, ub_ref, seq, ch):
    groups = seq // SUBLANES
    row = lax.broadcasted_iota(jnp.int32, (SUBLANES, ch), 0)

    def body(g, carries):
        cf, cb = carries
        rf = pl.multiple_of(g * SUBLANES, SUBLANES)
        rb = pl.multiple_of((groups - 1 - g) * SUBLANES, SUBLANES)
        return (_scan_group(af_ref, uf_ref, rf, cf, row, False),
                _scan_group(ab_ref, ub_ref, rb, cb, row, True))

    zero = jnp.zeros((SUBLANES, ch), F32)
    lax.fori_loop(0, groups, body, (zero, zero), unroll=4)


def _lru_kernel(x_ref, gate_ref, cw_ref, cb_ref, wg_ref, bg_ref, lam_ref, y_ref,
                xpad_ref, af_ref, uf_ref, ab_ref, ub_ref, pre_ref, *, seq, ch):
    pad = SUBLANES
    zeros = jnp.zeros((pad, ch), F32)
    xpad_ref[pl.ds(0, pad), :] = zeros
    xpad_ref[pl.ds(seq + pad, pad), :] = zeros
    xpad_ref[pl.ds(pad, seq), :] = x_ref[...]

    cw = cw_ref[...]
    cb = cb_ref[...]
    bg = bg_ref[...]
    log_sig = _log_sigmoid(lam_ref[...])
    lpad = LRU_CONV_WIDTH // 2

    def conv(c, _):
        t0 = pl.multiple_of(c * LRU_ROWS, LRU_ROWS)
        win = xpad_ref[pl.ds(t0, LRU_ROWS + 2 * pad), :]
        xc = jnp.zeros((LRU_ROWS, ch), F32) + cb
        for k in range(LRU_CONV_WIDTH):
            off = pad - lpad + k
            xc = xc + cw[k:k + 1, :] * win[off:off + LRU_ROWS, :]
        af_ref[pl.ds(t0, LRU_ROWS), :] = xc
        return 0

    lax.fori_loop(0, seq // LRU_ROWS, conv, 0)

    pre_ref[...] = jnp.dot(af_ref[...].astype(BF16), wg_ref[...], preferred_element_type=F32)

    def gates(c, _):
        t0 = pl.multiple_of(c * LRU_ROWS, LRU_ROWS)
        xc = af_ref[pl.ds(t0, LRU_ROWS), :]
        pre = pre_ref[pl.ds(t0, LRU_ROWS), :] + bg
        for d, (a_ref, u_ref) in enumerate(((af_ref, uf_ref), (ab_ref, ub_ref))):
            r = _sigmoid(pre[:, (2 * d) * ch:(2 * d + 1) * ch])
            i = _sigmoid(pre[:, (2 * d + 1) * ch:(2 * d + 2) * ch])
            a = jnp.exp(LRU_C * r * log_sig[d:d + 1, :])
            a_ref[pl.ds(t0, LRU_ROWS), :] = a
            u_ref[pl.ds(t0, LRU_ROWS), :] = jnp.sqrt(1.0 - a * a) * (i * xc)
        return 0

    lax.fori_loop(0, seq // LRU_ROWS, gates, 0)

    _scans_in_place(af_ref, uf_ref, ab_ref, ub_ref, seq, ch)

    def finish(c, _):
        t0 = pl.multiple_of(c * LRU_ROWS, LRU_ROWS)
        h = uf_ref[pl.ds(t0, LRU_ROWS), :] + ub_ref[pl.ds(t0, LRU_ROWS), :]
        y_ref[pl.ds(t0, LRU_ROWS), :] = h * _gelu_tanh(gate_ref[pl.ds(t0, LRU_ROWS), :])
        return 0

    lax.fori_loop(0, seq // LRU_ROWS, finish, 0)


def _lru_branch(z3, conv_w, conv_b, wg, bg, lam, d_lru):
    b, s, _ = z3.shape
    ch = MIX_CH
    nc = d_lru // ch
    kern = functools.partial(_lru_kernel, seq=s, ch=ch)
    return pl.pallas_call(
        kern,
        grid=(b, nc),
        in_specs=[pl.BlockSpec((None, s, ch), lambda i, c: (i, 0, c)),
                  pl.BlockSpec((None, s, ch), lambda i, c: (i, 0, nc + c)),
                  pl.BlockSpec((LRU_CONV_WIDTH, ch), lambda i, c: (0, c)),
                  pl.BlockSpec((1, ch), lambda i, c: (0, c)),
                  pl.BlockSpec((None, ch, 4 * ch), lambda i, c: (c, 0, 0)),
                  pl.BlockSpec((None, 1, 4 * ch), lambda i, c: (c, 0, 0)),
                  pl.BlockSpec((2, ch), lambda i, c: (0, c))],
        out_specs=pl.BlockSpec((None, s, ch), lambda i, c: (i, 0, c)),
        out_shape=jax.ShapeDtypeStruct((b, s, d_lru), F32),
        scratch_shapes=[pltpu.VMEM((s + 2 * SUBLANES, ch), F32)]
        + [pltpu.VMEM((s, ch), F32)] * 4 + [pltpu.VMEM((s, 4 * ch), F32)],
        compiler_params=_params(("parallel", "parallel"), 48),
        name="lru_branch",
    )(z3, z3, conv_w, conv_b, wg, bg, lam)


def _conf_kernel(a_ref, b_ref, cw_ref, cb_ref, ng_ref, nb_ref, avg_ref, y_ref,
                 gpad_ref, shift_ref, *, seq, ch):
    pad = 2 * SUBLANES
    half = CONF_KERNEL // 2
    zeros = jnp.zeros((pad, ch), F32)
    gpad_ref[pl.ds(0, pad), :] = zeros
    gpad_ref[pl.ds(seq + pad, pad), :] = zeros
    gpad_ref[pl.ds(pad, seq), :] = a_ref[...] * _sigmoid(b_ref[...])

    avg = avg_ref[...]
    span = CONV_ROWS + pad + SUBLANES

    def group_mean(v):
        hi = v.astype(BF16)
        r1 = v - hi.astype(F32)
        mid = r1.astype(BF16)
        lo = (r1 - mid.astype(F32)).astype(BF16)
        return (jnp.dot(hi, avg, preferred_element_type=F32)
                + jnp.dot(mid, avg, preferred_element_type=F32)
                + jnp.dot(lo, avg, preferred_element_type=F32))

    def conv_chunk(c, _):
        t0 = pl.multiple_of(c * CONV_ROWS, CONV_ROWS)
        for lc in range(ch // LANES):
            cols = slice(lc * LANES, (lc + 1) * LANES)
            cw = cw_ref[:, cols]
            win = gpad_ref[pl.ds(t0, CONV_ROWS + 2 * pad), cols]
            acc = jnp.zeros((CONV_ROWS, LANES), F32) + cb_ref[:, cols]
            for b in range(SUBLANES):
                shift_ref[lc, b] = win[b:b + span, :]
            for b in range(SUBLANES):
                for a in range(span // SUBLANES - CONV_ROWS // SUBLANES + 1):
                    k = SUBLANES * a + b - (pad - half)
                    if 0 <= k < CONF_KERNEL:
                        rows = slice(SUBLANES * a, SUBLANES * a + CONV_ROWS)
                        acc = acc + cw[k:k + 1, :] * shift_ref[lc, b, rows, :]
            y_ref[pl.ds(t0, CONV_ROWS), cols] = acc
        return 0

    lax.fori_loop(0, seq // CONV_ROWS, conv_chunk, 0)

    def norm_chunk(c, _):
        blocks = []
        for r in range(NORM_BLOCKS):
            t0 = pl.multiple_of((c * NORM_BLOCKS + r) * NORM_ROWS, NORM_ROWS)
            for lc in range(ch // LANES):
                blocks.append((pl.ds(t0, NORM_ROWS), slice(lc * LANES, (lc + 1) * LANES)))
        xs = [y_ref[rows, cols] for rows, cols in blocks]
        means = [group_mean(x) for x in xs]
        devs = [x - m for x, m in zip(xs, means)]
        variances = [group_mean(dev * dev) for dev in devs]
        for (rows, cols), dev, var in zip(blocks, devs, variances):
            y = dev * lax.rsqrt(var + EPS) * ng_ref[:, cols] + nb_ref[:, cols]
            y_ref[rows, cols] = y * _sigmoid(y)
        return 0

    lax.fori_loop(0, seq // (NORM_ROWS * NORM_BLOCKS), norm_chunk, 0)


def _conf_branch(z3, conv_w, conv_b, norm_g, norm_b, avg, d_lru, d_conv):
    b, s, _ = z3.shape
    ch = MIX_CH
    nc = d_conv // ch
    base = 2 * d_lru // ch
    kern = functools.partial(_conf_kernel, seq=s, ch=ch)
    return pl.pallas_call(
        kern,
        grid=(b, nc),
        in_specs=[pl.BlockSpec((None, s, ch), lambda i, c: (i, 0, base + c)),
                  pl.BlockSpec((None, s, ch), lambda i, c: (i, 0, base + nc + c)),
                  pl.BlockSpec((CONF_KERNEL, ch), lambda i, c: (0, c)),
                  pl.BlockSpec((1, ch), lambda i, c: (0, c)),
                  pl.BlockSpec((1, ch), lambda i, c: (0, c)),
                  pl.BlockSpec((1, ch), lambda i, c: (0, c)),
                  pl.BlockSpec((LANES, LANES), lambda i, c: (0, 0))],
        out_specs=pl.BlockSpec((None, s, ch), lambda i, c: (i, 0, c)),
        out_shape=jax.ShapeDtypeStruct((b, s, d_conv), F32),
        scratch_shapes=[pltpu.VMEM((s + 4 * SUBLANES, ch), F32),
                        pltpu.VMEM((ch // LANES, SUBLANES, CONV_ROWS + 3 * SUBLANES, LANES), F32)],
        compiler_params=_params(("parallel", "parallel"), 40),
        name="conf_branch",
    )(z3, z3, conv_w, conv_b, norm_g, norm_b, avg)


def _outproj_kernel(yl_ref, yc_ref, x_ref, bl_ref, bc_ref, wl_ref, wc_ref, fg_ref,
                    h_ref, ntb_ref):
    yl = _rms(yl_ref[...], bl_ref[...]).astype(BF16)
    yc = _rms(yc_ref[...], bc_ref[...]).astype(BF16)
    h = (x_ref[...]
         + jnp.dot(yl, wl_ref[...], preferred_element_type=F32)
         + jnp.dot(yc, wc_ref[...], preferred_element_type=F32))
    h_ref[...] = h
    ntb_ref[...] = _rms(h, fg_ref[...]).T.astype(BF16)


def _outproj(yl, yc, x2, beta_l, beta_c, wl_bf, wc_bf, ffn_g):
    t, d = x2.shape
    dl = yl.shape[1]
    dc = yc.shape[1]
    row = lambda i: (i, 0)
    fixed = lambda i: (0, 0)
    return pl.pallas_call(
        _outproj_kernel,
        grid=(t // ROW_TILE,),
        in_specs=[pl.BlockSpec((ROW_TILE, dl), row),
                  pl.BlockSpec((ROW_TILE, dc), row),
                  pl.BlockSpec((ROW_TILE, d), row),
                  pl.BlockSpec((1, dl), fixed),
                  pl.BlockSpec((1, dc), fixed),
                  pl.BlockSpec((dl, d), fixed),
                  pl.BlockSpec((dc, d), fixed),
                  pl.BlockSpec((1, d), fixed)],
        out_specs=[pl.BlockSpec((ROW_TILE, d), row),
                   pl.BlockSpec((d, ROW_TILE), lambda i: (0, i))],
        out_shape=[jax.ShapeDtypeStruct((t, d), F32),
                   jax.ShapeDtypeStruct((d, t), BF16)],
        compiler_params=_params(("parallel",), 40),
        name="outproj",
    )(yl, yc, x2, beta_l, beta_c, wl_bf, wc_bf, ffn_g)


def _sort_network(n):
    pairs = []
    p = 1
    while p < n:
        k = p
        while k >= 1:
            for j in range(k % p, n - k, 2 * k):
                for i in range(min(k, n - j - k)):
                    if (i + j) // (2 * p) == (i + j + k) // (2 * p):
                        pairs.append((i + j, i + j + k))
            k //= 2
        p *= 2
    return pairs


def _pruned_network(n_pow2, n_live, n_out):
    pairs = [(i, j) for i, j in _sort_network(n_pow2) if j < n_live]
    needed = set(range(n_out))
    kept = []
    for i, j in reversed(pairs):
        if i in needed or j in needed:
            kept.append((i, j))
            needed.update((i, j))
    return kept[::-1]


def _apply_network(vals, pairs):
    vals = list(vals)
    for i, j in pairs:
        hi = jnp.maximum(vals[i], vals[j])
        lo = jnp.minimum(vals[i], vals[j])
        vals[i], vals[j] = hi, lo
    return vals


def _top16_over_keys(s):
    k = PEER_TOPK
    blocks = [s[SUBLANES * v:SUBLANES * (v + 1), :] for v in range(PEER_N_KEYS // SUBLANES)]
    top = _apply_network(blocks, _sort_network(len(blocks)))
    for d in (1, 2, 4):
        top = [jnp.maximum(top[i], pltpu.roll(top[k - 1 - i], d, 0)) for i in range(k)]
        stride = k // 2
        while stride >= 1:
            pairs = [(i, i + stride) for i in range(k) if not i & stride]
            top = _apply_network(top, pairs)
            stride //= 2
    return top


def _staircase(k):
    return [(a, b) for a in range(k) for b in range(k) if (a + 1) * (b + 1) <= k]


def _k_largest(cands, k):
    n_pow2 = 1
    while n_pow2 < len(cands):
        n_pow2 *= 2
    return _apply_network(cands, _pruned_network(n_pow2, len(cands), k))[:k]


def _score_kernel(nt_ref, wq_ref, keys_ref, g1_ref, e2_ref, thr_ref,
                  q_ref, s_ref, top_ref, z_ref, *, tok):
    k = PEER_TOPK
    chunks = [(c, slice(c * LANES, (c + 1) * LANES)) for c in range(tok // LANES)]

    q_ref[...] = jnp.dot(wq_ref[...], nt_ref[...], preferred_element_type=F32)

    def per_half(hp, _):
        h = hp // 2
        p = hp % 2
        r0 = pl.multiple_of(hp * PEER_N_KEYS, PEER_N_KEYS)
        s = jnp.dot(keys_ref[hp], q_ref[pl.ds(r0, PEER_N_KEYS), :], precision=HIGHEST,
                    preferred_element_type=F32)
        s_ref[p, h] = s
        for c, cols in chunks:
            top = _top16_over_keys(s[:, cols])
            for i in range(k):
                top_ref[p, i, c, pl.ds(h, 1), :] = top[i][0:1, :]
        return 0

    lax.fori_loop(0, 2 * PEER_HEADS, per_half, 0, unroll=2)

    for c, cols in chunks:
        first = [top_ref[0, i, c] for i in range(k)]
        second = [top_ref[1, i, c] for i in range(k)]
        best = _k_largest([first[a] + second[b] for a, b in _staircase(k)], k)
        z = jnp.zeros_like(best[0])
        for v in best:
            z = z + jnp.exp(v - best[0])
        z_ref[c] = z
        inv_z = 1.0 / z
        g1_top = [jnp.exp(first[a] - first[0]) * inv_z for a in range(k)]
        e2_top = [jnp.exp(second[b] - second[0]) for b in range(k)]
        thr_ref[:, cols] = _k_largest([g1_top[a] * e2_top[b] for a, b in _staircase(k)], k)[k - 1]

    def per_head(h, _):
        for c, cols in chunks:
            inv_z = 1.0 / z_ref[c, pl.ds(h, 1), :]
            m1 = top_ref[0, 0, c, pl.ds(h, 1), :]
            m2 = top_ref[1, 0, c, pl.ds(h, 1), :]
            g1_ref[h, :, cols] = jnp.exp(s_ref[0, h, :, cols] - m1) * inv_z
            e2_ref[h, :, cols] = jnp.exp(s_ref[1, h, :, cols] - m2)
        return 0

    lax.fori_loop(0, PEER_HEADS, per_head, 0)


def _peer_scores(nt_bf, wq_t_bf, keys):
    d, t = nt_bf.shape
    tok = SCORE_TOK
    kern = functools.partial(_score_kernel, tok=tok)
    shape = (PEER_HEADS, PEER_N_KEYS, t)
    big_spec = pl.BlockSpec((PEER_HEADS, PEER_N_KEYS, tok), lambda i: (0, 0, i))
    return pl.pallas_call(
        kern,
        grid=(t // tok,),
        in_specs=[pl.BlockSpec((d, tok), lambda i: (0, i)),
                  pl.BlockSpec(wq_t_bf.shape, lambda i: (0, 0)),
                  pl.BlockSpec(keys.shape, lambda i: (0, 0, 0))],
        out_specs=[big_spec, big_spec,
                   pl.BlockSpec((PEER_HEADS, tok), lambda i: (0, i))],
        out_shape=[jax.ShapeDtypeStruct(shape, F32), jax.ShapeDtypeStruct(shape, F32),
                   jax.ShapeDtypeStruct((PEER_HEADS, t), F32)],
        scratch_shapes=[pltpu.VMEM((wq_t_bf.shape[0], tok), F32),
                        pltpu.VMEM((2, PEER_HEADS, PEER_N_KEYS, tok), F32),
                        pltpu.VMEM((2, PEER_TOPK, tok // LANES, PEER_HEADS, LANES), F32),
                        pltpu.VMEM((tok // LANES, PEER_HEADS, LANES), F32)],
        compiler_params=_params(("parallel",), 48),
        name="peer_scores",
    )(nt_bf, wq_t_bf, keys)


def _gelu_times(x, w):
    k0 = -2.0 * 0.7978845608028654 * 1.4426950408889634
    k1 = k0 * 0.044715
    e = jnp.exp2(x * (k0 + k1 * (x * x)))
    return (x * w) / (1.0 + e)


def _peer_kernel(nt_ref, u_ref, vt_ref, g1_ref, e2_ref, thr_ref, out_ref,
                 act0_ref, act1_ref, a0_ref, a1_ref, *, tok, n_exp, k_steps):
    g = pl.program_id(0)
    rows_per_step = n_exp // PEER_N_KEYS

    @pl.when(g == 0)
    def _():
        act1_ref[...] = jnp.zeros_like(act1_ref)
        a0_ref[...] = jnp.zeros_like(a0_ref)

    @pl.when(jnp.logical_or(g < 2, lax.rem(jnp.maximum(g - 2, 0), k_steps) == 0))
    def _():
        out_ref[...] = jnp.zeros_like(out_ref)

    kb = lax.rem(jnp.maximum(g - 1, 0), k_steps)
    i0 = pl.multiple_of(kb * rows_per_step, rows_per_step)

    def body(act_w, act_r, a_w, a_r):
        d = out_ref.shape[0]
        assert tok == MXU_COUNT * MXU_TILE and d % MXU_TILE == 0 and n_exp % MXU_TILE == 0
        k_tiles = {0: d // MXU_TILE, 1: n_exp // MXU_TILE}
        m_slices = {0: n_exp // MXU_TILE, 1: d // MXU_TILE}
        chunks = MXU_TILE // MM_ROWS
        acc_entries = MXU_TILE // 4

        def rhs_tile(mm, k, q):
            src = nt_ref if mm == 0 else a_r
            return src[k * MXU_TILE:(k + 1) * MXU_TILE, q * MXU_TILE:(q + 1) * MXU_TILE]

        def unit(mm, s, k, c):
            lhs_ref = u_ref if mm == 0 else vt_ref
            base = (2 * mm + s % 2) * acc_entries
            r0 = s * MXU_TILE + c * MM_ROWS
            lhs = lhs_ref[r0:r0 + MM_ROWS, k * MXU_TILE:(k + 1) * MXU_TILE]
            reg = k % 2
            for q in range(MXU_COUNT):
                if c == 0 and k == 0:
                    pltpu.matmul_push_rhs(rhs_tile(mm, 0, q), staging_register=reg, mxu_index=q)
                pltpu.matmul_acc_lhs(base + c * (MM_ROWS // 4), lhs, mxu_index=q,
                                     load_staged_rhs=reg if c == 0 else None)
                if c == 0 and k + 1 < k_tiles[mm]:
                    pltpu.matmul_push_rhs(rhs_tile(mm, k + 1, q), staging_register=1 - reg,
                                          mxu_index=q)

        def drain(mm, s):
            base = (2 * mm + s % 2) * acc_entries
            rows = slice(s * MXU_TILE, (s + 1) * MXU_TILE)
            for q in range(MXU_COUNT):
                cols = slice(q * MXU_TILE, (q + 1) * MXU_TILE)
                res = pltpu.matmul_pop(base, (MXU_TILE, MXU_TILE), F32, q)
                if mm == 0:
                    act_w[rows, cols] = res
                else:
                    out_ref[rows, cols] += res

        units = [(mm, s, k, c) for s in range(max(m_slices.values())) for mm in (0, 1)
                 if s < m_slices[mm] for k in range(k_tiles[mm]) for c in range(chunks)]
        pending = []

        def issue(idx):
            if idx < len(units):
                mm, s, k, c = units[idx]
                unit(mm, s, k, c)
                if k == k_tiles[mm] - 1 and c == chunks - 1:
                    pending.append((idx + POP_LAG_UNITS, mm, s))
            while pending and pending[0][0] <= idx:
                _, mm, s = pending.pop(0)
                drain(mm, s)

        slot = 0
        for ii in range(rows_per_step):
            for lc in range(tok // LANES):
                cols = slice(lc * LANES, (lc + 1) * LANES)
                g1b = [jnp.broadcast_to(
                    g1_ref[h, pl.ds(i0, rows_per_step), cols][ii:ii + 1, :], (W_ROWS, LANES))
                    for h in range(PEER_HEADS)]
                thrb = [jnp.broadcast_to(thr_ref[h:h + 1, cols], (W_ROWS, LANES))
                        for h in range(PEER_HEADS)]
                for jt in range(PEER_N_KEYS // W_ROWS):
                    keys = slice(jt * W_ROWS, (jt + 1) * W_ROWS)
                    rows = slice(ii * PEER_N_KEYS + jt * W_ROWS,
                                 ii * PEER_N_KEYS + (jt + 1) * W_ROWS)
                    parts = []
                    for h in range(PEER_HEADS):
                        p = e2_ref[h, keys, cols] * g1b[h]
                        parts.append(jnp.where(p >= thrb[h], p, 0.0))
                    while len(parts) > 1:
                        parts = [parts[i] + parts[i + 1] for i in range(0, len(parts), 2)]
                    a_w[rows, cols] = _gelu_times(act_r[rows, cols], parts[0]).astype(BF16)
                    issue(slot)
                    slot += 1
        while slot < len(units) or pending:
            issue(slot)
            slot += 1

    @pl.when(lax.rem(g, 2) == 0)
    def _():
        body(act0_ref, act1_ref, a1_ref, a0_ref)

    @pl.when(lax.rem(g, 2) == 1)
    def _():
        body(act1_ref, act0_ref, a0_ref, a1_ref)


def _peer_dense(nt_bf, u_bf, vt_bf, g1, e2, thr):
    d, t = nt_bf.shape
    n_experts = u_bf.shape[0]
    tok, n_exp = PEER_TOK, PEER_EXP
    n_tok = t // tok
    k_steps = n_experts // n_exp
    kern = functools.partial(_peer_kernel, tok=tok, n_exp=n_exp, k_steps=k_steps)

    def tok_tile(lag):
        return lambda g: jnp.clip((g - lag) // k_steps, 0, n_tok - 1)

    def exp_tile(lag):
        return lambda g: jnp.maximum(g - lag, 0) % k_steps

    big_spec = pl.BlockSpec((PEER_HEADS, PEER_N_KEYS, tok), lambda g: (0, 0, tok_tile(1)(g)))
    return pl.pallas_call(
        kern,
        grid=(n_tok * k_steps + 2,),
        in_specs=[pl.BlockSpec((d, tok), lambda g: (0, tok_tile(0)(g))),
                  pl.BlockSpec((n_exp, d), lambda g: (exp_tile(0)(g), 0)),
                  pl.BlockSpec((d, n_exp), lambda g: (0, exp_tile(2)(g))),
                  big_spec, big_spec,
                  pl.BlockSpec((PEER_HEADS, tok), lambda g: (0, tok_tile(1)(g)))],
        out_specs=pl.BlockSpec((d, tok), lambda g: (0, tok_tile(2)(g))),
        out_shape=jax.ShapeDtypeStruct((d, t), F32),
        scratch_shapes=[pltpu.VMEM((n_exp, tok), F32), pltpu.VMEM((n_exp, tok), F32),
                        pltpu.VMEM((n_exp, tok), BF16), pltpu.VMEM((n_exp, tok), BF16)],
        compiler_params=_params(("arbitrary",), 48),
        name="peer_dense",
    )(nt_bf, u_bf, vt_bf, g1, e2, thr)


def _final_kernel(h_ref, pt_ref, g_ref, o_ref):
    o_ref[...] = _rms(h_ref[...] + pt_ref[...].T, g_ref[...])


def _final(h, peer_t, g):
    t, d = h.shape
    return pl.pallas_call(
        _final_kernel,
        grid=(t // ROW_TILE,),
        in_specs=[pl.BlockSpec((ROW_TILE, d), lambda i: (i, 0)),
                  pl.BlockSpec((d, ROW_TILE), lambda i: (0, i)),
                  pl.BlockSpec((1, d), lambda i: (0, 0))],
        out_specs=pl.BlockSpec((ROW_TILE, d), lambda i: (i, 0)),
        out_shape=jax.ShapeDtypeStruct((t, d), F32),
        compiler_params=_params(("parallel",), 24),
        name="final_norm",
    )(h, peer_t, g)


def _block_diag_chunks(w, ch):
    heads, hd, _ = w.shape
    per = ch // hd
    w4 = w.reshape(heads // per, per, hd, hd)
    eye = jnp.eye(per, dtype=w.dtype)
    return jnp.einsum("chij,hg->chigj", w4, eye).reshape(heads // per, ch, ch)


def _layer(h2, batch, seq, mix_norm_g, w_in, lru_conv_w, lru_conv_b, lru_w_rg, lru_b_rg,
           lru_w_ig, lru_b_ig, lru_lambda, conf_conv_w, conf_conv_b, conf_norm_g,
           conf_norm_b, beta_lru, beta_conv, w_out, ffn_norm_g, peer_w_q, peer_sub_keys,
           peer_u, peer_v):
    t, d = h2.shape
    d_lru = lru_conv_w.shape[1]
    d_conv = conf_conv_w.shape[1]
    row = lambda v: v.reshape(1, -1)

    z = _inproj(h2, row(mix_norm_g), w_in.astype(BF16))
    z3 = z.reshape(batch, seq, z.shape[1])

    ch = MIX_CH
    nc = d_lru // ch
    wg = jnp.concatenate([_block_diag_chunks(lru_w_rg[0], ch), _block_diag_chunks(lru_w_ig[0], ch),
                          _block_diag_chunks(lru_w_rg[1], ch), _block_diag_chunks(lru_w_ig[1], ch)],
                         axis=-1)
    bg = jnp.concatenate([lru_b_rg[0].reshape(nc, 1, ch), lru_b_ig[0].reshape(nc, 1, ch),
                          lru_b_rg[1].reshape(nc, 1, ch), lru_b_ig[1].reshape(nc, 1, ch)],
                         axis=-1)
    y_lru = _lru_branch(z3, lru_conv_w, row(lru_conv_b), wg.astype(BF16), bg, lru_lambda, d_lru)

    gdim = d_conv // CONV_GROUPS
    grp = jnp.arange(LANES) // gdim
    avg = ((grp[:, None] == grp[None, :]).astype(F32) / gdim).astype(BF16)
    y_conv = _conf_branch(z3, conf_conv_w, row(conf_conv_b), row(conf_norm_g),
                          row(conf_norm_b), avg, d_lru, d_conv)

    w_out_bf = w_out.astype(BF16)
    h2, nt_bf = _outproj(y_lru.reshape(t, d_lru), y_conv.reshape(t, d_conv), h2,
                             row(beta_lru), row(beta_conv), w_out_bf[:d_lru], w_out_bf[d_lru:],
                             row(ffn_norm_g))

    keys = peer_sub_keys.reshape(2 * PEER_HEADS, PEER_N_KEYS, -1)
    g1, e2, thr = _peer_scores(nt_bf, peer_w_q.T.astype(BF16), keys)
    peer_t = _peer_dense(nt_bf, _cast(peer_u, BF16), _transpose_cast(peer_v, BF16),
                         g1, e2, thr)
    return h2, peer_t


def kernel(x, mix_norm_g, w_in, lru_conv_w, lru_conv_b, lru_w_rg, lru_b_rg, lru_w_ig, lru_b_ig, lru_lambda, conf_conv_w, conf_conv_b, conf_norm_g, conf_norm_b, beta_lru, beta_conv, w_out, ffn_norm_g, peer_w_q, peer_sub_keys, peer_u, peer_v, final_norm_g):
    batch, seq, d = x.shape
    depth = w_in.shape[0]
    h2 = x.reshape(batch * seq, d)
    peer_t = None
    for l in range(depth):
        if peer_t is not None:
            h2 = h2 + peer_t.T
        h2, peer_t = _layer(
            h2, batch, seq, mix_norm_g[l], w_in[l], lru_conv_w[l], lru_conv_b[l], lru_w_rg[l],
            lru_b_rg[l], lru_w_ig[l], lru_b_ig[l], lru_lambda[l], conf_conv_w[l], conf_conv_b[l],
            conf_norm_g[l], conf_norm_b[l], beta_lru[l], beta_conv[l], w_out[l], ffn_norm_g[l],
            peer_w_q[l], peer_sub_keys[l], peer_u[l], peer_v[l])
    out = _final(h2, peer_t, final_norm_g.reshape(1, -1))
    return out.reshape(batch, seq, d)
```

```python
import functools

import jax
import jax.numpy as jnp
from jax import lax
from jax.experimental import pallas as pl
from jax.experimental.pallas import tpu as pltpu

F32 = jnp.float32
BF16 = jnp.bfloat16
HIGHEST = lax.Precision.HIGHEST

SUBLANES = 8
LANES = 128
MIB = 1024 * 1024

EPS = 1e-6
LRU_C = 8.0
LRU_HEADS = 8
LRU_CONV_WIDTH = 4
CONV_GROUPS = 8
CONF_KERNEL = 31
PEER_HEADS = 8
PEER_N_KEYS = 128
PEER_TOPK = 16

ROW_TILE = 512
MIX_CH = 256
LRU_ROWS = 128
CONV_ROWS = 128
NORM_ROWS = 128
NORM_BLOCKS = 4
SCORE_TOK = 512
PEER_TOK = 512
PEER_EXP = 1024
W_ROWS = 8
MXU_COUNT = 2
MXU_TILE = 256
MM_ROWS = 16
POP_LAG_UNITS = 28


def _params(semantics, vmem_mib):
    return pltpu.CompilerParams(dimension_semantics=semantics,
                                vmem_limit_bytes=vmem_mib * MIB)


def _rms(x, g):
    return x * lax.rsqrt(jnp.mean(x * x, axis=-1, keepdims=True) + EPS) * g


def _gelu_tanh(x):
    c = 0.7978845608028654
    return x * (0.5 * (1.0 + jnp.tanh(c * (x + 0.044715 * (x * x * x)))))


def _sigmoid(x):
    return 1.0 / (1.0 + jnp.exp(-x))


def _log_sigmoid(x):
    return -(jnp.maximum(-x, 0.0) + jnp.log(1.0 + jnp.exp(-jnp.abs(x))))


def _transpose_cast_kernel(x_ref, o_ref):
    o_ref[...] = x_ref[...].T.astype(o_ref.dtype)


def _transpose_cast(x, dtype, tile=512):
    r, c = x.shape
    return pl.pallas_call(
        _transpose_cast_kernel,
        grid=(r // tile,),
        in_specs=[pl.BlockSpec((tile, c), lambda i: (i, 0))],
        out_specs=pl.BlockSpec((c, tile), lambda i: (0, i)),
        out_shape=jax.ShapeDtypeStruct((c, r), dtype),
        compiler_params=_params(("parallel",), 24),
        name="transpose_cast",
    )(x)


def _cast_kernel(x_ref, o_ref):
    o_ref[...] = x_ref[...].astype(o_ref.dtype)


def _cast(x, dtype, tile=512):
    r, c = x.shape
    return pl.pallas_call(
        _cast_kernel,
        grid=(r // tile,),
        in_specs=[pl.BlockSpec((tile, c), lambda i: (i, 0))],
        out_specs=pl.BlockSpec((tile, c), lambda i: (i, 0)),
        out_shape=jax.ShapeDtypeStruct((r, c), dtype),
        compiler_params=_params(("parallel",), 24),
        name="cast",
    )(x)


def _inproj_kernel(x_ref, g_ref, w_ref, z_ref):
    n = _rms(x_ref[...], g_ref[...])
    z_ref[...] = jnp.dot(n.astype(BF16), w_ref[...], preferred_element_type=F32)


def _inproj(x2, g, w_bf):
    t, d = x2.shape
    e = w_bf.shape[1]
    return pl.pallas_call(
        _inproj_kernel,
        grid=(t // ROW_TILE,),
        in_specs=[pl.BlockSpec((ROW_TILE, d), lambda i: (i, 0)),
                  pl.BlockSpec((1, d), lambda i: (0, 0)),
                  pl.BlockSpec((d, e), lambda i: (0, 0))],
        out_specs=pl.BlockSpec((ROW_TILE, e), lambda i: (i, 0)),
        out_shape=jax.ShapeDtypeStruct((t, e), F32),
        compiler_params=_params(("parallel",), 40),
        name="inproj",
    )(x2, g, w_bf)


def _scan_group(a_ref, u_ref, r0, carry, row, reverse):
    a = a_ref[pl.ds(r0, SUBLANES), :]
    u = u_ref[pl.ds(r0, SUBLANES), :]
    for d in (1, 2, 4):
        if reverse:
            shift, keep = SUBLANES - d, row < SUBLANES - d
        else:
            shift, keep = d, row >= d
        a_nb = pltpu.roll(a, shift, 0)
        u_nb = pltpu.roll(u, shift, 0)
        u = u + a * jnp.where(keep, u_nb, 0.0)
        a = a * jnp.where(keep, a_nb, 1.0)
    h = u + a * carry
    u_ref[pl.ds(r0, SUBLANES), :] = h
    edge = h[0:1] if reverse else h[SUBLANES - 1:SUBLANES]
    return jnp.broadcast_to(edge, h.shape)


def _scans_in_place(af_ref, uf_ref, ab_ref, ub_ref, seq, ch):
    groups = seq // SUBLANES
    row = lax.broadcasted_iota(jnp.int32, (SUBLANES, ch), 0)

    def body(g, carries):
        cf, cb = carries
        rf = pl.multiple_of(g * SUBLANES, SUBLANES)
        rb = pl.multiple_of((groups - 1 - g) * SUBLANES, SUBLANES)
        return (_scan_group(af_ref, uf_ref, rf, cf, row, False),
                _scan_group(ab_ref, ub_ref, rb, cb, row, True))

    zero = jnp.zeros((SUBLANES, ch), F32)
    lax.fori_loop(0, groups, body, (zero, zero), unroll=4)


def _lru_kernel(x_ref, gate_ref, cw_ref, cb_ref, wg_ref, bg_ref, lam_ref, y_ref,
                xpad_ref, af_ref, uf_ref, ab_ref, ub_ref, pre_ref, *, seq, ch):
    pad = SUBLANES
    zeros = jnp.zeros((pad, ch), F32)
    xpad_ref[pl.ds(0, pad), :] = zeros
    xpad_ref[pl.ds(seq + pad, pad), :] = zeros
    xpad_ref[pl.ds(pad, seq), :] = x_ref[...]

    cw = cw_ref[...]
    cb = cb_ref[...]
    bg = bg_ref[...]
    log_sig = _log_sigmoid(lam_ref[...])
    lpad = LRU_CONV_WIDTH // 2

    def conv(c, _):
        t0 = pl.multiple_of(c * LRU_ROWS, LRU_ROWS)
        win = xpad_ref[pl.ds(t0, LRU_ROWS + 2 * pad), :]
        xc = jnp.zeros((LRU_ROWS, ch), F32) + cb
        for k in range(LRU_CONV_WIDTH):
            off = pad - lpad + k
            xc = xc + cw[k:k + 1, :] * win[off:off + LRU_ROWS, :]
        af_ref[pl.ds(t0, LRU_ROWS), :] = xc
        return 0

    lax.fori_loop(0, seq // LRU_ROWS, conv, 0)

    pre_ref[...] = jnp.dot(af_ref[...].astype(BF16), wg_ref[...], preferred_element_type=F32)

    def gates(c, _):
        t0 = pl.multiple_of(c * LRU_ROWS, LRU_ROWS)
        xc = af_ref[pl.ds(t0, LRU_ROWS), :]
        pre = pre_ref[pl.ds(t0, LRU_ROWS), :] + bg
        for d, (a_ref, u_ref) in enumerate(((af_ref, uf_ref), (ab_ref, ub_ref))):
            r = _sigmoid(pre[:, (2 * d) * ch:(2 * d + 1) * ch])
            i = _sigmoid(pre[:, (2 * d + 1) * ch:(2 * d + 2) * ch])
            a = jnp.exp(LRU_C * r * log_sig[d:d + 1, :])
            a_ref[pl.ds(t0, LRU_ROWS), :] = a
            u_ref[pl.ds(t0, LRU_ROWS), :] = jnp.sqrt(1.0 - a * a) * (i * xc)
        return 0

    lax.fori_loop(0, seq // LRU_ROWS, gates, 0)

    _scans_in_place(af_ref, uf_ref, ab_ref, ub_ref, seq, ch)

    def finish(c, _):
        t0 = pl.multiple_of(c * LRU_ROWS, LRU_ROWS)
        h = uf_ref[pl.ds(t0, LRU_ROWS), :] + ub_ref[pl.ds(t0, LRU_ROWS), :]
        y_ref[pl.ds(t0, LRU_ROWS), :] = h * _gelu_tanh(gate_ref[pl.ds(t0, LRU_ROWS), :])
        return 0

    lax.fori_loop(0, seq // LRU_ROWS, finish, 0)


def _lru_branch(z3, conv_w, conv_b, wg, bg, lam, d_lru):
    b, s, _ = z3.shape
    ch = MIX_CH
    nc = d_lru // ch
    kern = functools.partial(_lru_kernel, seq=s, ch=ch)
    return pl.pallas_call(
        kern,
        grid=(b, nc),
        in_specs=[pl.BlockSpec((None, s, ch), lambda i, c: (i, 0, c)),
                  pl.BlockSpec((None, s, ch), lambda i, c: (i, 0, nc + c)),
                  pl.BlockSpec((LRU_CONV_WIDTH, ch), lambda i, c: (0, c)),
                  pl.BlockSpec((1, ch), lambda i, c: (0, c)),
                  pl.BlockSpec((None, ch, 4 * ch), lambda i, c: (c, 0, 0)),
                  pl.BlockSpec((None, 1, 4 * ch), lambda i, c: (c, 0, 0)),
                  pl.BlockSpec((2, ch), lambda i, c: (0, c))],
        out_specs=pl.BlockSpec((None, s, ch), lambda i, c: (i, 0, c)),
        out_shape=jax.ShapeDtypeStruct((b, s, d_lru), F32),
        scratch_shapes=[pltpu.VMEM((s + 2 * SUBLANES, ch), F32)]
        + [pltpu.VMEM((s, ch), F32)] * 4 + [pltpu.VMEM((s, 4 * ch), F32)],
        compiler_params=_params(("parallel", "parallel"), 48),
        name="lru_branch",
    )(z3, z3, conv_w, conv_b, wg, bg, lam)


def _conf_kernel(a_ref, b_ref, cw_ref, cb_ref, ng_ref, nb_ref, avg_ref, y_ref,
                 gpad_ref, shift_ref, *, seq, ch):
    pad = 2 * SUBLANES
    half = CONF_KERNEL // 2
    zeros = jnp.zeros((pad, ch), F32)
    gpad_ref[pl.ds(0, pad), :] = zeros
    gpad_ref[pl.ds(seq + pad, pad), :] = zeros
    gpad_ref[pl.ds(pad, seq), :] = a_ref[...] * _sigmoid(b_ref[...])

    avg = avg_ref[...]
    span = CONV_ROWS + pad + SUBLANES

    def group_mean(v):
        hi = v.astype(BF16)
        r1 = v - hi.astype(F32)
        mid = r1.astype(BF16)
        lo = (r1 - mid.astype(F32)).astype(BF16)
        return (jnp.dot(hi, avg, preferred_element_type=F32)
                + jnp.dot(mid, avg, preferred_element_type=F32)
                + jnp.dot(lo, avg, preferred_element_type=F32))

    def conv_chunk(c, _):
        t0 = pl.multiple_of(c * CONV_ROWS, CONV_ROWS)
        for lc in range(ch // LANES):
            cols = slice(lc * LANES, (lc + 1) * LANES)
            cw = cw_ref[:, cols]
            win = gpad_ref[pl.ds(t0, CONV_ROWS + 2 * pad), cols]
            acc = jnp.zeros((CONV_ROWS, LANES), F32) + cb_ref[:, cols]
            for b in range(SUBLANES):
                shift_ref[lc, b] = win[b:b + span, :]
            for b in range(SUBLANES):
                for a in range(span // SUBLANES - CONV_ROWS // SUBLANES + 1):
                    k = SUBLANES * a + b - (pad - half)
                    if 0 <= k < CONF_KERNEL:
                        rows = slice(SUBLANES * a, SUBLANES * a + CONV_ROWS)
                        acc = acc + cw[k:k + 1, :] * shift_ref[lc, b, rows, :]
            y_ref[pl.ds(t0, CONV_ROWS), cols] = acc
        return 0

    lax.fori_loop(0, seq // CONV_ROWS, conv_chunk, 0)

    def norm_chunk(c, _):
        blocks = []
        for r in range(NORM_BLOCKS):
            t0 = pl.multiple_of((c * NORM_BLOCKS + r) * NORM_ROWS, NORM_ROWS)
            for lc in range(ch // LANES):
                blocks.append((pl.ds(t0, NORM_ROWS), slice(lc * LANES, (lc + 1) * LANES)))
        xs = [y_ref[rows, cols] for rows, cols in blocks]
        means = [group_mean(x) for x in xs]
        devs = [x - m for x, m in zip(xs, means)]
        variances = [group_mean(dev * dev) for dev in devs]
        for (rows, cols), dev, var in zip(blocks, devs, variances):
            y = dev * lax.rsqrt(var + EPS) * ng_ref[:, cols] + nb_ref[:, cols]
            y_ref[rows, cols] = y * _sigmoid(y)
        return 0

    lax.fori_loop(0, seq // (NORM_ROWS * NORM_BLOCKS), norm_chunk, 0)


def _conf_branch(z3, conv_w, conv_b, norm_g, norm_b, avg, d_lru, d_conv):
    b, s, _ = z3.shape
    ch = MIX_CH
    nc = d_conv // ch
    base = 2 * d_lru // ch
    kern = functools.partial(_conf_kernel, seq=s, ch=ch)
    return pl.pallas_call(
        kern,
        grid=(b, nc),
        in_specs=[pl.BlockSpec((None, s, ch), lambda i, c: (i, 0, base + c)),
                  pl.BlockSpec((None, s, ch), lambda i, c: (i, 0, base + nc + c)),
                  pl.BlockSpec((CONF_KERNEL, ch), lambda i, c: (0, c)),
                  pl.BlockSpec((1, ch), lambda i, c: (0, c)),
                  pl.BlockSpec((1, ch), lambda i, c: (0, c)),
                  pl.BlockSpec((1, ch), lambda i, c: (0, c)),
                  pl.BlockSpec((LANES, LANES), lambda i, c: (0, 0))],
        out_specs=pl.BlockSpec((None, s, ch), lambda i, c: (i, 0, c)),
        out_shape=jax.ShapeDtypeStruct((b, s, d_conv), F32),
        scratch_shapes=[pltpu.VMEM((s + 4 * SUBLANES, ch), F32),
                        pltpu.VMEM((ch // LANES, SUBLANES, CONV_ROWS + 3 * SUBLANES, LANES), F32)],
        compiler_params=_params(("parallel", "parallel"), 40),
        name="conf_branch",
    )(z3, z3, conv_w, conv_b, norm_g, norm_b, avg)


def _outproj_kernel(yl_ref, yc_ref, x_ref, bl_ref, bc_ref, wl_ref, wc_ref, fg_ref,
                    h_ref, ntb_ref):
    yl = _rms(yl_ref[...], bl_ref[...]).astype(BF16)
    yc = _rms(yc_ref[...], bc_ref[...]).astype(BF16)
    h = (x_ref[...]
         + jnp.dot(yl, wl_ref[...], preferred_element_type=F32)
         + jnp.dot(yc, wc_ref[...], preferred_element_type=F32))
    h_ref[...] = h
    ntb_ref[...] = _rms(h, fg_ref[...]).T.astype(BF16)


def _outproj(yl, yc, x2, beta_l, beta_c, wl_bf, wc_bf, ffn_g):
    t, d = x2.shape
    dl = yl.shape[1]
    dc = yc.shape[1]
    row = lambda i: (i, 0)
    fixed = lambda i: (0, 0)
    return pl.pallas_call(
        _outproj_kernel,
        grid=(t // ROW_TILE,),
        in_specs=[pl.BlockSpec((ROW_TILE, dl), row),
                  pl.BlockSpec((ROW_TILE, dc), row),
                  pl.BlockSpec((ROW_TILE, d), row),
                  pl.BlockSpec((1, dl), fixed),
                  pl.BlockSpec((1, dc), fixed),
                  pl.BlockSpec((dl, d), fixed),
                  pl.BlockSpec((dc, d), fixed),
                  pl.BlockSpec((1, d), fixed)],
        out_specs=[pl.BlockSpec((ROW_TILE, d), row),
                   pl.BlockSpec((d, ROW_TILE), lambda i: (0, i))],
        out_shape=[jax.ShapeDtypeStruct((t, d), F32),
                   jax.ShapeDtypeStruct((d, t), BF16)],
        compiler_params=_params(("parallel",), 40),
        name="outproj",
    )(yl, yc, x2, beta_l, beta_c, wl_bf, wc_bf, ffn_g)


def _sort_network(n):
    pairs = []
    p = 1
    while p < n:
        k = p
        while k >= 1:
            for j in range(k % p, n - k, 2 * k):
                for i in range(min(k, n - j - k)):
                    if (i + j) // (2 * p) == (i + j + k) // (2 * p):
                        pairs.append((i + j, i + j + k))
            k //= 2
        p *= 2
    return pairs


def _pruned_network(n_pow2, n_live, n_out):
    pairs = [(i, j) for i, j in _sort_network(n_pow2) if j < n_live]
    needed = set(range(n_out))
    kept = []
    for i, j in reversed(pairs):
        if i in needed or j in needed:
            kept.append((i, j))
            needed.update((i, j))
    return kept[::-1]


def _apply_network(vals, pairs):
    vals = list(vals)
    for i, j in pairs:
        hi = jnp.maximum(vals[i], vals[j])
        lo = jnp.minimum(vals[i], vals[j])
        vals[i], vals[j] = hi, lo
    return vals


def _top16_over_keys(s):
    k = PEER_TOPK
    blocks = [s[SUBLANES * v:SUBLANES * (v + 1), :] for v in range(PEER_N_KEYS // SUBLANES)]
    top = _apply_network(blocks, _sort_network(len(blocks)))
    for d in (1, 2, 4):
        top = [jnp.maximum(top[i], pltpu.roll(top[k - 1 - i], d, 0)) for i in range(k)]
        stride = k // 2
        while stride >= 1:
            pairs = [(i, i + stride) for i in range(k) if not i & stride]
            top = _apply_network(top, pairs)
            stride //= 2
    return top


def _staircase(k):
    return [(a, b) for a in range(k) for b in range(k) if (a + 1) * (b + 1) <= k]


def _k_largest(cands, k):
    n_pow2 = 1
    while n_pow2 < len(cands):
        n_pow2 *= 2
    return _apply_network(cands, _pruned_network(n_pow2, len(cands), k))[:k]


def _score_kernel(nt_ref, wq_ref, keys_ref, g1_ref, e2_ref, thr_ref,
                  q_ref, s_ref, top_ref, z_ref, *, tok):
    k = PEER_TOPK
    chunks = [(c, slice(c * LANES, (c + 1) * LANES)) for c in range(tok // LANES)]

    q_ref[...] = jnp.dot(wq_ref[...], nt_ref[...], preferred_element_type=F32)

    def per_half(hp, _):
        h = hp // 2
        p = hp % 2
        r0 = pl.multiple_of(hp * PEER_N_KEYS, PEER_N_KEYS)
        s = jnp.dot(keys_ref[hp], q_ref[pl.ds(r0, PEER_N_KEYS), :], precision=HIGHEST,
                    preferred_element_type=F32)
        s_ref[p, h] = s
        for c, cols in chunks:
            top = _top16_over_keys(s[:, cols])
            for i in range(k):
                top_ref[p, i, c, pl.ds(h, 1), :] = top[i][0:1, :]
        return 0

    lax.fori_loop(0, 2 * PEER_HEADS, per_half, 0, unroll=2)

    for c, cols in chunks:
        first = [top_ref[0, i, c] for i in range(k)]
        second = [top_ref[1, i, c] for i in range(k)]
        best = _k_largest([first[a] + second[b] for a, b in _staircase(k)], k)
        z = jnp.zeros_like(best[0])
        for v in best:
            z = z + jnp.exp(v - best[0])
        z_ref[c] = z
        inv_z = 1.0 / z
        g1_top = [jnp.exp(first[a] - first[0]) * inv_z for a in range(k)]
        e2_top = [jnp.exp(second[b] - second[0]) for b in range(k)]
        thr_ref[:, cols] = _k_largest([g1_top[a] * e2_top[b] for a, b in _staircase(k)], k)[k - 1]

    def per_head(h, _):
        for c, cols in chunks:
            inv_z = 1.0 / z_ref[c, pl.ds(h, 1), :]
            m1 = top_ref[0, 0, c, pl.ds(h, 1), :]
            m2 = top_ref[1, 0, c, pl.ds(h, 1), :]
            g1_ref[h, :, cols] = jnp.exp(s_ref[0, h, :, cols] - m1) * inv_z
            e2_ref[h, :, cols] = jnp.exp(s_ref[1, h, :, cols] - m2)
        return 0

    lax.fori_loop(0, PEER_HEADS, per_head, 0)


def _peer_scores(nt_bf, wq_t_bf, keys):
    d, t = nt_bf.shape
    tok = SCORE_TOK
    kern = functools.partial(_score_kernel, tok=tok)
    shape = (PEER_HEADS, PEER_N_KEYS, t)
    big_spec = pl.BlockSpec((PEER_HEADS, PEER_N_KEYS, tok), lambda i: (0, 0, i))
    return pl.pallas_call(
        kern,
        grid=(t // tok,),
        in_specs=[pl.BlockSpec((d, tok), lambda i: (0, i)),
                  pl.BlockSpec(wq_t_bf.shape, lambda i: (0, 0)),
                  pl.BlockSpec(keys.shape, lambda i: (0, 0, 0))],
        out_specs=[big_spec, big_spec,
                   pl.BlockSpec((PEER_HEADS, tok), lambda i: (0, i))],
        out_shape=[jax.ShapeDtypeStruct(shape, F32), jax.ShapeDtypeStruct(shape, F32),
                   jax.ShapeDtypeStruct((PEER_HEADS, t), F32)],
        scratch_shapes=[pltpu.VMEM((wq_t_bf.shape[0], tok), F32),
                        pltpu.VMEM((2, PEER_HEADS, PEER_N_KEYS, tok), F32),
                        pltpu.VMEM((2, PEER_TOPK, tok // LANES, PEER_HEADS, LANES), F32),
                        pltpu.VMEM((tok // LANES, PEER_HEADS, LANES), F32)],
        compiler_params=_params(("parallel",), 48),
        name="peer_scores",
    )(nt_bf, wq_t_bf, keys)


def _gelu_times(x, w):
    k0 = -2.0 * 0.7978845608028654 * 1.4426950408889634
    k1 = k0 * 0.044715
    e = jnp.exp2(x * (k0 + k1 * (x * x)))
    return (x * w) / (1.0 + e)


def _peer_kernel(nt_ref, u_ref, vt_ref, g1_ref, e2_ref, thr_ref, out_ref,
                 act0_ref, act1_ref, a0_ref, a1_ref, *, tok, n_exp, k_steps):
    g = pl.program_id(0)
    rows_per_step = n_exp // PEER_N_KEYS

    @pl.when(g == 0)
    def _():
        act1_ref[...] = jnp.zeros_like(act1_ref)
        a0_ref[...] = jnp.zeros_like(a0_ref)

    @pl.when(jnp.logical_or(g < 2, lax.rem(jnp.maximum(g - 2, 0), k_steps) == 0))
    def _():
        out_ref[...] = jnp.zeros_like(out_ref)

    kb = lax.rem(jnp.maximum(g - 1, 0), k_steps)
    i0 = pl.multiple_of(kb * rows_per_step, rows_per_step)

    def body(act_w, act_r, a_w, a_r):
        d = out_ref.shape[0]
        assert tok == MXU_COUNT * MXU_TILE and d % MXU_TILE == 0 and n_exp % MXU_TILE == 0
        k_tiles = {0: d // MXU_TILE, 1: n_exp // MXU_TILE}
        m_slices = {0: n_exp // MXU_TILE, 1: d // MXU_TILE}
        chunks = MXU_TILE // MM_ROWS
        acc_entries = MXU_TILE // 4

        def rhs_tile(mm, k, q):
            src = nt_ref if mm == 0 else a_r
            return src[k * MXU_TILE:(k + 1) * MXU_TILE, q * MXU_TILE:(q + 1) * MXU_TILE]

        def unit(mm, s, k, c):
            lhs_ref = u_ref if mm == 0 else vt_ref
            base = (2 * mm + s % 2) * acc_entries
            r0 = s * MXU_TILE + c * MM_ROWS
            lhs = lhs_ref[r0:r0 + MM_ROWS, k * MXU_TILE:(k + 1) * MXU_TILE]
            reg = k % 2
            for q in range(MXU_COUNT):
                if c == 0 and k == 0:
                    pltpu.matmul_push_rhs(rhs_tile(mm, 0, q), staging_register=reg, mxu_index=q)
                pltpu.matmul_acc_lhs(base + c * (MM_ROWS // 4), lhs, mxu_index=q,
                                     load_staged_rhs=reg if c == 0 else None)
                if c == 0 and k + 1 < k_tiles[mm]:
                    pltpu.matmul_push_rhs(rhs_tile(mm, k + 1, q), staging_register=1 - reg,
                                          mxu_index=q)

        def drain(mm, s):
            base = (2 * mm + s % 2) * acc_entries
            rows = slice(s * MXU_TILE, (s + 1) * MXU_TILE)
            for q in range(MXU_COUNT):
                cols = slice(q * MXU_TILE, (q + 1) * MXU_TILE)
                res = pltpu.matmul_pop(base, (MXU_TILE, MXU_TILE), F32, q)
                if mm == 0:
                    act_w[rows, cols] = res
                else:
                    out_ref[rows, cols] += res

        units = [(mm, s, k, c) for s in range(max(m_slices.values())) for mm in (0, 1)
                 if s < m_slices[mm] for k in range(k_tiles[mm]) for c in range(chunks)]
        pending = []

        def issue(idx):
            if idx < len(units):
                mm, s, k, c = units[idx]
                unit(mm, s, k, c)
                if k == k_tiles[mm] - 1 and c == chunks - 1:
                    pending.append((idx + POP_LAG_UNITS, mm, s))
            while pending and pending[0][0] <= idx:
                _, mm, s = pending.pop(0)
                drain(mm, s)

        slot = 0
        for ii in range(rows_per_step):
            for lc in range(tok // LANES):
                cols = slice(lc * LANES, (lc + 1) * LANES)
                g1b = [jnp.broadcast_to(
                    g1_ref[h, pl.ds(i0, rows_per_step), cols][ii:ii + 1, :], (W_ROWS, LANES))
                    for h in range(PEER_HEADS)]
                thrb = [jnp.broadcast_to(thr_ref[h:h + 1, cols], (W_ROWS, LANES))
                        for h in range(PEER_HEADS)]
                for jt in range(PEER_N_KEYS // W_ROWS):
                    keys = slice(jt * W_ROWS, (jt + 1) * W_ROWS)
                    rows = slice(ii * PEER_N_KEYS + jt * W_ROWS,
                                 ii * PEER_N_KEYS + (jt + 1) * W_ROWS)
                    parts = []
                    for h in range(PEER_HEADS):
                        p = e2_ref[h, keys, cols] * g1b[h]
                        parts.append(jnp.where(p >= thrb[h], p, 0.0))
                    while len(parts) > 1:
                        parts = [parts[i] + parts[i + 1] for i in range(0, len(parts), 2)]
                    a_w[rows, cols] = _gelu_times(act_r[rows, cols], parts[0]).astype(BF16)
                    issue(slot)
                    slot += 1
        while slot < len(units) or pending:
            issue(slot)
            slot += 1

    @pl.when(lax.rem(g, 2) == 0)
    def _():
        body(act0_ref, act1_ref, a1_ref, a0_ref)

    @pl.when(lax.rem(g, 2) == 1)
    def _():
        body(act1_ref, act0_ref, a0_ref, a1_ref)


def _peer_dense(nt_bf, u_bf, vt_bf, g1, e2, thr):
    d, t = nt_bf.shape
    n_experts = u_bf.shape[0]
    tok, n_exp = PEER_TOK, PEER_EXP
    n_tok = t // tok
    k_steps = n_experts // n_exp
    kern = functools.partial(_peer_kernel, tok=tok, n_exp=n_exp, k_steps=k_steps)

    def tok_tile(lag):
        return lambda g: jnp.clip((g - lag) // k_steps, 0, n_tok - 1)

    def exp_tile(lag):
        return lambda g: jnp.maximum(g - lag, 0) % k_steps

    big_spec = pl.BlockSpec((PEER_HEADS, PEER_N_KEYS, tok), lambda g: (0, 0, tok_tile(1)(g)))
    return pl.pallas_call(
        kern,
        grid=(n_tok * k_steps + 2,),
        in_specs=[pl.BlockSpec((d, tok), lambda g: (0, tok_tile(0)(g))),
                  pl.BlockSpec((n_exp, d), lambda g: (exp_tile(0)(g), 0)),
                  pl.BlockSpec((d, n_exp), lambda g: (0, exp_tile(2)(g))),
                  big_spec, big_spec,
                  pl.BlockSpec((PEER_HEADS, tok), lambda g: (0, tok_tile(1)(g)))],
        out_specs=pl.BlockSpec((d, tok), lambda g: (0, tok_tile(2)(g))),
        out_shape=jax.ShapeDtypeStruct((d, t), F32),
        scratch_shapes=[pltpu.VMEM((n_exp, tok), F32), pltpu.VMEM((n_exp, tok), F32),
                        pltpu.VMEM((n_exp, tok), BF16), pltpu.VMEM((n_exp, tok), BF16)],
        compiler_params=_params(("arbitrary",), 48),
        name="peer_dense",
    )(nt_bf, u_bf, vt_bf, g1, e2, thr)


def _final_kernel(h_ref, pt_ref, g_ref, o_ref):
    o_ref[...] = _rms(h_ref[...] + pt_ref[...].T, g_ref[...])


def _final(h, peer_t, g):
    t, d = h.shape
    return pl.pallas_call(
        _final_kernel,
        grid=(t // ROW_TILE,),
        in_specs=[pl.BlockSpec((ROW_TILE, d), lambda i: (i, 0)),
                  pl.BlockSpec((d, ROW_TILE), lambda i: (0, i)),
                  pl.BlockSpec((1, d), lambda i: (0, 0))],
        out_specs=pl.BlockSpec((ROW_TILE, d), lambda i: (i, 0)),
        out_shape=jax.ShapeDtypeStruct((t, d), F32),
        compiler_params=_params(("parallel",), 24),
        name="final_norm",
    )(h, peer_t, g)


def _block_diag_chunks(w, ch):
    heads, hd, _ = w.shape
    per = ch // hd
    w4 = w.reshape(heads // per, per, hd, hd)
    eye = jnp.eye(per, dtype=w.dtype)
    return jnp.einsum("chij,hg->chigj", w4, eye).reshape(heads // per, ch, ch)


def _layer(h2, batch, seq, mix_norm_g, w_in, lru_conv_w, lru_conv_b, lru_w_rg, lru_b_rg,
           lru_w_ig, lru_b_ig, lru_lambda, conf_conv_w, conf_conv_b, conf_norm_g,
           conf_norm_b, beta_lru, beta_conv, w_out, ffn_norm_g, peer_w_q, peer_sub_keys,
           peer_u, peer_v):
    t, d = h2.shape
    d_lru = lru_conv_w.shape[1]
    d_conv = conf_conv_w.shape[1]
    row = lambda v: v.reshape(1, -1)

    z = _inproj(h2, row(mix_norm_g), w_in.astype(BF16))
    z3 = z.reshape(batch, seq, z.shape[1])

    ch = MIX_CH
    nc = d_lru // ch
    wg = jnp.concatenate([_block_diag_chunks(lru_w_rg[0], ch), _block_diag_chunks(lru_w_ig[0], ch),
                          _block_diag_chunks(lru_w_rg[1], ch), _block_diag_chunks(lru_w_ig[1], ch)],
                         axis=-1)
    bg = jnp.concatenate([lru_b_rg[0].reshape(nc, 1, ch), lru_b_ig[0].reshape(nc, 1, ch),
                          lru_b_rg[1].reshape(nc, 1, ch), lru_b_ig[1].reshape(nc, 1, ch)],
                         axis=-1)
    y_lru = _lru_branch(z3, lru_conv_w, row(lru_conv_b), wg.astype(BF16), bg, lru_lambda, d_lru)

    gdim = d_conv // CONV_GROUPS
    grp = jnp.arange(LANES) // gdim
    avg = ((grp[:, None] == grp[None, :]).astype(F32) / gdim).astype(BF16)
    y_conv = _conf_branch(z3, conf_conv_w, row(conf_conv_b), row(conf_norm_g),
                          row(conf_norm_b), avg, d_lru, d_conv)

    w_out_bf = w_out.astype(BF16)
    h2, nt_bf = _outproj(y_lru.reshape(t, d_lru), y_conv.reshape(t, d_conv), h2,
                             row(beta_lru), row(beta_conv), w_out_bf[:d_lru], w_out_bf[d_lru:],
                             row(ffn_norm_g))

    keys = peer_sub_keys.reshape(2 * PEER_HEADS, PEER_N_KEYS, -1)
    g1, e2, thr = _peer_scores(nt_bf, peer_w_q.T.astype(BF16), keys)
    peer_t = _peer_dense(nt_bf, peer_u, _transpose_cast(peer_v, F32), g1, e2, thr)
    return h2, peer_t


def kernel(x, mix_norm_g, w_in, lru_conv_w, lru_conv_b, lru_w_rg, lru_b_rg, lru_w_ig, lru_b_ig, lru_lambda, conf_conv_w, conf_conv_b, conf_norm_g, conf_norm_b, beta_lru, beta_conv, w_out, ffn_norm_g, peer_w_q, peer_sub_keys, peer_u, peer_v, final_norm_g):
    batch, seq, d = x.shape
    depth = w_in.shape[0]
    h2 = x.reshape(batch * seq, d)
    peer_t = None
    for l in range(depth):
        if peer_t is not None:
            h2 = h2 + peer_t.T
        h2, peer_t = _layer(
            h2, batch, seq, mix_norm_g[l], w_in[l], lru_conv_w[l], lru_conv_b[l], lru_w_rg[l],
            lru_b_rg[l], lru_w_ig[l], lru_b_ig[l], lru_lambda[l], conf_conv_w[l], conf_conv_b[l],
            conf_norm_g[l], conf_norm_b[l], beta_lru[l], beta_conv[l], w_out[l], ffn_norm_g[l],
            peer_w_q[l], peer_sub_keys[l], peer_u[l], peer_v[l])
    out = _final(h2, peer_t, final_norm_g.reshape(1, -1))
    return out.reshape(batch, seq, d)
```

```python
import functools

import jax
import jax.numpy as jnp
from jax import lax
from jax.experimental import pallas as pl
from jax.experimental.pallas import tpu as pltpu

F32 = jnp.float32
BF16 = jnp.bfloat16
HIGHEST = lax.Precision.HIGHEST

SUBLANES = 8
LANES = 128
MIB = 1024 * 1024

EPS = 1e-6
LRU_C = 8.0
LRU_HEADS = 8
LRU_CONV_WIDTH = 4
CONV_GROUPS = 8
CONF_KERNEL = 31
PEER_HEADS = 8
PEER_N_KEYS = 128
PEER_TOPK = 16

ROW_TILE = 512
MIX_CH = 256
LRU_ROWS = 128
CONV_ROWS = 128
NORM_ROWS = 128
NORM_BLOCKS = 4
SCORE_TOK = 512
PEER_TOK = 512
PEER_EXP = 1024
W_ROWS = 8
MXU_COUNT = 2
MXU_TILE = 256
MM_ROWS = 16
POP_LAG_UNITS = 28


def _params(semantics, vmem_mib):
    return pltpu.CompilerParams(dimension_semantics=semantics,
                                vmem_limit_bytes=vmem_mib * MIB)


def _rms(x, g):
    return x * lax.rsqrt(jnp.mean(x * x, axis=-1, keepdims=True) + EPS) * g


def _gelu_tanh(x):
    c = 0.7978845608028654
    return x * (0.5 * (1.0 + jnp.tanh(c * (x + 0.044715 * (x * x * x)))))


def _sigmoid(x):
    return 1.0 / (1.0 + jnp.exp(-x))


def _log_sigmoid(x):
    return -(jnp.maximum(-x, 0.0) + jnp.log(1.0 + jnp.exp(-jnp.abs(x))))


def _transpose_cast_kernel(x_ref, o_ref):
    o_ref[...] = x_ref[...].T.astype(o_ref.dtype)


def _transpose_cast(x, dtype, tile=512):
    r, c = x.shape
    return pl.pallas_call(
        _transpose_cast_kernel,
        grid=(r // tile,),
        in_specs=[pl.BlockSpec((tile, c), lambda i: (i, 0))],
        out_specs=pl.BlockSpec((c, tile), lambda i: (0, i)),
        out_shape=jax.ShapeDtypeStruct((c, r), dtype),
        compiler_params=_params(("parallel",), 24),
        name="transpose_cast",
    )(x)


def _cast_kernel(x_ref, o_ref):
    o_ref[...] = x_ref[...].astype(o_ref.dtype)


def _cast(x, dtype, tile=512):
    r, c = x.shape
    return pl.pallas_call(
        _cast_kernel,
        grid=(r // tile,),
        in_specs=[pl.BlockSpec((tile, c), lambda i: (i, 0))],
        out_specs=pl.BlockSpec((tile, c), lambda i: (i, 0)),
        out_shape=jax.ShapeDtypeStruct((r, c), dtype),
        compiler_params=_params(("parallel",), 24),
        name="cast",
    )(x)


def _inproj_kernel(x_ref, g_ref, w_ref, z_ref):
    n = _rms(x_ref[...], g_ref[...])
    z_ref[...] = jnp.dot(n.astype(BF16), w_ref[...], preferred_element_type=F32)


def _inproj(x2, g, w_bf):
    t, d = x2.shape
    e = w_bf.shape[1]
    return pl.pallas_call(
        _inproj_kernel,
        grid=(t // ROW_TILE,),
        in_specs=[pl.BlockSpec((ROW_TILE, d), lambda i: (i, 0)),
                  pl.BlockSpec((1, d), lambda i: (0, 0)),
                  pl.BlockSpec((d, e), lambda i: (0, 0))],
        out_specs=pl.BlockSpec((ROW_TILE, e), lambda i: (i, 0)),
        out_shape=jax.ShapeDtypeStruct((t, e), F32),
        compiler_params=_params(("parallel",), 40),
        name="inproj",
    )(x2, g, w_bf)


def _scan_group(a_ref, u_ref, r0, carry, row, reverse):
    a = a_ref[pl.ds(r0, SUBLANES), :]
    u = u_ref[pl.ds(r0, SUBLANES), :]
    for d in (1, 2, 4):
        if reverse:
            shift, keep = SUBLANES - d, row < SUBLANES - d
        else:
            shift, keep = d, row >= d
        a_nb = pltpu.roll(a, shift, 0)
        u_nb = pltpu.roll(u, shift, 0)
        u = u + a * jnp.where(keep, u_nb, 0.0)
        a = a * jnp.where(keep, a_nb, 1.0)
    h = u + a * carry
    u_ref[pl.ds(r0, SUBLANES), :] = h
    edge = h[0:1] if reverse else h[SUBLANES - 1:SUBLANES]
    return jnp.broadcast_to(edge, h.shape)


def _scans_in_place(af_ref, uf_ref, ab_ref, ub_ref, seq, ch):
    groups = seq // SUBLANES
    row = lax.broadcasted_iota(jnp.int32, (SUBLANES, ch), 0)

    def body(g, carries):
        cf, cb = carries
        rf = pl.multiple_of(g * SUBLANES, SUBLANES)
        rb = pl.multiple_of((groups - 1 - g) * SUBLANES, SUBLANES)
        return (_scan_group(af_ref, uf_ref, rf, cf, row, False),
                _scan_group(ab_ref, ub_ref, rb, cb, row, True))

    zero = jnp.zeros((SUBLANES, ch), F32)
    lax.fori_loop(0, groups, body, (zero, zero), unroll=4)


def _lru_kernel(x_ref, gate_ref, cw_ref, cb_ref, wg_ref, bg_ref, lam_ref, y_ref,
                xpad_ref, af_ref, uf_ref, ab_ref, ub_ref, pre_ref, *, seq, ch):
    pad = SUBLANES
    zeros = jnp.zeros((pad, ch), F32)
    xpad_ref[pl.ds(0, pad), :] = zeros
    xpad_ref[pl.ds(seq + pad, pad), :] = zeros
    xpad_ref[pl.ds(pad, seq), :] = x_ref[...]

    cw = cw_ref[...]
    cb = cb_ref[...]
    bg = bg_ref[...]
    log_sig = _log_sigmoid(lam_ref[...])
    lpad = LRU_CONV_WIDTH // 2

    def conv(c, _):
        t0 = pl.multiple_of(c * LRU_ROWS, LRU_ROWS)
        win = xpad_ref[pl.ds(t0, LRU_ROWS + 2 * pad), :]
        xc = jnp.zeros((LRU_ROWS, ch), F32) + cb
        for k in range(LRU_CONV_WIDTH):
            off = pad - lpad + k
            xc = xc + cw[k:k + 1, :] * win[off:off + LRU_ROWS, :]
        af_ref[pl.ds(t0, LRU_ROWS), :] = xc
        return 0

    lax.fori_loop(0, seq // LRU_ROWS, conv, 0)

    pre_ref[...] = jnp.dot(af_ref[...].astype(BF16), wg_ref[...], preferred_element_type=F32)

    def gates(c, _):
        t0 = pl.multiple_of(c * LRU_ROWS, LRU_ROWS)
        xc = af_ref[pl.ds(t0, LRU_ROWS), :]
        pre = pre_ref[pl.ds(t0, LRU_ROWS), :] + bg
        for d, (a_ref, u_ref) in enumerate(((af_ref, uf_ref), (ab_ref, ub_ref))):
            r = _sigmoid(pre[:, (2 * d) * ch:(2 * d + 1) * ch])
            i = _sigmoid(pre[:, (2 * d + 1) * ch:(2 * d + 2) * ch])
            a = jnp.exp(LRU_C * r * log_sig[d:d + 1, :])
            a_ref[pl.ds(t0, LRU_ROWS), :] = a
            u_ref[pl.ds(t0, LRU_ROWS), :] = jnp.sqrt(1.0 - a * a) * (i * xc)
        return 0

    lax.fori_loop(0, seq // LRU_ROWS, gates, 0)

    _scans_in_place(af_ref, uf_ref, ab_ref, ub_ref, seq, ch)

    def finish(c, _):
        t0 = pl.multiple_of(c * LRU_ROWS, LRU_ROWS)
        h = uf_ref[pl.ds(t0, LRU_ROWS), :] + ub_ref[pl.ds(t0, LRU_ROWS), :]
        y_ref[pl.ds(t0, LRU_ROWS), :] = h * _gelu_tanh(gate_ref[pl.ds(t0, LRU_ROWS), :])
        return 0

    lax.fori_loop(0, seq // LRU_ROWS, finish, 0)


def _lru_branch(z3, conv_w, conv_b, wg, bg, lam, d_lru):
    b, s, _ = z3.shape
    ch = MIX_CH
    nc = d_lru // ch
    kern = functools.partial(_lru_kernel, seq=s, ch=ch)
    return pl.pallas_call(
        kern,
        grid=(b, nc),
        in_specs=[pl.BlockSpec((None, s, ch), lambda i, c: (i, 0, c)),
                  pl.BlockSpec((None, s, ch), lambda i, c: (i, 0, nc + c)),
                  pl.BlockSpec((LRU_CONV_WIDTH, ch), lambda i, c: (0, c)),
                  pl.BlockSpec((1, ch), lambda i, c: (0, c)),
                  pl.BlockSpec((None, ch, 4 * ch), lambda i, c: (c, 0, 0)),
                  pl.BlockSpec((None, 1, 4 * ch), lambda i, c: (c, 0, 0)),
                  pl.BlockSpec((2, ch), lambda i, c: (0, c))],
        out_specs=pl.BlockSpec((None, s, ch), lambda i, c: (i, 0, c)),
        out_shape=jax.ShapeDtypeStruct((b, s, d_lru), F32),
        scratch_shapes=[pltpu.VMEM((s + 2 * SUBLANES, ch), F32)]
        + [pltpu.VMEM((s, ch), F32)] * 4 + [pltpu.VMEM((s, 4 * ch), F32)],
        compiler_params=_params(("parallel", "parallel"), 48),
        name="lru_branch",
    )(z3, z3, conv_w, conv_b, wg, bg, lam)


def _conf_kernel(a_ref, b_ref, cw_ref, cb_ref, ng_ref, nb_ref, avg_ref, y_ref,
                 gpad_ref, shift_ref, *, seq, ch):
    pad = 2 * SUBLANES
    half = CONF_KERNEL // 2
    zeros = jnp.zeros((pad, ch), F32)
    gpad_ref[pl.ds(0, pad), :] = zeros
    gpad_ref[pl.ds(seq + pad, pad), :] = zeros
    gpad_ref[pl.ds(pad, seq), :] = a_ref[...] * _sigmoid(b_ref[...])

    avg = avg_ref[...]
    span = CONV_ROWS + pad + SUBLANES

    def group_mean(v):
        hi = v.astype(BF16)
        r1 = v - hi.astype(F32)
        mid = r1.astype(BF16)
        lo = (r1 - mid.astype(F32)).astype(BF16)
        return (jnp.dot(hi, avg, preferred_element_type=F32)
                + jnp.dot(mid, avg, preferred_element_type=F32)
                + jnp.dot(lo, avg, preferred_element_type=F32))

    def conv_chunk(c, _):
        t0 = pl.multiple_of(c * CONV_ROWS, CONV_ROWS)
        for lc in range(ch // LANES):
            cols = slice(lc * LANES, (lc + 1) * LANES)
            cw = cw_ref[:, cols]
            win = gpad_ref[pl.ds(t0, CONV_ROWS + 2 * pad), cols]
            acc = jnp.zeros((CONV_ROWS, LANES), F32) + cb_ref[:, cols]
            for b in range(SUBLANES):
                shift_ref[lc, b] = win[b:b + span, :]
            for b in range(SUBLANES):
                for a in range(span // SUBLANES - CONV_ROWS // SUBLANES + 1):
                    k = SUBLANES * a + b - (pad - half)
                    if 0 <= k < CONF_KERNEL:
                        rows = slice(SUBLANES * a, SUBLANES * a + CONV_ROWS)
                        acc = acc + cw[k:k + 1, :] * shift_ref[lc, b, rows, :]
            y_ref[pl.ds(t0, CONV_ROWS), cols] = acc
        return 0

    lax.fori_loop(0, seq // CONV_ROWS, conv_chunk, 0)

    def norm_chunk(c, _):
        blocks = []
        for r in range(NORM_BLOCKS):
            t0 = pl.multiple_of((c * NORM_BLOCKS + r) * NORM_ROWS, NORM_ROWS)
            for lc in range(ch // LANES):
                blocks.append((pl.ds(t0, NORM_ROWS), slice(lc * LANES, (lc + 1) * LANES)))
        xs = [y_ref[rows, cols] for rows, cols in blocks]
        means = [group_mean(x) for x in xs]
        devs = [x - m for x, m in zip(xs, means)]
        variances = [group_mean(dev * dev) for dev in devs]
        for (rows, cols), dev, var in zip(blocks, devs, variances):
            y = dev * lax.rsqrt(var + EPS) * ng_ref[:, cols] + nb_ref[:, cols]
            y_ref[rows, cols] = y * _sigmoid(y)
        return 0

    lax.fori_loop(0, seq // (NORM_ROWS * NORM_BLOCKS), norm_chunk, 0)


def _conf_branch(z3, conv_w, conv_b, norm_g, norm_b, avg, d_lru, d_conv):
    b, s, _ = z3.shape
    ch = MIX_CH
    nc = d_conv // ch
    base = 2 * d_lru // ch
    kern = functools.partial(_conf_kernel, seq=s, ch=ch)
    return pl.pallas_call(
        kern,
        grid=(b, nc),
        in_specs=[pl.BlockSpec((None, s, ch), lambda i, c: (i, 0, base + c)),
                  pl.BlockSpec((None, s, ch), lambda i, c: (i, 0, base + nc + c)),
                  pl.BlockSpec((CONF_KERNEL, ch), lambda i, c: (0, c)),
                  pl.BlockSpec((1, ch), lambda i, c: (0, c)),
                  pl.BlockSpec((1, ch), lambda i, c: (0, c)),
                  pl.BlockSpec((1, ch), lambda i, c: (0, c)),
                  pl.BlockSpec((LANES, LANES), lambda i, c: (0, 0))],
        out_specs=pl.BlockSpec((None, s, ch), lambda i, c: (i, 0, c)),
        out_shape=jax.ShapeDtypeStruct((b, s, d_conv), F32),
        scratch_shapes=[pltpu.VMEM((s + 4 * SUBLANES, ch), F32),
                        pltpu.VMEM((ch // LANES, SUBLANES, CONV_ROWS + 3 * SUBLANES, LANES), F32)],
        compiler_params=_params(("parallel", "parallel"), 40),
        name="conf_branch",
    )(z3, z3, conv_w, conv_b, norm_g, norm_b, avg)


def _outproj_kernel(yl_ref, yc_ref, x_ref, bl_ref, bc_ref, wl_ref, wc_ref, fg_ref,
                    h_ref, ntb_ref):
    yl = _rms(yl_ref[...], bl_ref[...]).astype(BF16)
    yc = _rms(yc_ref[...], bc_ref[...]).astype(BF16)
    h = (x_ref[...]
         + jnp.dot(yl, wl_ref[...], preferred_element_type=F32)
         + jnp.dot(yc, wc_ref[...], preferred_element_type=F32))
    h_ref[...] = h
    ntb_ref[...] = _rms(h, fg_ref[...]).T.astype(BF16)


def _outproj(yl, yc, x2, beta_l, beta_c, wl_bf, wc_bf, ffn_g):
    t, d = x2.shape
    dl = yl.shape[1]
    dc = yc.shape[1]
    row = lambda i: (i, 0)
    fixed = lambda i: (0, 0)
    return pl.pallas_call(
        _outproj_kernel,
        grid=(t // ROW_TILE,),
        in_specs=[pl.BlockSpec((ROW_TILE, dl), row),
                  pl.BlockSpec((ROW_TILE, dc), row),
                  pl.BlockSpec((ROW_TILE, d), row),
                  pl.BlockSpec((1, dl), fixed),
                  pl.BlockSpec((1, dc), fixed),
                  pl.BlockSpec((dl, d), fixed),
                  pl.BlockSpec((dc, d), fixed),
                  pl.BlockSpec((1, d), fixed)],
        out_specs=[pl.BlockSpec((ROW_TILE, d), row),
                   pl.BlockSpec((d, ROW_TILE), lambda i: (0, i))],
        out_shape=[jax.ShapeDtypeStruct((t, d), F32),
                   jax.ShapeDtypeStruct((d, t), BF16)],
        compiler_params=_params(("parallel",), 40),
        name="outproj",
    )(yl, yc, x2, beta_l, beta_c, wl_bf, wc_bf, ffn_g)


def _sort_network(n):
    pairs = []
    p = 1
    while p < n:
        k = p
        while k >= 1:
            for j in range(k % p, n - k, 2 * k):
                for i in range(min(k, n - j - k)):
                    if (i + j) // (2 * p) == (i + j + k) // (2 * p):
                        pairs.append((i + j, i + j + k))
            k //= 2
        p *= 2
    return pairs


def _pruned_network(n_pow2, n_live, n_out):
    pairs = [(i, j) for i, j in _sort_network(n_pow2) if j < n_live]
    needed = set(range(n_out))
    kept = []
    for i, j in reversed(pairs):
        if i in needed or j in needed:
            kept.append((i, j))
            needed.update((i, j))
    return kept[::-1]


def _apply_network(vals, pairs):
    vals = list(vals)
    for i, j in pairs:
        hi = jnp.maximum(vals[i], vals[j])
        lo = jnp.minimum(vals[i], vals[j])
        vals[i], vals[j] = hi, lo
    return vals


def _top16_over_keys(s):
    k = PEER_TOPK
    blocks = [s[SUBLANES * v:SUBLANES * (v + 1), :] for v in range(PEER_N_KEYS // SUBLANES)]
    top = _apply_network(blocks, _sort_network(len(blocks)))
    for d in (1, 2, 4):
        top = [jnp.maximum(top[i], pltpu.roll(top[k - 1 - i], d, 0)) for i in range(k)]
        stride = k // 2
        while stride >= 1:
            pairs = [(i, i + stride) for i in range(k) if not i & stride]
            top = _apply_network(top, pairs)
            stride //= 2
    return top


def _staircase(k):
    return [(a, b) for a in range(k) for b in range(k) if (a + 1) * (b + 1) <= k]


def _k_largest(cands, k):
    n_pow2 = 1
    while n_pow2 < len(cands):
        n_pow2 *= 2
    return _apply_network(cands, _pruned_network(n_pow2, len(cands), k))[:k]


def _score_kernel(nt_ref, wq_ref, keys_ref, g1_ref, e2_ref, thr_ref,
                  q_ref, s_ref, top_ref, z_ref, *, tok):
    k = PEER_TOPK
    chunks = [(c, slice(c * LANES, (c + 1) * LANES)) for c in range(tok // LANES)]

    q_ref[...] = jnp.dot(wq_ref[...], nt_ref[...], preferred_element_type=F32)

    def per_half(hp, _):
        h = hp // 2
        p = hp % 2
        r0 = pl.multiple_of(hp * PEER_N_KEYS, PEER_N_KEYS)
        s = jnp.dot(keys_ref[hp], q_ref[pl.ds(r0, PEER_N_KEYS), :], precision=HIGHEST,
                    preferred_element_type=F32)
        s_ref[p, h] = s
        for c, cols in chunks:
            top = _top16_over_keys(s[:, cols])
            for i in range(k):
                top_ref[p, i, c, pl.ds(h, 1), :] = top[i][0:1, :]
        return 0

    lax.fori_loop(0, 2 * PEER_HEADS, per_half, 0, unroll=2)

    for c, cols in chunks:
        first = [top_ref[0, i, c] for i in range(k)]
        second = [top_ref[1, i, c] for i in range(k)]
        best = _k_largest([first[a] + second[b] for a, b in _staircase(k)], k)
        z = jnp.zeros_like(best[0])
        for v in best:
            z = z + jnp.exp(v - best[0])
        z_ref[c] = z
        inv_z = 1.0 / z
        g1_top = [jnp.exp(first[a] - first[0]) * inv_z for a in range(k)]
        e2_top = [jnp.exp(second[b] - second[0]) for b in range(k)]
        thr_ref[:, cols] = _k_largest([g1_top[a] * e2_top[b] for a, b in _staircase(k)], k)[k - 1]

    def per_head(h, _):
        for c, cols in chunks:
            inv_z = 1.0 / z_ref[c, pl.ds(h, 1), :]
            m1 = top_ref[0, 0, c, pl.ds(h, 1), :]
            m2 = top_ref[1, 0, c, pl.ds(h, 1), :]
            g1_ref[h, :, cols] = jnp.exp(s_ref[0, h, :, cols] - m1) * inv_z
            e2_ref[h, :, cols] = jnp.exp(s_ref[1, h, :, cols] - m2)
        return 0

    lax.fori_loop(0, PEER_HEADS, per_head, 0)


def _peer_scores(nt_bf, wq_t_bf, keys):
    d, t = nt_bf.shape
    tok = SCORE_TOK
    kern = functools.partial(_score_kernel, tok=tok)
    shape = (PEER_HEADS, PEER_N_KEYS, t)
    big_spec = pl.BlockSpec((PEER_HEADS, PEER_N_KEYS, tok), lambda i: (0, 0, i))
    return pl.pallas_call(
        kern,
        grid=(t // tok,),
        in_specs=[pl.BlockSpec((d, tok), lambda i: (0, i)),
                  pl.BlockSpec(wq_t_bf.shape, lambda i: (0, 0)),
                  pl.BlockSpec(keys.shape, lambda i: (0, 0, 0))],
        out_specs=[big_spec, big_spec,
                   pl.BlockSpec((PEER_HEADS, tok), lambda i: (0, i))],
        out_shape=[jax.ShapeDtypeStruct(shape, F32), jax.ShapeDtypeStruct(shape, F32),
                   jax.ShapeDtypeStruct((PEER_HEADS, t), F32)],
        scratch_shapes=[pltpu.VMEM((wq_t_bf.shape[0], tok), F32),
                        pltpu.VMEM((2, PEER_HEADS, PEER_N_KEYS, tok), F32),
                        pltpu.VMEM((2, PEER_TOPK, tok // LANES, PEER_HEADS, LANES), F32),
                        pltpu.VMEM((tok // LANES, PEER_HEADS, LANES), F32)],
        compiler_params=_params(("parallel",), 48),
        name="peer_scores",
    )(nt_bf, wq_t_bf, keys)


def _gelu_times(x, w):
    k0 = -2.0 * 0.7978845608028654 * 1.4426950408889634
    k1 = k0 * 0.044715
    e = jnp.exp2(x * (k0 + k1 * (x * x)))
    return (x * w) / (1.0 + e)


def _peer_kernel(nt_ref, u_ref, vt_ref, g1_ref, e2_ref, thr_ref, out_ref,
                 act0_ref, act1_ref, a0_ref, a1_ref, *, tok, n_exp, k_steps):
    g = pl.program_id(0)
    rows_per_step = n_exp // PEER_N_KEYS

    @pl.when(g == 0)
    def _():
        act1_ref[...] = jnp.zeros_like(act1_ref)
        a0_ref[...] = jnp.zeros_like(a0_ref)

    @pl.when(jnp.logical_or(g < 2, lax.rem(jnp.maximum(g - 2, 0), k_steps) == 0))
    def _():
        out_ref[...] = jnp.zeros_like(out_ref)

    kb = lax.rem(jnp.maximum(g - 1, 0), k_steps)
    i0 = pl.multiple_of(kb * rows_per_step, rows_per_step)

    def body(act_w, act_r, a_w, a_r):
        d = out_ref.shape[0]
        assert tok == MXU_COUNT * MXU_TILE and d % MXU_TILE == 0 and n_exp % MXU_TILE == 0
        k_tiles = {0: d // MXU_TILE, 1: n_exp // MXU_TILE}
        m_slices = {0: n_exp // MXU_TILE, 1: d // MXU_TILE}
        chunks = MXU_TILE // MM_ROWS
        acc_entries = MXU_TILE // 4

        def rhs_tile(mm, k, q):
            src = nt_ref if mm == 0 else a_r
            return src[k * MXU_TILE:(k + 1) * MXU_TILE, q * MXU_TILE:(q + 1) * MXU_TILE]

        def unit(mm, s, k, c):
            lhs_ref = u_ref if mm == 0 else vt_ref
            base = (2 * mm + s % 2) * acc_entries
            r0 = s * MXU_TILE + c * MM_ROWS
            lhs = lhs_ref[r0:r0 + MM_ROWS, k * MXU_TILE:(k + 1) * MXU_TILE]
            reg = k % 2
            for q in range(MXU_COUNT):
                if c == 0 and k == 0:
                    pltpu.matmul_push_rhs(rhs_tile(mm, 0, q), staging_register=reg, mxu_index=q)
                pltpu.matmul_acc_lhs(base + c * (MM_ROWS // 4), lhs, mxu_index=q,
                                     load_staged_rhs=reg if c == 0 else None)
                if c == 0 and k + 1 < k_tiles[mm]:
                    pltpu.matmul_push_rhs(rhs_tile(mm, k + 1, q), staging_register=1 - reg,
                                          mxu_index=q)

        def drain(mm, s):
            base = (2 * mm + s % 2) * acc_entries
            rows = slice(s * MXU_TILE, (s + 1) * MXU_TILE)
            for q in range(MXU_COUNT):
                cols = slice(q * MXU_TILE, (q + 1) * MXU_TILE)
                res = pltpu.matmul_pop(base, (MXU_TILE, MXU_TILE), F32, q)
                if mm == 0:
                    act_w[rows, cols] = res
                else:
                    out_ref[rows, cols] += res

        units = [(mm, s, k, c) for s in range(max(m_slices.values())) for mm in (0, 1)
                 if s < m_slices[mm] for k in range(k_tiles[mm]) for c in range(chunks)]
        pending = []

        def issue(idx):
            if idx < len(units):
                mm, s, k, c = units[idx]
                unit(mm, s, k, c)
                if k == k_tiles[mm] - 1 and c == chunks - 1:
                    pending.append((idx + POP_LAG_UNITS, mm, s))
            while pending and pending[0][0] <= idx:
                _, mm, s = pending.pop(0)
                drain(mm, s)

        slot = 0
        for ii in range(rows_per_step):
            for lc in range(tok // LANES):
                cols = slice(lc * LANES, (lc + 1) * LANES)
                g1b = [jnp.broadcast_to(
                    g1_ref[h, pl.ds(i0, rows_per_step), cols][ii:ii + 1, :], (W_ROWS, LANES))
                    for h in range(PEER_HEADS)]
                thrb = [jnp.broadcast_to(thr_ref[h:h + 1, cols], (W_ROWS, LANES))
                        for h in range(PEER_HEADS)]
                for jt in range(PEER_N_KEYS // W_ROWS):
                    keys = slice(jt * W_ROWS, (jt + 1) * W_ROWS)
                    rows = slice(ii * PEER_N_KEYS + jt * W_ROWS,
                                 ii * PEER_N_KEYS + (jt + 1) * W_ROWS)
                    parts = []
                    for h in range(PEER_HEADS):
                        p = e2_ref[h, keys, cols] * g1b[h]
                        parts.append(jnp.where(p >= thrb[h], p, 0.0))
                    while len(parts) > 1:
                        parts = [parts[i] + parts[i + 1] for i in range(0, len(parts), 2)]
                    a_w[rows, cols] = _gelu_times(act_r[rows, cols], parts[0]).astype(BF16)
                    issue(slot)
                    slot += 1
        while slot < len(units) or pending:
            issue(slot)
            slot += 1

    @pl.when(lax.rem(g, 2) == 0)
    def _():
        body(act0_ref, act1_ref, a1_ref, a0_ref)

    @pl.when(lax.rem(g, 2) == 1)
    def _():
        body(act1_ref, act0_ref, a0_ref, a1_ref)


def _peer_dense(nt_bf, u_bf, vt_bf, g1, e2, thr):
    d, t = nt_bf.shape
    n_experts = u_bf.shape[0]
    tok, n_exp = PEER_TOK, PEER_EXP
    n_tok = t // tok
    k_steps = n_experts // n_exp
    kern = functools.partial(_peer_kernel, tok=tok, n_exp=n_exp, k_steps=k_steps)

    def tok_tile(lag):
        return lambda g: jnp.clip((g - lag) // k_steps, 0, n_tok - 1)

    def exp_tile(lag):
        return lambda g: jnp.maximum(g - lag, 0) % k_steps

    big_spec = pl.BlockSpec((PEER_HEADS, PEER_N_KEYS, tok), lambda g: (0, 0, tok_tile(1)(g)))
    return pl.pallas_call(
        kern,
        grid=(n_tok * k_steps + 2,),
        in_specs=[pl.BlockSpec((d, tok), lambda g: (0, tok_tile(0)(g))),
                  pl.BlockSpec((n_exp, d), lambda g: (exp_tile(0)(g), 0)),
                  pl.BlockSpec((d, n_exp), lambda g: (0, exp_tile(2)(g))),
                  big_spec, big_spec,
                  pl.BlockSpec((PEER_HEADS, tok), lambda g: (0, tok_tile(1)(g)))],
        out_specs=pl.BlockSpec((d, tok), lambda g: (0, tok_tile(2)(g))),
        out_shape=jax.ShapeDtypeStruct((d, t), F32),
        scratch_shapes=[pltpu.VMEM((n_exp, tok), F32), pltpu.VMEM((n_exp, tok), F32),
                        pltpu.VMEM((n_exp, tok), BF16), pltpu.VMEM((n_exp, tok), BF16)],
        compiler_params=_params(("arbitrary",), 48),
        name="peer_dense",
    )(nt_bf, u_bf, vt_bf, g1, e2, thr)


def _final_kernel(h_ref, pt_ref, g_ref, o_ref):
    o_ref[...] = _rms(h_ref[...] + pt_ref[...].T, g_ref[...])


def _final(h, peer_t, g):
    t, d = h.shape
    return pl.pallas_call(
        _final_kernel,
        grid=(t // ROW_TILE,),
        in_specs=[pl.BlockSpec((ROW_TILE, d), lambda i: (i, 0)),
                  pl.BlockSpec((d, ROW_TILE), lambda i: (0, i)),
                  pl.BlockSpec((1, d), lambda i: (0, 0))],
        out_specs=pl.BlockSpec((ROW_TILE, d), lambda i: (i, 0)),
        out_shape=jax.ShapeDtypeStruct((t, d), F32),
        compiler_params=_params(("parallel",), 24),
        name="final_norm",
    )(h, peer_t, g)


def _block_diag_chunks(w, ch):
    heads, hd, _ = w.shape
    per = ch // hd
    w4 = w.reshape(heads // per, per, hd, hd)
    eye = jnp.eye(per, dtype=w.dtype)
    return jnp.einsum("chij,hg->chigj", w4, eye).reshape(heads // per, ch, ch)


def _layer(h2, batch, seq, mix_norm_g, w_in, lru_conv_w, lru_conv_b, lru_w_rg, lru_b_rg,
           lru_w_ig, lru_b_ig, lru_lambda, conf_conv_w, conf_conv_b, conf_norm_g,
           conf_norm_b, beta_lru, beta_conv, w_out, ffn_norm_g, peer_w_q, peer_sub_keys,
           peer_u, peer_v):
    t, d = h2.shape
    d_lru = lru_conv_w.shape[1]
    d_conv = conf_conv_w.shape[1]
    row = lambda v: v.reshape(1, -1)

    z = _inproj(h2, row(mix_norm_g), w_in.astype(BF16))
    z3 = z.reshape(batch, seq, z.shape[1])

    ch = MIX_CH
    nc = d_lru // ch
    wg = jnp.concatenate([_block_diag_chunks(lru_w_rg[0], ch), _block_diag_chunks(lru_w_ig[0], ch),
                          _block_diag_chunks(lru_w_rg[1], ch), _block_diag_chunks(lru_w_ig[1], ch)],
                         axis=-1)
    bg = jnp.concatenate([lru_b_rg[0].reshape(nc, 1, ch), lru_b_ig[0].reshape(nc, 1, ch),
                          lru_b_rg[1].reshape(nc, 1, ch), lru_b_ig[1].reshape(nc, 1, ch)],
                         axis=-1)
    y_lru = _lru_branch(z3, lru_conv_w, row(lru_conv_b), wg.astype(BF16), bg, lru_lambda, d_lru)

    gdim = d_conv // CONV_GROUPS
    grp = jnp.arange(LANES) // gdim
    avg = ((grp[:, None] == grp[None, :]).astype(F32) / gdim).astype(BF16)
    y_conv = _conf_branch(z3, conf_conv_w, row(conf_conv_b), row(conf_norm_g),
                          row(conf_norm_b), avg, d_lru, d_conv)

    w_out_bf = w_out.astype(BF16)
    h2, nt_bf = _outproj(y_lru.reshape(t, d_lru), y_conv.reshape(t, d_conv), h2,
                             row(beta_lru), row(beta_conv), w_out_bf[:d_lru], w_out_bf[d_lru:],
                             row(ffn_norm_g))

    keys = peer_sub_keys.reshape(2 * PEER_HEADS, PEER_N_KEYS, -1)
    g1, e2, thr = _peer_scores(nt_bf, peer_w_q.T.astype(BF16), keys)
    peer_t = _peer_dense(nt_bf, peer_u, _transpose_cast(peer_v, BF16), g1, e2, thr)
    return h2, peer_t


def kernel(x, mix_norm_g, w_in, lru_conv_w, lru_conv_b, lru_w_rg, lru_b_rg, lru_w_ig, lru_b_ig, lru_lambda, conf_conv_w, conf_conv_b, conf_norm_g, conf_norm_b, beta_lru, beta_conv, w_out, ffn_norm_g, peer_w_q, peer_sub_keys, peer_u, peer_v, final_norm_g):
    batch, seq, d = x.shape
    depth = w_in.shape[0]
    h2 = x.reshape(batch * seq, d)
    peer_t = None
    for l in range(depth):
        if peer_t is not None:
            h2 = h2 + peer_t.T
        h2, peer_t = _layer(
            h2, batch, seq, mix_norm_g[l], w_in[l], lru_conv_w[l], lru_conv_b[l], lru_w_rg[l],
            lru_b_rg[l], lru_w_ig[l], lru_b_ig[l], lru_lambda[l], conf_conv_w[l], conf_conv_b[l],
            conf_norm_g[l], conf_norm_b[l], beta_lru[l], beta_conv[l], w_out[l], ffn_norm_g[l],
            peer_w_q[l], peer_sub_keys[l], peer_u[l], peer_v[l])
    out = _final(h2, peer_t, final_norm_g.reshape(1, -1))
    return out.reshape(batch, seq, d)
```

```python
import functools

import jax
import jax.numpy as jnp
from jax import lax
from jax.experimental import pallas as pl
from jax.experimental.pallas import tpu as pltpu

F32 = jnp.float32
BF16 = jnp.bfloat16
HIGHEST = lax.Precision.HIGHEST

SUBLANES = 8
LANES = 128
MIB = 1024 * 1024

EPS = 1e-6
LRU_C = 8.0
LRU_CONV_WIDTH = 4
CONV_GROUPS = 8
CONF_KERNEL = 31
PEER_HEADS = 8
PEER_N_KEYS = 128
PEER_TOPK = 16

ROW_TILE = 512
MIX_CH = 256
LRU_ROWS = 128
CONV_ROWS = 128
NORM_ROWS = 128
NORM_BLOCKS = 4
SCORE_TOK = 512
PEER_TOK = 512
PEER_EXP = 1024
W_ROWS = 8
MXU_COUNT = 2
MXU_TILE = 256
MM_ROWS = 16
POP_LAG_UNITS = 28


def _params(semantics, vmem_mib):
    return pltpu.CompilerParams(dimension_semantics=semantics,
                                vmem_limit_bytes=vmem_mib * MIB)


def _rms(x, g):
    return x * lax.rsqrt(jnp.mean(x * x, axis=-1, keepdims=True) + EPS) * g


def _gelu_tanh(x):
    c = 0.7978845608028654
    return x * (0.5 * (1.0 + jnp.tanh(c * (x + 0.044715 * (x * x * x)))))


def _sigmoid(x):
    return 1.0 / (1.0 + jnp.exp(-x))


def _log_sigmoid(x):
    return -(jnp.maximum(-x, 0.0) + jnp.log(1.0 + jnp.exp(-jnp.abs(x))))


def _transpose_kernel(x_ref, o_ref):
    o_ref[...] = x_ref[...].T


def _transpose(x):
    r, c = x.shape
    return pl.pallas_call(
        _transpose_kernel,
        grid=(r // ROW_TILE,),
        in_specs=[pl.BlockSpec((ROW_TILE, c), lambda i: (i, 0))],
        out_specs=pl.BlockSpec((c, ROW_TILE), lambda i: (0, i)),
        out_shape=jax.ShapeDtypeStruct((c, r), x.dtype),
        compiler_params=_params(("parallel",), 24),
        name="transpose",
    )(x)


def _inproj_kernel(x_ref, g_ref, w_ref, z_ref):
    n = _rms(x_ref[...], g_ref[...])
    z_ref[...] = jnp.dot(n.astype(BF16), w_ref[...], preferred_element_type=F32)


def _inproj(x2, g, w_bf):
    t, d = x2.shape
    e = w_bf.shape[1]
    return pl.pallas_call(
        _inproj_kernel,
        grid=(t // ROW_TILE,),
        in_specs=[pl.BlockSpec((ROW_TILE, d), lambda i: (i, 0)),
                  pl.BlockSpec((1, d), lambda i: (0, 0)),
                  pl.BlockSpec((d, e), lambda i: (0, 0))],
        out_specs=pl.BlockSpec((ROW_TILE, e), lambda i: (i, 0)),
        out_shape=jax.ShapeDtypeStruct((t, e), F32),
        compiler_params=_params(("parallel",), 40),
        name="inproj",
    )(x2, g, w_bf)


def _scan_group(a_ref, u_ref, r0, carry, row, reverse):
    a = a_ref[pl.ds(r0, SUBLANES), :]
    u = u_ref[pl.ds(r0, SUBLANES), :]
    for d in (1, 2, 4):
        if reverse:
            shift, keep = SUBLANES - d, row < SUBLANES - d
        else:
            shift, keep = d, row >= d
        a_nb = pltpu.roll(a, shift, 0)
        u_nb = pltpu.roll(u, shift, 0)
        u = u + a * jnp.where(keep, u_nb, 0.0)
        a = a * jnp.where(keep, a_nb, 1.0)
    h = u + a * carry
    u_ref[pl.ds(r0, SUBLANES), :] = h
    edge = h[0:1] if reverse else h[SUBLANES - 1:SUBLANES]
    return jnp.broadcast_to(edge, h.shape)


def _scans_in_place(af_ref, uf_ref, ab_ref, ub_ref, seq, ch):
    groups = seq // SUBLANES
    row = lax.broadcasted_iota(jnp.int32, (SUBLANES, ch), 0)

    def body(g, carries):
        cf, cb = carries
        rf = pl.multiple_of(g * SUBLANES, SUBLANES)
        rb = pl.multiple_of((groups - 1 - g) * SUBLANES, SUBLANES)
        return (_scan_group(af_ref, uf_ref, rf, cf, row, False),
                _scan_group(ab_ref, ub_ref, rb, cb, row, True))

    zero = jnp.zeros((SUBLANES, ch), F32)
    lax.fori_loop(0, groups, body, (zero, zero), unroll=4)


def _lru_kernel(x_ref, gate_ref, cw_ref, cb_ref, wg_ref, bg_ref, lam_ref, y_ref,
                xpad_ref, af_ref, uf_ref, ab_ref, ub_ref, pre_ref, *, seq, ch):
    pad = SUBLANES
    zeros = jnp.zeros((pad, ch), F32)
    xpad_ref[pl.ds(0, pad), :] = zeros
    xpad_ref[pl.ds(seq + pad, pad), :] = zeros
    xpad_ref[pl.ds(pad, seq), :] = x_ref[...]

    cw = cw_ref[...]
    cb = cb_ref[...]
    bg = bg_ref[...]
    log_sig = _log_sigmoid(lam_ref[...])
    lpad = LRU_CONV_WIDTH // 2

    def conv(c, _):
        t0 = pl.multiple_of(c * LRU_ROWS, LRU_ROWS)
        win = xpad_ref[pl.ds(t0, LRU_ROWS + 2 * pad), :]
        xc = jnp.zeros((LRU_ROWS, ch), F32) + cb
        for k in range(LRU_CONV_WIDTH):
            off = pad - lpad + k
            xc = xc + cw[k:k + 1, :] * win[off:off + LRU_ROWS, :]
        af_ref[pl.ds(t0, LRU_ROWS), :] = xc
        return 0

    lax.fori_loop(0, seq // LRU_ROWS, conv, 0)

    pre_ref[...] = jnp.dot(af_ref[...].astype(BF16), wg_ref[...], preferred_element_type=F32)

    def gates(c, _):
        t0 = pl.multiple_of(c * LRU_ROWS, LRU_ROWS)
        xc = af_ref[pl.ds(t0, LRU_ROWS), :]
        pre = pre_ref[pl.ds(t0, LRU_ROWS), :] + bg
        for d, (a_ref, u_ref) in enumerate(((af_ref, uf_ref), (ab_ref, ub_ref))):
            r = _sigmoid(pre[:, (2 * d) * ch:(2 * d + 1) * ch])
            i = _sigmoid(pre[:, (2 * d + 1) * ch:(2 * d + 2) * ch])
            a = jnp.exp(LRU_C * r * log_sig[d:d + 1, :])
            a_ref[pl.ds(t0, LRU_ROWS), :] = a
            u_ref[pl.ds(t0, LRU_ROWS), :] = jnp.sqrt(1.0 - a * a) * (i * xc)
        return 0

    lax.fori_loop(0, seq // LRU_ROWS, gates, 0)

    _scans_in_place(af_ref, uf_ref, ab_ref, ub_ref, seq, ch)

    def finish(c, _):
        t0 = pl.multiple_of(c * LRU_ROWS, LRU_ROWS)
        h = uf_ref[pl.ds(t0, LRU_ROWS), :] + ub_ref[pl.ds(t0, LRU_ROWS), :]
        y_ref[pl.ds(t0, LRU_ROWS), :] = h * _gelu_tanh(gate_ref[pl.ds(t0, LRU_ROWS), :])
        return 0

    lax.fori_loop(0, seq // LRU_ROWS, finish, 0)


def _lru_branch(z3, conv_w, conv_b, wg, bg, lam, d_lru):
    b, s, _ = z3.shape
    ch = MIX_CH
    nc = d_lru // ch
    kern = functools.partial(_lru_kernel, seq=s, ch=ch)
    return pl.pallas_call(
        kern,
        grid=(b, nc),
        in_specs=[pl.BlockSpec((None, s, ch), lambda i, c: (i, 0, c)),
                  pl.BlockSpec((None, s, ch), lambda i, c: (i, 0, nc + c)),
                  pl.BlockSpec((LRU_CONV_WIDTH, ch), lambda i, c: (0, c)),
                  pl.BlockSpec((1, ch), lambda i, c: (0, c)),
                  pl.BlockSpec((None, ch, 4 * ch), lambda i, c: (c, 0, 0)),
                  pl.BlockSpec((None, 1, 4 * ch), lambda i, c: (c, 0, 0)),
                  pl.BlockSpec((2, ch), lambda i, c: (0, c))],
        out_specs=pl.BlockSpec((None, s, ch), lambda i, c: (i, 0, c)),
        out_shape=jax.ShapeDtypeStruct((b, s, d_lru), F32),
        scratch_shapes=[pltpu.VMEM((s + 2 * SUBLANES, ch), F32)]
        + [pltpu.VMEM((s, ch), F32)] * 4 + [pltpu.VMEM((s, 4 * ch), F32)],
        compiler_params=_params(("parallel", "parallel"), 48),
        name="lru_branch",
    )(z3, z3, conv_w, conv_b, wg, bg, lam)


def _conf_kernel(a_ref, b_ref, cw_ref, cb_ref, ng_ref, nb_ref, avg_ref, y_ref,
                 gpad_ref, shift_ref, *, seq, ch):
    pad = 2 * SUBLANES
    half = CONF_KERNEL // 2
    zeros = jnp.zeros((pad, ch), F32)
    gpad_ref[pl.ds(0, pad), :] = zeros
    gpad_ref[pl.ds(seq + pad, pad), :] = zeros
    gpad_ref[pl.ds(pad, seq), :] = a_ref[...] * _sigmoid(b_ref[...])

    avg = avg_ref[...]
    span = CONV_ROWS + pad + SUBLANES

    def group_mean(v):
        hi = v.astype(BF16)
        r1 = v - hi.astype(F32)
        mid = r1.astype(BF16)
        lo = (r1 - mid.astype(F32)).astype(BF16)
        return (jnp.dot(hi, avg, preferred_element_type=F32)
                + jnp.dot(mid, avg, preferred_element_type=F32)
                + jnp.dot(lo, avg, preferred_element_type=F32))

    def conv_chunk(c, _):
        t0 = pl.multiple_of(c * CONV_ROWS, CONV_ROWS)
        for lc in range(ch // LANES):
            cols = slice(lc * LANES, (lc + 1) * LANES)
            cw = cw_ref[:, cols]
            win = gpad_ref[pl.ds(t0, CONV_ROWS + 2 * pad), cols]
            acc = jnp.zeros((CONV_ROWS, LANES), F32) + cb_ref[:, cols]
            for b in range(SUBLANES):
                shift_ref[lc, b] = win[b:b + span, :]
            for b in range(SUBLANES):
                for a in range(span // SUBLANES - CONV_ROWS // SUBLANES + 1):
                    k = SUBLANES * a + b - (pad - half)
                    if 0 <= k < CONF_KERNEL:
                        rows = slice(SUBLANES * a, SUBLANES * a + CONV_ROWS)
                        acc = acc + cw[k:k + 1, :] * shift_ref[lc, b, rows, :]
            y_ref[pl.ds(t0, CONV_ROWS), cols] = acc
        return 0

    lax.fori_loop(0, seq // CONV_ROWS, conv_chunk, 0)

    def norm_chunk(c, _):
        blocks = []
        for r in range(NORM_BLOCKS):
            t0 = pl.multiple_of((c * NORM_BLOCKS + r) * NORM_ROWS, NORM_ROWS)
            for lc in range(ch // LANES):
                blocks.append((pl.ds(t0, NORM_ROWS), slice(lc * LANES, (lc + 1) * LANES)))
        xs = [y_ref[rows, cols] for rows, cols in blocks]
        means = [group_mean(x) for x in xs]
        devs = [x - m for x, m in zip(xs, means)]
        variances = [group_mean(dev * dev) for dev in devs]
        for (rows, cols), dev, var in zip(blocks, devs, variances):
            y = dev * lax.rsqrt(var + EPS) * ng_ref[:, cols] + nb_ref[:, cols]
            y_ref[rows, cols] = y * _sigmoid(y)
        return 0

    lax.fori_loop(0, seq // (NORM_ROWS * NORM_BLOCKS), norm_chunk, 0)


def _conf_branch(z3, conv_w, conv_b, norm_g, norm_b, avg, d_lru, d_conv):
    b, s, _ = z3.shape
    ch = MIX_CH
    nc = d_conv // ch
    base = 2 * d_lru // ch
    kern = functools.partial(_conf_kernel, seq=s, ch=ch)
    return pl.pallas_call(
        kern,
        grid=(b, nc),
        in_specs=[pl.BlockSpec((None, s, ch), lambda i, c: (i, 0, base + c)),
                  pl.BlockSpec((None, s, ch), lambda i, c: (i, 0, base + nc + c)),
                  pl.BlockSpec((CONF_KERNEL, ch), lambda i, c: (0, c)),
                  pl.BlockSpec((1, ch), lambda i, c: (0, c)),
                  pl.BlockSpec((1, ch), lambda i, c: (0, c)),
                  pl.BlockSpec((1, ch), lambda i, c: (0, c)),
                  pl.BlockSpec((LANES, LANES), lambda i, c: (0, 0))],
        out_specs=pl.BlockSpec((None, s, ch), lambda i, c: (i, 0, c)),
        out_shape=jax.ShapeDtypeStruct((b, s, d_conv), F32),
        scratch_shapes=[pltpu.VMEM((s + 4 * SUBLANES, ch), F32),
                        pltpu.VMEM((ch // LANES, SUBLANES, CONV_ROWS + 3 * SUBLANES, LANES), F32)],
        compiler_params=_params(("parallel", "parallel"), 40),
        name="conf_branch",
    )(z3, z3, conv_w, conv_b, norm_g, norm_b, avg)


def _outproj_kernel(yl_ref, yc_ref, x_ref, bl_ref, bc_ref, wl_ref, wc_ref, fg_ref,
                    h_ref, ntb_ref):
    yl = _rms(yl_ref[...], bl_ref[...]).astype(BF16)
    yc = _rms(yc_ref[...], bc_ref[...]).astype(BF16)
    h = (x_ref[...]
         + jnp.dot(yl, wl_ref[...], preferred_element_type=F32)
         + jnp.dot(yc, wc_ref[...], preferred_element_type=F32))
    h_ref[...] = h
    ntb_ref[...] = _rms(h, fg_ref[...]).T.astype(BF16)


def _outproj(yl, yc, x2, beta_l, beta_c, wl_bf, wc_bf, ffn_g):
    t, d = x2.shape
    dl = yl.shape[1]
    dc = yc.shape[1]
    row = lambda i: (i, 0)
    fixed = lambda i: (0, 0)
    return pl.pallas_call(
        _outproj_kernel,
        grid=(t // ROW_TILE,),
        in_specs=[pl.BlockSpec((ROW_TILE, dl), row),
                  pl.BlockSpec((ROW_TILE, dc), row),
                  pl.BlockSpec((ROW_TILE, d), row),
                  pl.BlockSpec((1, dl), fixed),
                  pl.BlockSpec((1, dc), fixed),
                  pl.BlockSpec((dl, d), fixed),
                  pl.BlockSpec((dc, d), fixed),
                  pl.BlockSpec((1, d), fixed)],
        out_specs=[pl.BlockSpec((ROW_TILE, d), row),
                   pl.BlockSpec((d, ROW_TILE), lambda i: (0, i))],
        out_shape=[jax.ShapeDtypeStruct((t, d), F32),
                   jax.ShapeDtypeStruct((d, t), BF16)],
        compiler_params=_params(("parallel",), 40),
        name="outproj",
    )(yl, yc, x2, beta_l, beta_c, wl_bf, wc_bf, ffn_g)


def _sort_network(n):
    pairs = []
    p = 1
    while p < n:
        k = p
        while k >= 1:
            for j in range(k % p, n - k, 2 * k):
                for i in range(min(k, n - j - k)):
                    if (i + j) // (2 * p) == (i + j + k) // (2 * p):
                        pairs.append((i + j, i + j + k))
            k //= 2
        p *= 2
    return pairs


def _pruned_network(n_pow2, n_live, n_out):
    pairs = [(i, j) for i, j in _sort_network(n_pow2) if j < n_live]
    needed = set(range(n_out))
    kept = []
    for i, j in reversed(pairs):
        if i in needed or j in needed:
            kept.append((i, j))
            needed.update((i, j))
    return kept[::-1]


def _apply_network(vals, pairs):
    vals = list(vals)
    for i, j in pairs:
        hi = jnp.maximum(vals[i], vals[j])
        lo = jnp.minimum(vals[i], vals[j])
        vals[i], vals[j] = hi, lo
    return vals


def _top16_over_keys(s):
    k = PEER_TOPK
    blocks = [s[SUBLANES * v:SUBLANES * (v + 1), :] for v in range(PEER_N_KEYS // SUBLANES)]
    top = _apply_network(blocks, _sort_network(len(blocks)))
    for d in (1, 2, 4):
        top = [jnp.maximum(top[i], pltpu.roll(top[k - 1 - i], d, 0)) for i in range(k)]
        stride = k // 2
        while stride >= 1:
            pairs = [(i, i + stride) for i in range(k) if not i & stride]
            top = _apply_network(top, pairs)
            stride //= 2
    return top


def _staircase(k):
    return [(a, b) for a in range(k) for b in range(k) if (a + 1) * (b + 1) <= k]


def _k_largest(cands, k):
    n_pow2 = 1
    while n_pow2 < len(cands):
        n_pow2 *= 2
    return _apply_network(cands, _pruned_network(n_pow2, len(cands), k))[:k]


def _score_kernel(nt_ref, wq_ref, keys_ref, g1_ref, e2_ref, thr_ref,
                  q_ref, s_ref, top_ref, z_ref, *, tok):
    k = PEER_TOPK
    chunks = [(c, slice(c * LANES, (c + 1) * LANES)) for c in range(tok // LANES)]

    q_ref[...] = jnp.dot(wq_ref[...], nt_ref[...], preferred_element_type=F32)

    def per_half(hp, _):
        h = hp // 2
        p = hp % 2
        r0 = pl.multiple_of(hp * PEER_N_KEYS, PEER_N_KEYS)
        s = jnp.dot(keys_ref[hp], q_ref[pl.ds(r0, PEER_N_KEYS), :], precision=HIGHEST,
                    preferred_element_type=F32)
        s_ref[p, h] = s
        for c, cols in chunks:
            top = _top16_over_keys(s[:, cols])
            for i in range(k):
                top_ref[p, i, c, pl.ds(h, 1), :] = top[i][0:1, :]
        return 0

    lax.fori_loop(0, 2 * PEER_HEADS, per_half, 0, unroll=2)

    for c, cols in chunks:
        first = [top_ref[0, i, c] for i in range(k)]
        second = [top_ref[1, i, c] for i in range(k)]
        best = _k_largest([first[a] + second[b] for a, b in _staircase(k)], k)
        z = jnp.zeros_like(best[0])
        for v in best:
            z = z + jnp.exp(v - best[0])
        z_ref[c] = z
        inv_z = 1.0 / z
        g1_top = [jnp.exp(first[a] - first[0]) * inv_z for a in range(k)]
        e2_top = [jnp.exp(second[b] - second[0]) for b in range(k)]
        thr_ref[:, cols] = _k_largest([g1_top[a] * e2_top[b] for a, b in _staircase(k)], k)[k - 1]

    def per_head(h, _):
        for c, cols in chunks:
            inv_z = 1.0 / z_ref[c, pl.ds(h, 1), :]
            m1 = top_ref[0, 0, c, pl.ds(h, 1), :]
            m2 = top_ref[1, 0, c, pl.ds(h, 1), :]
            g1_ref[h, :, cols] = jnp.exp(s_ref[0, h, :, cols] - m1) * inv_z
            e2_ref[h, :, cols] = jnp.exp(s_ref[1, h, :, cols] - m2)
        return 0

    lax.fori_loop(0, PEER_HEADS, per_head, 0)


def _peer_scores(nt_bf, wq_t_bf, keys):
    d, t = nt_bf.shape
    tok = SCORE_TOK
    kern = functools.partial(_score_kernel, tok=tok)
    shape = (PEER_HEADS, PEER_N_KEYS, t)
    big_spec = pl.BlockSpec((PEER_HEADS, PEER_N_KEYS, tok), lambda i: (0, 0, i))
    return pl.pallas_call(
        kern,
        grid=(t // tok,),
        in_specs=[pl.BlockSpec((d, tok), lambda i: (0, i)),
                  pl.BlockSpec(wq_t_bf.shape, lambda i: (0, 0)),
                  pl.BlockSpec(keys.shape, lambda i: (0, 0, 0))],
        out_specs=[big_spec, big_spec,
                   pl.BlockSpec((PEER_HEADS, tok), lambda i: (0, i))],
        out_shape=[jax.ShapeDtypeStruct(shape, F32), jax.ShapeDtypeStruct(shape, F32),
                   jax.ShapeDtypeStruct((PEER_HEADS, t), F32)],
        scratch_shapes=[pltpu.VMEM((wq_t_bf.shape[0], tok), F32),
                        pltpu.VMEM((2, PEER_HEADS, PEER_N_KEYS, tok), F32),
                        pltpu.VMEM((2, PEER_TOPK, tok // LANES, PEER_HEADS, LANES), F32),
                        pltpu.VMEM((tok // LANES, PEER_HEADS, LANES), F32)],
        compiler_params=_params(("parallel",), 48),
        name="peer_scores",
    )(nt_bf, wq_t_bf, keys)


def _gelu_times(x, w):
    k0 = -2.0 * 0.7978845608028654 * 1.4426950408889634
    k1 = k0 * 0.044715
    e = jnp.exp2(x * (k0 + k1 * (x * x)))
    return (x * w) / (1.0 + e)


def _peer_kernel(nt_ref, u_ref, vt_ref, g1_ref, e2_ref, thr_ref, out_ref,
                 act0_ref, act1_ref, a0_ref, a1_ref, *, tok, n_exp, k_steps):
    g = pl.program_id(0)
    rows_per_step = n_exp // PEER_N_KEYS

    @pl.when(g == 0)
    def _():
        act1_ref[...] = jnp.zeros_like(act1_ref)
        a0_ref[...] = jnp.zeros_like(a0_ref)

    @pl.when(jnp.logical_or(g < 2, lax.rem(jnp.maximum(g - 2, 0), k_steps) == 0))
    def _():
        out_ref[...] = jnp.zeros_like(out_ref)

    kb = lax.rem(jnp.maximum(g - 1, 0), k_steps)
    i0 = pl.multiple_of(kb * rows_per_step, rows_per_step)

    def body(act_w, act_r, a_w, a_r):
        d = out_ref.shape[0]
        assert tok == MXU_COUNT * MXU_TILE and d % MXU_TILE == 0 and n_exp % MXU_TILE == 0
        k_tiles = {0: d // MXU_TILE, 1: n_exp // MXU_TILE}
        m_slices = {0: n_exp // MXU_TILE, 1: d // MXU_TILE}
        chunks = MXU_TILE // MM_ROWS
        acc_entries = MXU_TILE // 4

        def rhs_tile(mm, k, q):
            src = nt_ref if mm == 0 else a_r
            return src[k * MXU_TILE:(k + 1) * MXU_TILE, q * MXU_TILE:(q + 1) * MXU_TILE]

        def unit(mm, s, k, c):
            lhs_ref = u_ref if mm == 0 else vt_ref
            base = (2 * mm + s % 2) * acc_entries
            r0 = s * MXU_TILE + c * MM_ROWS
            lhs = lhs_ref[r0:r0 + MM_ROWS, k * MXU_TILE:(k + 1) * MXU_TILE]
            reg = k % 2
            for q in range(MXU_COUNT):
                if c == 0 and k == 0:
                    pltpu.matmul_push_rhs(rhs_tile(mm, 0, q), staging_register=reg, mxu_index=q)
                pltpu.matmul_acc_lhs(base + c * (MM_ROWS // 4), lhs, mxu_index=q,
                                     load_staged_rhs=reg if c == 0 else None)
                if c == 0 and k + 1 < k_tiles[mm]:
                    pltpu.matmul_push_rhs(rhs_tile(mm, k + 1, q), staging_register=1 - reg,
                                          mxu_index=q)

        def drain(mm, s):
            base = (2 * mm + s % 2) * acc_entries
            rows = slice(s * MXU_TILE, (s + 1) * MXU_TILE)
            for q in range(MXU_COUNT):
                cols = slice(q * MXU_TILE, (q + 1) * MXU_TILE)
                res = pltpu.matmul_pop(base, (MXU_TILE, MXU_TILE), F32, q)
                if mm == 0:
                    act_w[rows, cols] = res
                else:
                    out_ref[rows, cols] += res

        units = [(mm, s, k, c) for s in range(max(m_slices.values())) for mm in (0, 1)
                 if s < m_slices[mm] for k in range(k_tiles[mm]) for c in range(chunks)]
        pending = []

        def issue(idx):
            if idx < len(units):
                mm, s, k, c = units[idx]
                unit(mm, s, k, c)
                if k == k_tiles[mm] - 1 and c == chunks - 1:
                    pending.append((idx + POP_LAG_UNITS, mm, s))
            while pending and pending[0][0] <= idx:
                _, mm, s = pending.pop(0)
                drain(mm, s)

        slot = 0
        for ii in range(rows_per_step):
            for lc in range(tok // LANES):
                cols = slice(lc * LANES, (lc + 1) * LANES)
                g1b = [jnp.broadcast_to(
                    g1_ref[h, pl.ds(i0, rows_per_step), cols][ii:ii + 1, :], (W_ROWS, LANES))
                    for h in range(PEER_HEADS)]
                thrb = [jnp.broadcast_to(thr_ref[h:h + 1, cols], (W_ROWS, LANES))
                        for h in range(PEER_HEADS)]
                for jt in range(PEER_N_KEYS // W_ROWS):
                    keys = slice(jt * W_ROWS, (jt + 1) * W_ROWS)
                    rows = slice(ii * PEER_N_KEYS + jt * W_ROWS,
                                 ii * PEER_N_KEYS + (jt + 1) * W_ROWS)
                    parts = []
                    for h in range(PEER_HEADS):
                        p = e2_ref[h, keys, cols] * g1b[h]
                        parts.append(jnp.where(p >= thrb[h], p, 0.0))
                    while len(parts) > 1:
                        parts = [parts[i] + parts[i + 1] for i in range(0, len(parts), 2)]
                    a_w[rows, cols] = _gelu_times(act_r[rows, cols], parts[0]).astype(BF16)
                    issue(slot)
                    slot += 1
        while slot < len(units) or pending:
            issue(slot)
            slot += 1

    @pl.when(lax.rem(g, 2) == 0)
    def _():
        body(act0_ref, act1_ref, a1_ref, a0_ref)

    @pl.when(lax.rem(g, 2) == 1)
    def _():
        body(act1_ref, act0_ref, a0_ref, a1_ref)


def _peer_dense(nt_bf, u, vt, g1, e2, thr):
    d, t = nt_bf.shape
    n_experts = u.shape[0]
    tok, n_exp = PEER_TOK, PEER_EXP
    n_tok = t // tok
    k_steps = n_experts // n_exp
    kern = functools.partial(_peer_kernel, tok=tok, n_exp=n_exp, k_steps=k_steps)

    def tok_tile(lag):
        return lambda g: jnp.clip((g - lag) // k_steps, 0, n_tok - 1)

    def exp_tile(lag):
        return lambda g: jnp.maximum(g - lag, 0) % k_steps

    big_spec = pl.BlockSpec((PEER_HEADS, PEER_N_KEYS, tok), lambda g: (0, 0, tok_tile(1)(g)))
    return pl.pallas_call(
        kern,
        grid=(n_tok * k_steps + 2,),
        in_specs=[pl.BlockSpec((d, tok), lambda g: (0, tok_tile(0)(g))),
                  pl.BlockSpec((n_exp, d), lambda g: (exp_tile(0)(g), 0)),
                  pl.BlockSpec((d, n_exp), lambda g: (0, exp_tile(2)(g))),
                  big_spec, big_spec,
                  pl.BlockSpec((PEER_HEADS, tok), lambda g: (0, tok_tile(1)(g)))],
        out_specs=pl.BlockSpec((d, tok), lambda g: (0, tok_tile(2)(g))),
        out_shape=jax.ShapeDtypeStruct((d, t), F32),
        scratch_shapes=[pltpu.VMEM((n_exp, tok), F32), pltpu.VMEM((n_exp, tok), F32),
                        pltpu.VMEM((n_exp, tok), BF16), pltpu.VMEM((n_exp, tok), BF16)],
        compiler_params=_params(("arbitrary",), 48),
        name="peer_dense",
    )(nt_bf, u, vt, g1, e2, thr)


def _final_kernel(h_ref, pt_ref, g_ref, o_ref):
    o_ref[...] = _rms(h_ref[...] + pt_ref[...].T, g_ref[...])


def _final(h, peer_t, g):
    t, d = h.shape
    return pl.pallas_call(
        _final_kernel,
        grid=(t // ROW_TILE,),
        in_specs=[pl.BlockSpec((ROW_TILE, d), lambda i: (i, 0)),
                  pl.BlockSpec((d, ROW_TILE), lambda i: (0, i)),
                  pl.BlockSpec((1, d), lambda i: (0, 0))],
        out_specs=pl.BlockSpec((ROW_TILE, d), lambda i: (i, 0)),
        out_shape=jax.ShapeDtypeStruct((t, d), F32),
        compiler_params=_params(("parallel",), 24),
        name="final_norm",
    )(h, peer_t, g)


def _block_diag_chunks(w, ch):
    heads, hd, _ = w.shape
    per = ch // hd
    w4 = w.reshape(heads // per, per, hd, hd)
    eye = jnp.eye(per, dtype=w.dtype)
    return jnp.einsum("chij,hg->chigj", w4, eye).reshape(heads // per, ch, ch)


def _layer(h2, batch, seq, mix_norm_g, w_in, lru_conv_w, lru_conv_b, lru_w_rg, lru_b_rg,
           lru_w_ig, lru_b_ig, lru_lambda, conf_conv_w, conf_conv_b, conf_norm_g,
           conf_norm_b, beta_lru, beta_conv, w_out, ffn_norm_g, peer_w_q, peer_sub_keys,
           peer_u, peer_v):
    t, d = h2.shape
    d_lru = lru_conv_w.shape[1]
    d_conv = conf_conv_w.shape[1]
    row = lambda v: v.reshape(1, -1)

    z = _inproj(h2, row(mix_norm_g), w_in.astype(BF16))
    z3 = z.reshape(batch, seq, z.shape[1])

    ch = MIX_CH
    nc = d_lru // ch
    wg = jnp.concatenate([_block_diag_chunks(lru_w_rg[0], ch), _block_diag_chunks(lru_w_ig[0], ch),
                          _block_diag_chunks(lru_w_rg[1], ch), _block_diag_chunks(lru_w_ig[1], ch)],
                         axis=-1)
    bg = jnp.concatenate([lru_b_rg[0].reshape(nc, 1, ch), lru_b_ig[0].reshape(nc, 1, ch),
                          lru_b_rg[1].reshape(nc, 1, ch), lru_b_ig[1].reshape(nc, 1, ch)],
                         axis=-1)
    y_lru = _lru_branch(z3, lru_conv_w, row(lru_conv_b), wg.astype(BF16), bg, lru_lambda, d_lru)

    gdim = d_conv // CONV_GROUPS
    grp = jnp.arange(LANES) // gdim
    avg = ((grp[:, None] == grp[None, :]).astype(F32) / gdim).astype(BF16)
    y_conv = _conf_branch(z3, conf_conv_w, row(conf_conv_b), row(conf_norm_g),
                          row(conf_norm_b), avg, d_lru, d_conv)

    w_out_bf = w_out.astype(BF16)
    h2, nt_bf = _outproj(y_lru.reshape(t, d_lru), y_conv.reshape(t, d_conv), h2,
                             row(beta_lru), row(beta_conv), w_out_bf[:d_lru], w_out_bf[d_lru:],
                             row(ffn_norm_g))

    keys = peer_sub_keys.reshape(2 * PEER_HEADS, PEER_N_KEYS, -1)
    g1, e2, thr = _peer_scores(nt_bf, peer_w_q.T.astype(BF16), keys)
    peer_t = _peer_dense(nt_bf, peer_u, _transpose(peer_v), g1, e2, thr)
    return h2, peer_t


def kernel(x, mix_norm_g, w_in, lru_conv_w, lru_conv_b, lru_w_rg, lru_b_rg, lru_w_ig, lru_b_ig, lru_lambda, conf_conv_w, conf_conv_b, conf_norm_g, conf_norm_b, beta_lru, beta_conv, w_out, ffn_norm_g, peer_w_q, peer_sub_keys, peer_u, peer_v, final_norm_g):
    batch, seq, d = x.shape
    assert w_in.shape[0] == 1, "one layer: the final norm is fused with the layer's PEER residual"
    h2, peer_t = _layer(
        x.reshape(batch * seq, d), batch, seq, mix_norm_g[0], w_in[0], lru_conv_w[0],
        lru_conv_b[0], lru_w_rg[0], lru_b_rg[0], lru_w_ig[0], lru_b_ig[0], lru_lambda[0],
        conf_conv_w[0], conf_conv_b[0], conf_norm_g[0], conf_norm_b[0], beta_lru[0], beta_conv[0],
        w_out[0], ffn_norm_g[0], peer_w_q[0], peer_sub_keys[0], peer_u[0], peer_v[0])
    out = _final(h2, peer_t, final_norm_g.reshape(1, -1))
    return out.reshape(batch, seq, d)
```

```python
import functools

import jax
import jax.numpy as jnp
from jax import lax
from jax.experimental import pallas as pl
from jax.experimental.pallas import tpu as pltpu

F32 = jnp.float32
BF16 = jnp.bfloat16

SUBLANES = 8
LANES = 128
MIB = 1024 * 1024

EPS = 1e-6
LRU_C = 8.0
LRU_CONV_WIDTH = 4
CONV_GROUPS = 8
CONF_KERNEL = 31
PEER_HEADS = 8
PEER_N_KEYS = 128
PEER_TOPK = 16

ROW_TILE = 512
MIX_CH = 256
LRU_ROWS = 128
CONV_ROWS = 128
NORM_ROWS = 128
NORM_BLOCKS = 4
SCORE_TOK = 512
PEER_TOK = 512
PEER_EXP = 1024
W_ROWS = 8
MXU_COUNT = 2
MXU_TILE = 256
MM_ROWS = 16
POP_LAG_UNITS = 28


def _params(semantics, vmem_mib):
    return pltpu.CompilerParams(dimension_semantics=semantics,
                                vmem_limit_bytes=vmem_mib * MIB)


def _rms(x, g):
    return x * lax.rsqrt(jnp.mean(x * x, axis=-1, keepdims=True) + EPS) * g


def _gelu_tanh(x):
    c = 0.7978845608028654
    return x * (0.5 * (1.0 + jnp.tanh(c * (x + 0.044715 * (x * x * x)))))


def _sigmoid(x):
    return 1.0 / (1.0 + jnp.exp(-x))


def _log_sigmoid(x):
    return -(jnp.maximum(-x, 0.0) + jnp.log(1.0 + jnp.exp(-jnp.abs(x))))


def _transpose_kernel(x_ref, o_ref):
    o_ref[...] = x_ref[...].T


def _transpose(x):
    r, c = x.shape
    return pl.pallas_call(
        _transpose_kernel,
        grid=(r // ROW_TILE,),
        in_specs=[pl.BlockSpec((ROW_TILE, c), lambda i: (i, 0))],
        out_specs=pl.BlockSpec((c, ROW_TILE), lambda i: (0, i)),
        out_shape=jax.ShapeDtypeStruct((c, r), x.dtype),
        compiler_params=_params(("parallel",), 24),
        name="transpose",
    )(x)


def _inproj_kernel(x_ref, g_ref, w_ref, z_ref):
    n = _rms(x_ref[...], g_ref[...])
    z_ref[...] = jnp.dot(n.astype(BF16), w_ref[...], preferred_element_type=F32)


def _inproj(x2, g, w_bf):
    t, d = x2.shape
    e = w_bf.shape[1]
    return pl.pallas_call(
        _inproj_kernel,
        grid=(t // ROW_TILE,),
        in_specs=[pl.BlockSpec((ROW_TILE, d), lambda i: (i, 0)),
                  pl.BlockSpec((1, d), lambda i: (0, 0)),
                  pl.BlockSpec((d, e), lambda i: (0, 0))],
        out_specs=pl.BlockSpec((ROW_TILE, e), lambda i: (i, 0)),
        out_shape=jax.ShapeDtypeStruct((t, e), F32),
        compiler_params=_params(("parallel",), 40),
        name="inproj",
    )(x2, g, w_bf)


def _scan_group(a_ref, u_ref, r0, carry, row, reverse):
    a = a_ref[pl.ds(r0, SUBLANES), :]
    u = u_ref[pl.ds(r0, SUBLANES), :]
    for d in (1, 2, 4):
        if reverse:
            shift, keep = SUBLANES - d, row < SUBLANES - d
        else:
            shift, keep = d, row >= d
        a_nb = pltpu.roll(a, shift, 0)
        u_nb = pltpu.roll(u, shift, 0)
        u = u + a * jnp.where(keep, u_nb, 0.0)
        a = a * jnp.where(keep, a_nb, 1.0)
    h = u + a * carry
    u_ref[pl.ds(r0, SUBLANES), :] = h
    edge = h[0:1] if reverse else h[SUBLANES - 1:SUBLANES]
    return jnp.broadcast_to(edge, h.shape)


def _scans_in_place(af_ref, uf_ref, ab_ref, ub_ref, seq, ch):
    groups = seq // SUBLANES
    row = lax.broadcasted_iota(jnp.int32, (SUBLANES, ch), 0)

    def body(g, carries):
        cf, cb = carries
        rf = pl.multiple_of(g * SUBLANES, SUBLANES)
        rb = pl.multiple_of((groups - 1 - g) * SUBLANES, SUBLANES)
        return (_scan_group(af_ref, uf_ref, rf, cf, row, False),
                _scan_group(ab_ref, ub_ref, rb, cb, row, True))

    zero = jnp.zeros((SUBLANES, ch), F32)
    lax.fori_loop(0, groups, body, (zero, zero), unroll=4)


def _lru_kernel(x_ref, gate_ref, cw_ref, cb_ref, wg_ref, bg_ref, lam_ref, y_ref,
                xpad_ref, af_ref, uf_ref, ab_ref, ub_ref, pre_ref, *, seq, ch):
    pad = SUBLANES
    zeros = jnp.zeros((pad, ch), F32)
    xpad_ref[pl.ds(0, pad), :] = zeros
    xpad_ref[pl.ds(seq + pad, pad), :] = zeros
    xpad_ref[pl.ds(pad, seq), :] = x_ref[...]

    cw = cw_ref[...]
    cb = cb_ref[...]
    bg = bg_ref[...]
    log_sig = _log_sigmoid(lam_ref[...])
    lpad = LRU_CONV_WIDTH // 2

    def conv(c, _):
        t0 = pl.multiple_of(c * LRU_ROWS, LRU_ROWS)
        win = xpad_ref[pl.ds(t0, LRU_ROWS + 2 * pad), :]
        xc = jnp.zeros((LRU_ROWS, ch), F32) + cb
        for k in range(LRU_CONV_WIDTH):
            off = pad - lpad + k
            xc = xc + cw[k:k + 1, :] * win[off:off + LRU_ROWS, :]
        af_ref[pl.ds(t0, LRU_ROWS), :] = xc
        return 0

    lax.fori_loop(0, seq // LRU_ROWS, conv, 0)

    pre_ref[...] = jnp.dot(af_ref[...].astype(BF16), wg_ref[...], preferred_element_type=F32)

    def gates(c, _):
        t0 = pl.multiple_of(c * LRU_ROWS, LRU_ROWS)
        xc = af_ref[pl.ds(t0, LRU_ROWS), :]
        pre = pre_ref[pl.ds(t0, LRU_ROWS), :] + bg
        for d, (a_ref, u_ref) in enumerate(((af_ref, uf_ref), (ab_ref, ub_ref))):
            r = _sigmoid(pre[:, (2 * d) * ch:(2 * d + 1) * ch])
            i = _sigmoid(pre[:, (2 * d + 1) * ch:(2 * d + 2) * ch])
            a = jnp.exp(LRU_C * r * log_sig[d:d + 1, :])
            a_ref[pl.ds(t0, LRU_ROWS), :] = a
            u_ref[pl.ds(t0, LRU_ROWS), :] = jnp.sqrt(1.0 - a * a) * (i * xc)
        return 0

    lax.fori_loop(0, seq // LRU_ROWS, gates, 0)

    _scans_in_place(af_ref, uf_ref, ab_ref, ub_ref, seq, ch)

    def finish(c, _):
        t0 = pl.multiple_of(c * LRU_ROWS, LRU_ROWS)
        h = uf_ref[pl.ds(t0, LRU_ROWS), :] + ub_ref[pl.ds(t0, LRU_ROWS), :]
        y_ref[pl.ds(t0, LRU_ROWS), :] = h * _gelu_tanh(gate_ref[pl.ds(t0, LRU_ROWS), :])
        return 0

    lax.fori_loop(0, seq // LRU_ROWS, finish, 0)


def _lru_branch(z3, conv_w, conv_b, wg, bg, lam, d_lru):
    b, s, _ = z3.shape
    ch = MIX_CH
    nc = d_lru // ch
    kern = functools.partial(_lru_kernel, seq=s, ch=ch)
    return pl.pallas_call(
        kern,
        grid=(b, nc),
        in_specs=[pl.BlockSpec((None, s, ch), lambda i, c: (i, 0, c)),
                  pl.BlockSpec((None, s, ch), lambda i, c: (i, 0, nc + c)),
                  pl.BlockSpec((LRU_CONV_WIDTH, ch), lambda i, c: (0, c)),
                  pl.BlockSpec((1, ch), lambda i, c: (0, c)),
                  pl.BlockSpec((None, ch, 4 * ch), lambda i, c: (c, 0, 0)),
                  pl.BlockSpec((None, 1, 4 * ch), lambda i, c: (c, 0, 0)),
                  pl.BlockSpec((2, ch), lambda i, c: (0, c))],
        out_specs=pl.BlockSpec((None, s, ch), lambda i, c: (i, 0, c)),
        out_shape=jax.ShapeDtypeStruct((b, s, d_lru), F32),
        scratch_shapes=[pltpu.VMEM((s + 2 * SUBLANES, ch), F32)]
        + [pltpu.VMEM((s, ch), F32)] * 4 + [pltpu.VMEM((s, 4 * ch), F32)],
        compiler_params=_params(("parallel", "parallel"), 48),
        name="lru_branch",
    )(z3, z3, conv_w, conv_b, wg, bg, lam)


def _conf_kernel(a_ref, b_ref, cw_ref, cb_ref, ng_ref, nb_ref, avg_ref, y_ref,
                 gpad_ref, shift_ref, *, seq, ch):
    pad = 2 * SUBLANES
    half = CONF_KERNEL // 2
    zeros = jnp.zeros((pad, ch), F32)
    gpad_ref[pl.ds(0, pad), :] = zeros
    gpad_ref[pl.ds(seq + pad, pad), :] = zeros
    gpad_ref[pl.ds(pad, seq), :] = a_ref[...] * _sigmoid(b_ref[...])

    avg = avg_ref[...]
    span = CONV_ROWS + pad + SUBLANES

    def group_mean(v):
        hi = v.astype(BF16)
        r1 = v - hi.astype(F32)
        mid = r1.astype(BF16)
        lo = (r1 - mid.astype(F32)).astype(BF16)
        return (jnp.dot(hi, avg, preferred_element_type=F32)
                + jnp.dot(mid, avg, preferred_element_type=F32)
                + jnp.dot(lo, avg, preferred_element_type=F32))

    def conv_chunk(c, _):
        t0 = pl.multiple_of(c * CONV_ROWS, CONV_ROWS)
        for lc in range(ch // LANES):
            cols = slice(lc * LANES, (lc + 1) * LANES)
            cw = cw_ref[:, cols]
            win = gpad_ref[pl.ds(t0, CONV_ROWS + 2 * pad), cols]
            acc = jnp.zeros((CONV_ROWS, LANES), F32) + cb_ref[:, cols]
            for b in range(SUBLANES):
                shift_ref[lc, b] = win[b:b + span, :]
            for b in range(SUBLANES):
                for a in range(span // SUBLANES - CONV_ROWS // SUBLANES + 1):
                    k = SUBLANES * a + b - (pad - half)
                    if 0 <= k < CONF_KERNEL:
                        rows = slice(SUBLANES * a, SUBLANES * a + CONV_ROWS)
                        acc = acc + cw[k:k + 1, :] * shift_ref[lc, b, rows, :]
            y_ref[pl.ds(t0, CONV_ROWS), cols] = acc
        return 0

    lax.fori_loop(0, seq // CONV_ROWS, conv_chunk, 0)

    def norm_chunk(c, _):
        blocks = []
        for r in range(NORM_BLOCKS):
            t0 = pl.multiple_of((c * NORM_BLOCKS + r) * NORM_ROWS, NORM_ROWS)
            for lc in range(ch // LANES):
                blocks.append((pl.ds(t0, NORM_ROWS), slice(lc * LANES, (lc + 1) * LANES)))
        xs = [y_ref[rows, cols] for rows, cols in blocks]
        means = [group_mean(x) for x in xs]
        devs = [x - m for x, m in zip(xs, means)]
        variances = [group_mean(dev * dev) for dev in devs]
        for (rows, cols), dev, var in zip(blocks, devs, variances):
            y = dev * lax.rsqrt(var + EPS) * ng_ref[:, cols] + nb_ref[:, cols]
            y_ref[rows, cols] = y * _sigmoid(y)
        return 0

    lax.fori_loop(0, seq // (NORM_ROWS * NORM_BLOCKS), norm_chunk, 0)


def _conf_branch(z3, conv_w, conv_b, norm_g, norm_b, avg, d_lru, d_conv):
    b, s, _ = z3.shape
    ch = MIX_CH
    nc = d_conv // ch
    base = 2 * d_lru // ch
    kern = functools.partial(_conf_kernel, seq=s, ch=ch)
    return pl.pallas_call(
        kern,
        grid=(b, nc),
        in_specs=[pl.BlockSpec((None, s, ch), lambda i, c: (i, 0, base + c)),
                  pl.BlockSpec((None, s, ch), lambda i, c: (i, 0, base + nc + c)),
                  pl.BlockSpec((CONF_KERNEL, ch), lambda i, c: (0, c)),
                  pl.BlockSpec((1, ch), lambda i, c: (0, c)),
                  pl.BlockSpec((1, ch), lambda i, c: (0, c)),
                  pl.BlockSpec((1, ch), lambda i, c: (0, c)),
                  pl.BlockSpec((LANES, LANES), lambda i, c: (0, 0))],
        out_specs=pl.BlockSpec((None, s, ch), lambda i, c: (i, 0, c)),
        out_shape=jax.ShapeDtypeStruct((b, s, d_conv), F32),
        scratch_shapes=[pltpu.VMEM((s + 4 * SUBLANES, ch), F32),
                        pltpu.VMEM((ch // LANES, SUBLANES, CONV_ROWS + 3 * SUBLANES, LANES), F32)],
        compiler_params=_params(("parallel", "parallel"), 40),
        name="conf_branch",
    )(z3, z3, conv_w, conv_b, norm_g, norm_b, avg)


def _outproj_kernel(yl_ref, yc_ref, x_ref, bl_ref, bc_ref, wl_ref, wc_ref, fg_ref,
                    h_ref, ntb_ref):
    yl = _rms(yl_ref[...], bl_ref[...]).astype(BF16)
    yc = _rms(yc_ref[...], bc_ref[...]).astype(BF16)
    h = (x_ref[...]
         + jnp.dot(yl, wl_ref[...], preferred_element_type=F32)
         + jnp.dot(yc, wc_ref[...], preferred_element_type=F32))
    h_ref[...] = h
    ntb_ref[...] = _rms(h, fg_ref[...]).T.astype(BF16)


def _outproj(yl, yc, x2, beta_l, beta_c, wl_bf, wc_bf, ffn_g):
    t, d = x2.shape
    dl = yl.shape[1]
    dc = yc.shape[1]
    row = lambda i: (i, 0)
    fixed = lambda i: (0, 0)
    return pl.pallas_call(
        _outproj_kernel,
        grid=(t // ROW_TILE,),
        in_specs=[pl.BlockSpec((ROW_TILE, dl), row),
                  pl.BlockSpec((ROW_TILE, dc), row),
                  pl.BlockSpec((ROW_TILE, d), row),
                  pl.BlockSpec((1, dl), fixed),
                  pl.BlockSpec((1, dc), fixed),
                  pl.BlockSpec((dl, d), fixed),
                  pl.BlockSpec((dc, d), fixed),
                  pl.BlockSpec((1, d), fixed)],
        out_specs=[pl.BlockSpec((ROW_TILE, d), row),
                   pl.BlockSpec((d, ROW_TILE), lambda i: (0, i))],
        out_shape=[jax.ShapeDtypeStruct((t, d), F32),
                   jax.ShapeDtypeStruct((d, t), BF16)],
        compiler_params=_params(("parallel",), 40),
        name="outproj",
    )(yl, yc, x2, beta_l, beta_c, wl_bf, wc_bf, ffn_g)


def _sort_network(n):
    pairs = []
    p = 1
    while p < n:
        k = p
        while k >= 1:
            for j in range(k % p, n - k, 2 * k):
                for i in range(min(k, n - j - k)):
                    if (i + j) // (2 * p) == (i + j + k) // (2 * p):
                        pairs.append((i + j, i + j + k))
            k //= 2
        p *= 2
    return pairs


def _pruned_network(n_pow2, n_live, n_out):
    pairs = [(i, j) for i, j in _sort_network(n_pow2) if j < n_live]
    needed = set(range(n_out))
    kept = []
    for i, j in reversed(pairs):
        if i in needed or j in needed:
            kept.append((i, j))
            needed.update((i, j))
    return kept[::-1]


def _apply_network(vals, pairs):
    vals = list(vals)
    for i, j in pairs:
        hi = jnp.maximum(vals[i], vals[j])
        lo = jnp.minimum(vals[i], vals[j])
        vals[i], vals[j] = hi, lo
    return vals


def _top16_over_keys(s):
    k = PEER_TOPK
    blocks = [s[SUBLANES * v:SUBLANES * (v + 1), :] for v in range(PEER_N_KEYS // SUBLANES)]
    top = _apply_network(blocks, _sort_network(len(blocks)))
    for d in (1, 2, 4):
        top = [jnp.maximum(top[i], pltpu.roll(top[k - 1 - i], d, 0)) for i in range(k)]
        stride = k // 2
        while stride >= 1:
            pairs = [(i, i + stride) for i in range(k) if not i & stride]
            top = _apply_network(top, pairs)
            stride //= 2
    return top


def _staircase(k):
    return [(a, b) for a in range(k) for b in range(k) if (a + 1) * (b + 1) <= k]


def _k_largest(cands, k):
    n_pow2 = 1
    while n_pow2 < len(cands):
        n_pow2 *= 2
    return _apply_network(cands, _pruned_network(n_pow2, len(cands), k))[:k]


def _score_kernel(nt_ref, wq_ref, keys_ref, g1_ref, e2_ref, thr_ref,
                  q_ref, s_ref, top_ref, z_ref, *, tok):
    k = PEER_TOPK
    chunks = [(c, slice(c * LANES, (c + 1) * LANES)) for c in range(tok // LANES)]

    q_ref[...] = jnp.dot(wq_ref[...], nt_ref[...], preferred_element_type=F32)

    def per_half(hp, _):
        h = hp // 2
        p = hp % 2
        r0 = pl.multiple_of(hp * PEER_N_KEYS, PEER_N_KEYS)
        kf = keys_ref[hp]
        qf = q_ref[pl.ds(r0, PEER_N_KEYS), :]
        kh = kf.astype(BF16)
        qh = qf.astype(BF16)
        kl = (kf - kh.astype(F32)).astype(BF16)
        ql = (qf - qh.astype(F32)).astype(BF16)
        s = (jnp.dot(kh, qh, preferred_element_type=F32)
             + jnp.dot(kh, ql, preferred_element_type=F32)
             + jnp.dot(kl, qh, preferred_element_type=F32))
        s_ref[p, h] = s
        for c, cols in chunks:
            top = _top16_over_keys(s[:, cols])
            for i in range(k):
                top_ref[p, i, c, pl.ds(h, 1), :] = top[i][0:1, :]
        return 0

    lax.fori_loop(0, 2 * PEER_HEADS, per_half, 0, unroll=2)

    for c, cols in chunks:
        first = [top_ref[0, i, c] for i in range(k)]
        second = [top_ref[1, i, c] for i in range(k)]
        best = _k_largest([first[a] + second[b] for a, b in _staircase(k)], k)
        z = jnp.zeros_like(best[0])
        for v in best:
            z = z + jnp.exp(v - best[0])
        z_ref[c] = z
        inv_z = 1.0 / z
        g1_top = [jnp.exp(first[a] - first[0]) * inv_z for a in range(k)]
        e2_top = [jnp.exp(second[b] - second[0]) for b in range(k)]
        thr_ref[:, cols] = _k_largest([g1_top[a] * e2_top[b] for a, b in _staircase(k)], k)[k - 1]

    def per_head(h, _):
        for c, cols in chunks:
            inv_z = 1.0 / z_ref[c, pl.ds(h, 1), :]
            m1 = top_ref[0, 0, c, pl.ds(h, 1), :]
            m2 = top_ref[1, 0, c, pl.ds(h, 1), :]
            g1_ref[h, :, cols] = jnp.exp(s_ref[0, h, :, cols] - m1) * inv_z
            e2_ref[h, :, cols] = jnp.exp(s_ref[1, h, :, cols] - m2)
        return 0

    lax.fori_loop(0, PEER_HEADS, per_head, 0)


def _peer_scores(nt_bf, wq_t_bf, keys):
    d, t = nt_bf.shape
    tok = SCORE_TOK
    kern = functools.partial(_score_kernel, tok=tok)
    shape = (PEER_HEADS, PEER_N_KEYS, t)
    big_spec = pl.BlockSpec((PEER_HEADS, PEER_N_KEYS, tok), lambda i: (0, 0, i))
    return pl.pallas_call(
        kern,
        grid=(t // tok,),
        in_specs=[pl.BlockSpec((d, tok), lambda i: (0, i)),
                  pl.BlockSpec(wq_t_bf.shape, lambda i: (0, 0)),
                  pl.BlockSpec(keys.shape, lambda i: (0, 0, 0))],
        out_specs=[big_spec, big_spec,
                   pl.BlockSpec((PEER_HEADS, tok), lambda i: (0, i))],
        out_shape=[jax.ShapeDtypeStruct(shape, F32), jax.ShapeDtypeStruct(shape, F32),
                   jax.ShapeDtypeStruct((PEER_HEADS, t), F32)],
        scratch_shapes=[pltpu.VMEM((wq_t_bf.shape[0], tok), F32),
                        pltpu.VMEM((2, PEER_HEADS, PEER_N_KEYS, tok), F32),
                        pltpu.VMEM((2, PEER_TOPK, tok // LANES, PEER_HEADS, LANES), F32),
                        pltpu.VMEM((tok // LANES, PEER_HEADS, LANES), F32)],
        compiler_params=_params(("parallel",), 48),
        name="peer_scores",
    )(nt_bf, wq_t_bf, keys)


def _gelu_times(x, w):
    k0 = -2.0 * 0.7978845608028654 * 1.4426950408889634
    k1 = k0 * 0.044715
    e = jnp.exp2(x * (k0 + k1 * (x * x)))
    return (x * w) / (1.0 + e)


def _peer_kernel(nt_ref, u_ref, vt_ref, g1_ref, e2_ref, thr_ref, out_ref,
                 act0_ref, act1_ref, a0_ref, a1_ref, *, tok, n_exp, k_steps):
    g = pl.program_id(0)
    rows_per_step = n_exp // PEER_N_KEYS

    @pl.when(g == 0)
    def _():
        act1_ref[...] = jnp.zeros_like(act1_ref)
        a0_ref[...] = jnp.zeros_like(a0_ref)

    @pl.when(jnp.logical_or(g < 2, lax.rem(jnp.maximum(g - 2, 0), k_steps) == 0))
    def _():
        out_ref[...] = jnp.zeros_like(out_ref)

    kb = lax.rem(jnp.maximum(g - 1, 0), k_steps)
    i0 = pl.multiple_of(kb * rows_per_step, rows_per_step)

    def body(act_w, act_r, a_w, a_r):
        d = out_ref.shape[0]
        assert tok == MXU_COUNT * MXU_TILE and d % MXU_TILE == 0 and n_exp % MXU_TILE == 0
        k_tiles = {0: d // MXU_TILE, 1: n_exp // MXU_TILE}
        m_slices = {0: n_exp // MXU_TILE, 1: d // MXU_TILE}
        chunks = MXU_TILE // MM_ROWS
        acc_entries = MXU_TILE // 4

        def rhs_tile(mm, k, q):
            src = nt_ref if mm == 0 else a_r
            return src[k * MXU_TILE:(k + 1) * MXU_TILE, q * MXU_TILE:(q + 1) * MXU_TILE]

        def unit(mm, s, k, c):
            lhs_ref = u_ref if mm == 0 else vt_ref
            base = (2 * mm + s % 2) * acc_entries
            r0 = s * MXU_TILE + c * MM_ROWS
            lhs = lhs_ref[r0:r0 + MM_ROWS, k * MXU_TILE:(k + 1) * MXU_TILE]
            reg = k % 2
            for q in range(MXU_COUNT):
                if c == 0 and k == 0:
                    pltpu.matmul_push_rhs(rhs_tile(mm, 0, q), staging_register=reg, mxu_index=q)
                pltpu.matmul_acc_lhs(base + c * (MM_ROWS // 4), lhs, mxu_index=q,
                                     load_staged_rhs=reg if c == 0 else None)
                if c == 0 and k + 1 < k_tiles[mm]:
                    pltpu.matmul_push_rhs(rhs_tile(mm, k + 1, q), staging_register=1 - reg,
                                          mxu_index=q)

        def drain(mm, s):
            base = (2 * mm + s % 2) * acc_entries
            rows = slice(s * MXU_TILE, (s + 1) * MXU_TILE)
            for q in range(MXU_COUNT):
                cols = slice(q * MXU_TILE, (q + 1) * MXU_TILE)
                res = pltpu.matmul_pop(base, (MXU_TILE, MXU_TILE), F32, q)
                if mm == 0:
                    act_w[rows, cols] = res
                else:
                    out_ref[rows, cols] += res

        units = [(mm, s, k, c) for s in range(max(m_slices.values())) for mm in (0, 1)
                 if s < m_slices[mm] for k in range(k_tiles[mm]) for c in range(chunks)]
        pending = []

        def issue(idx):
            if idx < len(units):
                mm, s, k, c = units[idx]
                unit(mm, s, k, c)
                if k == k_tiles[mm] - 1 and c == chunks - 1:
                    pending.append((idx + POP_LAG_UNITS, mm, s))
            while pending and pending[0][0] <= idx:
                _, mm, s = pending.pop(0)
                drain(mm, s)

        slot = 0
        for ii in range(rows_per_step):
            for lc in range(tok // LANES):
                cols = slice(lc * LANES, (lc + 1) * LANES)
                g1b = [jnp.broadcast_to(
                    g1_ref[h, pl.ds(i0, rows_per_step), cols][ii:ii + 1, :], (W_ROWS, LANES))
                    for h in range(PEER_HEADS)]
                thrb = [jnp.broadcast_to(thr_ref[h:h + 1, cols], (W_ROWS, LANES))
                        for h in range(PEER_HEADS)]
                for jt in range(PEER_N_KEYS // W_ROWS):
                    keys = slice(jt * W_ROWS, (jt + 1) * W_ROWS)
                    rows = slice(ii * PEER_N_KEYS + jt * W_ROWS,
                                 ii * PEER_N_KEYS + (jt + 1) * W_ROWS)
                    parts = []
                    for h in range(PEER_HEADS):
                        p = e2_ref[h, keys, cols] * g1b[h]
                        parts.append(jnp.where(p >= thrb[h], p, 0.0))
                    while len(parts) > 1:
                        parts = [parts[i] + parts[i + 1] for i in range(0, len(parts), 2)]
                    a_w[rows, cols] = _gelu_times(act_r[rows, cols], parts[0]).astype(BF16)
                    issue(slot)
                    slot += 1
        while slot < len(units) or pending:
            issue(slot)
            slot += 1

    @pl.when(lax.rem(g, 2) == 0)
    def _():
        body(act0_ref, act1_ref, a1_ref, a0_ref)

    @pl.when(lax.rem(g, 2) == 1)
    def _():
        body(act1_ref, act0_ref, a0_ref, a1_ref)


def _peer_dense(nt_bf, u, vt, g1, e2, thr):
    d, t = nt_bf.shape
    n_experts = u.shape[0]
    tok, n_exp = PEER_TOK, PEER_EXP
    n_tok = t // tok
    k_steps = n_experts // n_exp
    kern = functools.partial(_peer_kernel, tok=tok, n_exp=n_exp, k_steps=k_steps)

    def tok_tile(lag):
        return lambda g: jnp.clip((g - lag) // k_steps, 0, n_tok - 1)

    def exp_tile(lag):
        return lambda g: jnp.maximum(g - lag, 0) % k_steps

    big_spec = pl.BlockSpec((PEER_HEADS, PEER_N_KEYS, tok), lambda g: (0, 0, tok_tile(1)(g)))
    return pl.pallas_call(
        kern,
        grid=(n_tok * k_steps + 2,),
        in_specs=[pl.BlockSpec((d, tok), lambda g: (0, tok_tile(0)(g))),
                  pl.BlockSpec((n_exp, d), lambda g: (exp_tile(0)(g), 0)),
                  pl.BlockSpec((d, n_exp), lambda g: (0, exp_tile(2)(g))),
                  big_spec, big_spec,
                  pl.BlockSpec((PEER_HEADS, tok), lambda g: (0, tok_tile(1)(g)))],
        out_specs=pl.BlockSpec((d, tok), lambda g: (0, tok_tile(2)(g))),
        out_shape=jax.ShapeDtypeStruct((d, t), F32),
        scratch_shapes=[pltpu.VMEM((n_exp, tok), F32), pltpu.VMEM((n_exp, tok), F32),
                        pltpu.VMEM((n_exp, tok), BF16), pltpu.VMEM((n_exp, tok), BF16)],
        compiler_params=_params(("arbitrary",), 48),
        name="peer_dense",
    )(nt_bf, u, vt, g1, e2, thr)


def _final_kernel(h_ref, pt_ref, g_ref, o_ref):
    o_ref[...] = _rms(h_ref[...] + pt_ref[...].T, g_ref[...])


def _final(h, peer_t, g):
    t, d = h.shape
    return pl.pallas_call(
        _final_kernel,
        grid=(t // ROW_TILE,),
        in_specs=[pl.BlockSpec((ROW_TILE, d), lambda i: (i, 0)),
                  pl.BlockSpec((d, ROW_TILE), lambda i: (0, i)),
                  pl.BlockSpec((1, d), lambda i: (0, 0))],
        out_specs=pl.BlockSpec((ROW_TILE, d), lambda i: (i, 0)),
        out_shape=jax.ShapeDtypeStruct((t, d), F32),
        compiler_params=_params(("parallel",), 24),
        name="final_norm",
    )(h, peer_t, g)


def _block_diag_chunks(w, ch):
    heads, hd, _ = w.shape
    per = ch // hd
    w4 = w.reshape(heads // per, per, hd, hd)
    eye = jnp.eye(per, dtype=w.dtype)
    return jnp.einsum("chij,hg->chigj", w4, eye).reshape(heads // per, ch, ch)


def _layer(h2, batch, seq, mix_norm_g, w_in, lru_conv_w, lru_conv_b, lru_w_rg, lru_b_rg,
           lru_w_ig, lru_b_ig, lru_lambda, conf_conv_w, conf_conv_b, conf_norm_g,
           conf_norm_b, beta_lru, beta_conv, w_out, ffn_norm_g, peer_w_q, peer_sub_keys,
           peer_u, peer_v):
    t, d = h2.shape
    d_lru = lru_conv_w.shape[1]
    d_conv = conf_conv_w.shape[1]
    row = lambda v: v.reshape(1, -1)

    z = _inproj(h2, row(mix_norm_g), w_in.astype(BF16))
    z3 = z.reshape(batch, seq, z.shape[1])

    ch = MIX_CH
    nc = d_lru // ch
    wg = jnp.concatenate([_block_diag_chunks(lru_w_rg[0], ch), _block_diag_chunks(lru_w_ig[0], ch),
                          _block_diag_chunks(lru_w_rg[1], ch), _block_diag_chunks(lru_w_ig[1], ch)],
                         axis=-1)
    bg = jnp.concatenate([lru_b_rg[0].reshape(nc, 1, ch), lru_b_ig[0].reshape(nc, 1, ch),
                          lru_b_rg[1].reshape(nc, 1, ch), lru_b_ig[1].reshape(nc, 1, ch)],
                         axis=-1)
    y_lru = _lru_branch(z3, lru_conv_w, row(lru_conv_b), wg.astype(BF16), bg, lru_lambda, d_lru)

    gdim = d_conv // CONV_GROUPS
    grp = jnp.arange(LANES) // gdim
    avg = ((grp[:, None] == grp[None, :]).astype(F32) / gdim).astype(BF16)
    y_conv = _conf_branch(z3, conf_conv_w, row(conf_conv_b), row(conf_norm_g),
                          row(conf_norm_b), avg, d_lru, d_conv)

    w_out_bf = w_out.astype(BF16)
    h2, nt_bf = _outproj(y_lru.reshape(t, d_lru), y_conv.reshape(t, d_conv), h2,
                             row(beta_lru), row(beta_conv), w_out_bf[:d_lru], w_out_bf[d_lru:],
                             row(ffn_norm_g))

    keys = peer_sub_keys.reshape(2 * PEER_HEADS, PEER_N_KEYS, -1)
    g1, e2, thr = _peer_scores(nt_bf, peer_w_q.T.astype(BF16), keys)
    peer_t = _peer_dense(nt_bf, peer_u, _transpose(peer_v), g1, e2, thr)
    return h2, peer_t


def kernel(x, mix_norm_g, w_in, lru_conv_w, lru_conv_b, lru_w_rg, lru_b_rg, lru_w_ig, lru_b_ig, lru_lambda, conf_conv_w, conf_conv_b, conf_norm_g, conf_norm_b, beta_lru, beta_conv, w_out, ffn_norm_g, peer_w_q, peer_sub_keys, peer_u, peer_v, final_norm_g):
    batch, seq, d = x.shape
    assert w_in.shape[0] == 1, "one layer: the final norm is fused with the layer's PEER residual"
    h2, peer_t = _layer(
        x.reshape(batch * seq, d), batch, seq, mix_norm_g[0], w_in[0], lru_conv_w[0],
        lru_conv_b[0], lru_w_rg[0], lru_b_rg[0], lru_w_ig[0], lru_b_ig[0], lru_lambda[0],
        conf_conv_w[0], conf_conv_b[0], conf_norm_g[0], conf_norm_b[0], beta_lru[0], beta_conv[0],
        w_out[0], ffn_norm_g[0], peer_w_q[0], peer_sub_keys[0], peer_u[0], peer_v[0])
    out = _final(h2, peer_t, final_norm_g.reshape(1, -1))
    return out.reshape(batch, seq, d)
```

```python
import functools

import jax
import jax.numpy as jnp
from jax import lax
from jax.experimental import pallas as pl
from jax.experimental.pallas import tpu as pltpu

F32 = jnp.float32
BF16 = jnp.bfloat16

SUBLANES = 8
LANES = 128
MIB = 1024 * 1024

EPS = 1e-6
LRU_C = 8.0
LRU_CONV_WIDTH = 4
CONV_GROUPS = 8
CONF_KERNEL = 31
PEER_HEADS = 8
PEER_N_KEYS = 128
PEER_TOPK = 16

ROW_TILE = 1024
MIX_CH = 256
LRU_ROWS = 128
CONV_ROWS = 128
NORM_ROWS = 128
NORM_BLOCKS = 4
SCORE_TOK = 512
PEER_TOK = 512
PEER_EXP = 1024
W_ROWS = 8
MXU_COUNT = 2
MXU_TILE = 256
MM_ROWS = 16
POP_LAG_UNITS = 28


def _params(semantics, vmem_mib):
    return pltpu.CompilerParams(dimension_semantics=semantics,
                                vmem_limit_bytes=vmem_mib * MIB)


def _rms(x, g):
    return x * lax.rsqrt(jnp.mean(x * x, axis=-1, keepdims=True) + EPS) * g


def _gelu_tanh(x):
    c = 0.7978845608028654
    return x * (0.5 * (1.0 + jnp.tanh(c * (x + 0.044715 * (x * x * x)))))


def _sigmoid(x):
    return 1.0 / (1.0 + jnp.exp(-x))


def _log_sigmoid(x):
    return -(jnp.maximum(-x, 0.0) + jnp.log(1.0 + jnp.exp(-jnp.abs(x))))


def _transpose_kernel(x_ref, o_ref):
    o_ref[...] = x_ref[...].T


def _transpose(x):
    r, c = x.shape
    return pl.pallas_call(
        _transpose_kernel,
        grid=(r // ROW_TILE,),
        in_specs=[pl.BlockSpec((ROW_TILE, c), lambda i: (i, 0))],
        out_specs=pl.BlockSpec((c, ROW_TILE), lambda i: (0, i)),
        out_shape=jax.ShapeDtypeStruct((c, r), x.dtype),
        compiler_params=_params(("parallel",), 24),
        name="transpose",
    )(x)


def _inproj_kernel(x_ref, g_ref, w_ref, z_ref):
    n = _rms(x_ref[...], g_ref[...])
    z_ref[...] = jnp.dot(n.astype(BF16), w_ref[...], preferred_element_type=F32)


def _inproj(x2, g, w_bf):
    t, d = x2.shape
    e = w_bf.shape[1]
    return pl.pallas_call(
        _inproj_kernel,
        grid=(t // ROW_TILE,),
        in_specs=[pl.BlockSpec((ROW_TILE, d), lambda i: (i, 0)),
                  pl.BlockSpec((1, d), lambda i: (0, 0)),
                  pl.BlockSpec((d, e), lambda i: (0, 0))],
        out_specs=pl.BlockSpec((ROW_TILE, e), lambda i: (i, 0)),
        out_shape=jax.ShapeDtypeStruct((t, e), F32),
        compiler_params=_params(("parallel",), 40),
        name="inproj",
    )(x2, g, w_bf)


def _scan_group(a_ref, u_ref, r0, carry, row, reverse):
    a = a_ref[pl.ds(r0, SUBLANES), :]
    u = u_ref[pl.ds(r0, SUBLANES), :]
    for d in (1, 2, 4):
        if reverse:
            shift, keep = SUBLANES - d, row < SUBLANES - d
        else:
            shift, keep = d, row >= d
        a_nb = pltpu.roll(a, shift, 0)
        u_nb = pltpu.roll(u, shift, 0)
        u = u + a * jnp.where(keep, u_nb, 0.0)
        a = a * jnp.where(keep, a_nb, 1.0)
    h = u + a * carry
    u_ref[pl.ds(r0, SUBLANES), :] = h
    edge = h[0:1] if reverse else h[SUBLANES - 1:SUBLANES]
    return jnp.broadcast_to(edge, h.shape)


def _scans_in_place(af_ref, uf_ref, ab_ref, ub_ref, seq, ch):
    groups = seq // SUBLANES
    row = lax.broadcasted_iota(jnp.int32, (SUBLANES, ch), 0)

    def body(g, carries):
        cf, cb = carries
        rf = pl.multiple_of(g * SUBLANES, SUBLANES)
        rb = pl.multiple_of((groups - 1 - g) * SUBLANES, SUBLANES)
        return (_scan_group(af_ref, uf_ref, rf, cf, row, False),
                _scan_group(ab_ref, ub_ref, rb, cb, row, True))

    zero = jnp.zeros((SUBLANES, ch), F32)
    lax.fori_loop(0, groups, body, (zero, zero), unroll=4)


def _lru_kernel(x_ref, gate_ref, cw_ref, cb_ref, wg_ref, bg_ref, lam_ref, y_ref,
                xpad_ref, af_ref, uf_ref, ab_ref, ub_ref, pre_ref, *, seq, ch):
    pad = SUBLANES
    zeros = jnp.zeros((pad, ch), F32)
    xpad_ref[pl.ds(0, pad), :] = zeros
    xpad_ref[pl.ds(seq + pad, pad), :] = zeros
    xpad_ref[pl.ds(pad, seq), :] = x_ref[...]

    cw = cw_ref[...]
    cb = cb_ref[...]
    bg = bg_ref[...]
    log_sig = _log_sigmoid(lam_ref[...])
    lpad = LRU_CONV_WIDTH // 2

    def conv(c, _):
        t0 = pl.multiple_of(c * LRU_ROWS, LRU_ROWS)
        win = xpad_ref[pl.ds(t0, LRU_ROWS + 2 * pad), :]
        xc = jnp.zeros((LRU_ROWS, ch), F32) + cb
        for k in range(LRU_CONV_WIDTH):
            off = pad - lpad + k
            xc = xc + cw[k:k + 1, :] * win[off:off + LRU_ROWS, :]
        af_ref[pl.ds(t0, LRU_ROWS), :] = xc
        return 0

    lax.fori_loop(0, seq // LRU_ROWS, conv, 0)

    pre_ref[...] = jnp.dot(af_ref[...].astype(BF16), wg_ref[...], preferred_element_type=F32)

    def gates(c, _):
        t0 = pl.multiple_of(c * LRU_ROWS, LRU_ROWS)
        xc = af_ref[pl.ds(t0, LRU_ROWS), :]
        pre = pre_ref[pl.ds(t0, LRU_ROWS), :] + bg
        for d, (a_ref, u_ref) in enumerate(((af_ref, uf_ref), (ab_ref, ub_ref))):
            r = _sigmoid(pre[:, (2 * d) * ch:(2 * d + 1) * ch])
            i = _sigmoid(pre[:, (2 * d + 1) * ch:(2 * d + 2) * ch])
            a = jnp.exp(LRU_C * r * log_sig[d:d + 1, :])
            a_ref[pl.ds(t0, LRU_ROWS), :] = a
            u_ref[pl.ds(t0, LRU_ROWS), :] = jnp.sqrt(1.0 - a * a) * (i * xc)
        return 0

    lax.fori_loop(0, seq // LRU_ROWS, gates, 0)

    _scans_in_place(af_ref, uf_ref, ab_ref, ub_ref, seq, ch)

    def finish(c, _):
        t0 = pl.multiple_of(c * LRU_ROWS, LRU_ROWS)
        h = uf_ref[pl.ds(t0, LRU_ROWS), :] + ub_ref[pl.ds(t0, LRU_ROWS), :]
        y_ref[pl.ds(t0, LRU_ROWS), :] = h * _gelu_tanh(gate_ref[pl.ds(t0, LRU_ROWS), :])
        return 0

    lax.fori_loop(0, seq // LRU_ROWS, finish, 0)


def _lru_branch(z3, conv_w, conv_b, wg, bg, lam, d_lru):
    b, s, _ = z3.shape
    ch = MIX_CH
    nc = d_lru // ch
    kern = functools.partial(_lru_kernel, seq=s, ch=ch)
    return pl.pallas_call(
        kern,
        grid=(b, nc),
        in_specs=[pl.BlockSpec((None, s, ch), lambda i, c: (i, 0, c)),
                  pl.BlockSpec((None, s, ch), lambda i, c: (i, 0, nc + c)),
                  pl.BlockSpec((LRU_CONV_WIDTH, ch), lambda i, c: (0, c)),
                  pl.BlockSpec((1, ch), lambda i, c: (0, c)),
                  pl.BlockSpec((None, ch, 4 * ch), lambda i, c: (c, 0, 0)),
                  pl.BlockSpec((None, 1, 4 * ch), lambda i, c: (c, 0, 0)),
                  pl.BlockSpec((2, ch), lambda i, c: (0, c))],
        out_specs=pl.BlockSpec((None, s, ch), lambda i, c: (i, 0, c)),
        out_shape=jax.ShapeDtypeStruct((b, s, d_lru), F32),
        scratch_shapes=[pltpu.VMEM((s + 2 * SUBLANES, ch), F32)]
        + [pltpu.VMEM((s, ch), F32)] * 4 + [pltpu.VMEM((s, 4 * ch), F32)],
        compiler_params=_params(("parallel", "parallel"), 48),
        name="lru_branch",
    )(z3, z3, conv_w, conv_b, wg, bg, lam)


def _conf_kernel(a_ref, b_ref, cw_ref, cb_ref, ng_ref, nb_ref, avg_ref, y_ref,
                 gpad_ref, shift_ref, *, seq, ch):
    pad = 2 * SUBLANES
    half = CONF_KERNEL // 2
    zeros = jnp.zeros((pad, ch), F32)
    gpad_ref[pl.ds(0, pad), :] = zeros
    gpad_ref[pl.ds(seq + pad, pad), :] = zeros
    gpad_ref[pl.ds(pad, seq), :] = a_ref[...] * _sigmoid(b_ref[...])

    avg = avg_ref[...]
    span = CONV_ROWS + pad + SUBLANES

    def group_mean(v):
        hi = v.astype(BF16)
        r1 = v - hi.astype(F32)
        mid = r1.astype(BF16)
        lo = (r1 - mid.astype(F32)).astype(BF16)
        return (jnp.dot(hi, avg, preferred_element_type=F32)
                + jnp.dot(mid, avg, preferred_element_type=F32)
                + jnp.dot(lo, avg, preferred_element_type=F32))

    def conv_chunk(c, _):
        t0 = pl.multiple_of(c * CONV_ROWS, CONV_ROWS)
        for lc in range(ch // LANES):
            cols = slice(lc * LANES, (lc + 1) * LANES)
            cw = cw_ref[:, cols]
            win = gpad_ref[pl.ds(t0, CONV_ROWS + 2 * pad), cols]
            acc = jnp.zeros((CONV_ROWS, LANES), F32) + cb_ref[:, cols]
            for b in range(SUBLANES):
                shift_ref[lc, b] = win[b:b + span, :]
            for b in range(SUBLANES):
                for a in range(span // SUBLANES - CONV_ROWS // SUBLANES + 1):
                    k = SUBLANES * a + b - (pad - half)
                    if 0 <= k < CONF_KERNEL:
                        rows = slice(SUBLANES * a, SUBLANES * a + CONV_ROWS)
                        acc = acc + cw[k:k + 1, :] * shift_ref[lc, b, rows, :]
            y_ref[pl.ds(t0, CONV_ROWS), cols] = acc
        return 0

    lax.fori_loop(0, seq // CONV_ROWS, conv_chunk, 0)

    def norm_chunk(c, _):
        blocks = []
        for r in range(NORM_BLOCKS):
            t0 = pl.multiple_of((c * NORM_BLOCKS + r) * NORM_ROWS, NORM_ROWS)
            for lc in range(ch // LANES):
                blocks.append((pl.ds(t0, NORM_ROWS), slice(lc * LANES, (lc + 1) * LANES)))
        xs = [y_ref[rows, cols] for rows, cols in blocks]
        means = [group_mean(x) for x in xs]
        devs = [x - m for x, m in zip(xs, means)]
        variances = [group_mean(dev * dev) for dev in devs]
        for (rows, cols), dev, var in zip(blocks, devs, variances):
            y = dev * lax.rsqrt(var + EPS) * ng_ref[:, cols] + nb_ref[:, cols]
            y_ref[rows, cols] = y * _sigmoid(y)
        return 0

    lax.fori_loop(0, seq // (NORM_ROWS * NORM_BLOCKS), norm_chunk, 0)


def _conf_branch(z3, conv_w, conv_b, norm_g, norm_b, avg, d_lru, d_conv):
    b, s, _ = z3.shape
    ch = MIX_CH
    nc = d_conv // ch
    base = 2 * d_lru // ch
    kern = functools.partial(_conf_kernel, seq=s, ch=ch)
    return pl.pallas_call(
        kern,
        grid=(b, nc),
        in_specs=[pl.BlockSpec((None, s, ch), lambda i, c: (i, 0, base + c)),
                  pl.BlockSpec((None, s, ch), lambda i, c: (i, 0, base + nc + c)),
                  pl.BlockSpec((CONF_KERNEL, ch), lambda i, c: (0, c)),
                  pl.BlockSpec((1, ch), lambda i, c: (0, c)),
                  pl.BlockSpec((1, ch), lambda i, c: (0, c)),
                  pl.BlockSpec((1, ch), lambda i, c: (0, c)),
                  pl.BlockSpec((LANES, LANES), lambda i, c: (0, 0))],
        out_specs=pl.BlockSpec((None, s, ch), lambda i, c: (i, 0, c)),
        out_shape=jax.ShapeDtypeStruct((b, s, d_conv), F32),
        scratch_shapes=[pltpu.VMEM((s + 4 * SUBLANES, ch), F32),
                        pltpu.VMEM((ch // LANES, SUBLANES, CONV_ROWS + 3 * SUBLANES, LANES), F32)],
        compiler_params=_params(("parallel", "parallel"), 40),
        name="conf_branch",
    )(z3, z3, conv_w, conv_b, norm_g, norm_b, avg)


def _outproj_kernel(yl_ref, yc_ref, x_ref, bl_ref, bc_ref, wl_ref, wc_ref, fg_ref,
                    h_ref, ntb_ref):
    yl = _rms(yl_ref[...], bl_ref[...]).astype(BF16)
    yc = _rms(yc_ref[...], bc_ref[...]).astype(BF16)
    h = (x_ref[...]
         + jnp.dot(yl, wl_ref[...], preferred_element_type=F32)
         + jnp.dot(yc, wc_ref[...], preferred_element_type=F32))
    h_ref[...] = h
    ntb_ref[...] = _rms(h, fg_ref[...]).T.astype(BF16)


def _outproj(yl, yc, x2, beta_l, beta_c, wl_bf, wc_bf, ffn_g):
    t, d = x2.shape
    dl = yl.shape[1]
    dc = yc.shape[1]
    row = lambda i: (i, 0)
    fixed = lambda i: (0, 0)
    return pl.pallas_call(
        _outproj_kernel,
        grid=(t // ROW_TILE,),
        in_specs=[pl.BlockSpec((ROW_TILE, dl), row),
                  pl.BlockSpec((ROW_TILE, dc), row),
                  pl.BlockSpec((ROW_TILE, d), row),
                  pl.BlockSpec((1, dl), fixed),
                  pl.BlockSpec((1, dc), fixed),
                  pl.BlockSpec((dl, d), fixed),
                  pl.BlockSpec((dc, d), fixed),
                  pl.BlockSpec((1, d), fixed)],
        out_specs=[pl.BlockSpec((ROW_TILE, d), row),
                   pl.BlockSpec((d, ROW_TILE), lambda i: (0, i))],
        out_shape=[jax.ShapeDtypeStruct((t, d), F32),
                   jax.ShapeDtypeStruct((d, t), BF16)],
        compiler_params=_params(("parallel",), 48),
        name="outproj",
    )(yl, yc, x2, beta_l, beta_c, wl_bf, wc_bf, ffn_g)


def _sort_network(n):
    pairs = []
    p = 1
    while p < n:
        k = p
        while k >= 1:
            for j in range(k % p, n - k, 2 * k):
                for i in range(min(k, n - j - k)):
                    if (i + j) // (2 * p) == (i + j + k) // (2 * p):
                        pairs.append((i + j, i + j + k))
            k //= 2
        p *= 2
    return pairs


def _pruned_network(n_pow2, n_live, n_out):
    pairs = [(i, j) for i, j in _sort_network(n_pow2) if j < n_live]
    needed = set(range(n_out))
    kept = []
    for i, j in reversed(pairs):
        if i in needed or j in needed:
            kept.append((i, j))
            needed.update((i, j))
    return kept[::-1]


def _apply_network(vals, pairs):
    vals = list(vals)
    for i, j in pairs:
        hi = jnp.maximum(vals[i], vals[j])
        lo = jnp.minimum(vals[i], vals[j])
        vals[i], vals[j] = hi, lo
    return vals


def _top16_over_keys(s):
    k = PEER_TOPK
    blocks = [s[SUBLANES * v:SUBLANES * (v + 1), :] for v in range(PEER_N_KEYS // SUBLANES)]
    top = _apply_network(blocks, _sort_network(len(blocks)))
    for d in (1, 2, 4):
        top = [jnp.maximum(top[i], pltpu.roll(top[k - 1 - i], d, 0)) for i in range(k)]
        stride = k // 2
        while stride >= 1:
            pairs = [(i, i + stride) for i in range(k) if not i & stride]
            top = _apply_network(top, pairs)
            stride //= 2
    return top


def _staircase(k):
    return [(a, b) for a in range(k) for b in range(k) if (a + 1) * (b + 1) <= k]


def _k_largest(cands, k):
    n_pow2 = 1
    while n_pow2 < len(cands):
        n_pow2 *= 2
    return _apply_network(cands, _pruned_network(n_pow2, len(cands), k))[:k]


def _score_kernel(nt_ref, wq_ref, keys_ref, g1_ref, e2_ref, thr_ref,
                  q_ref, s_ref, top_ref, z_ref, *, tok):
    k = PEER_TOPK
    chunks = [(c, slice(c * LANES, (c + 1) * LANES)) for c in range(tok // LANES)]

    q_ref[...] = jnp.dot(wq_ref[...], nt_ref[...], preferred_element_type=F32)

    def per_half(hp, _):
        h = hp // 2
        p = hp % 2
        r0 = pl.multiple_of(hp * PEER_N_KEYS, PEER_N_KEYS)
        kf = keys_ref[hp]
        qf = q_ref[pl.ds(r0, PEER_N_KEYS), :]
        kh = kf.astype(BF16)
        qh = qf.astype(BF16)
        kl = (kf - kh.astype(F32)).astype(BF16)
        ql = (qf - qh.astype(F32)).astype(BF16)
        s = (jnp.dot(kh, qh, preferred_element_type=F32)
             + jnp.dot(kh, ql, preferred_element_type=F32)
             + jnp.dot(kl, qh, preferred_element_type=F32))
        s_ref[p, h] = s
        for c, cols in chunks:
            top = _top16_over_keys(s[:, cols])
            for i in range(k):
                top_ref[p, i, c, pl.ds(h, 1), :] = top[i][0:1, :]
        return 0

    lax.fori_loop(0, 2 * PEER_HEADS, per_half, 0, unroll=2)

    for c, cols in chunks:
        first = [top_ref[0, i, c] for i in range(k)]
        second = [top_ref[1, i, c] for i in range(k)]
        best = _k_largest([first[a] + second[b] for a, b in _staircase(k)], k)
        z = jnp.zeros_like(best[0])
        for v in best:
            z = z + jnp.exp(v - best[0])
        z_ref[c] = z
        inv_z = 1.0 / z
        g1_top = [jnp.exp(first[a] - first[0]) * inv_z for a in range(k)]
        e2_top = [jnp.exp(second[b] - second[0]) for b in range(k)]
        thr_ref[:, cols] = _k_largest([g1_top[a] * e2_top[b] for a, b in _staircase(k)], k)[k - 1]

    def per_head(h, _):
        for c, cols in chunks:
            inv_z = 1.0 / z_ref[c, pl.ds(h, 1), :]
            m1 = top_ref[0, 0, c, pl.ds(h, 1), :]
            m2 = top_ref[1, 0, c, pl.ds(h, 1), :]
            g1_ref[h, :, cols] = jnp.exp(s_ref[0, h, :, cols] - m1) * inv_z
            e2_ref[h, :, cols] = jnp.exp(s_ref[1, h, :, cols] - m2)
        return 0

    lax.fori_loop(0, PEER_HEADS, per_head, 0)


def _peer_scores(nt_bf, wq_t_bf, keys):
    d, t = nt_bf.shape
    tok = SCORE_TOK
    kern = functools.partial(_score_kernel, tok=tok)
    shape = (PEER_HEADS, PEER_N_KEYS, t)
    big_spec = pl.BlockSpec((PEER_HEADS, PEER_N_KEYS, tok), lambda i: (0, 0, i))
    return pl.pallas_call(
        kern,
        grid=(t // tok,),
        in_specs=[pl.BlockSpec((d, tok), lambda i: (0, i)),
                  pl.BlockSpec(wq_t_bf.shape, lambda i: (0, 0)),
                  pl.BlockSpec(keys.shape, lambda i: (0, 0, 0))],
        out_specs=[big_spec, big_spec,
                   pl.BlockSpec((PEER_HEADS, tok), lambda i: (0, i))],
        out_shape=[jax.ShapeDtypeStruct(shape, F32), jax.ShapeDtypeStruct(shape, F32),
                   jax.ShapeDtypeStruct((PEER_HEADS, t), F32)],
        scratch_shapes=[pltpu.VMEM((wq_t_bf.shape[0], tok), F32),
                        pltpu.VMEM((2, PEER_HEADS, PEER_N_KEYS, tok), F32),
                        pltpu.VMEM((2, PEER_TOPK, tok // LANES, PEER_HEADS, LANES), F32),
                        pltpu.VMEM((tok // LANES, PEER_HEADS, LANES), F32)],
        compiler_params=_params(("parallel",), 48),
        name="peer_scores",
    )(nt_bf, wq_t_bf, keys)


def _gelu_times(x, w):
    k0 = -2.0 * 0.7978845608028654 * 1.4426950408889634
    k1 = k0 * 0.044715
    e = jnp.exp2(x * (k0 + k1 * (x * x)))
    return (x * w) / (1.0 + e)


def _peer_kernel(nt_ref, u_ref, vt_ref, g1_ref, e2_ref, thr_ref, out_ref,
                 act0_ref, act1_ref, a0_ref, a1_ref, *, tok, n_exp, k_steps):
    g = pl.program_id(0)
    rows_per_step = n_exp // PEER_N_KEYS

    @pl.when(g == 0)
    def _():
        act1_ref[...] = jnp.zeros_like(act1_ref)
        a0_ref[...] = jnp.zeros_like(a0_ref)

    @pl.when(jnp.logical_or(g < 2, lax.rem(jnp.maximum(g - 2, 0), k_steps) == 0))
    def _():
        out_ref[...] = jnp.zeros_like(out_ref)

    kb = lax.rem(jnp.maximum(g - 1, 0), k_steps)
    i0 = pl.multiple_of(kb * rows_per_step, rows_per_step)

    def body(act_w, act_r, a_w, a_r):
        d = out_ref.shape[0]
        assert tok == MXU_COUNT * MXU_TILE and d % MXU_TILE == 0 and n_exp % MXU_TILE == 0
        k_tiles = {0: d // MXU_TILE, 1: n_exp // MXU_TILE}
        m_slices = {0: n_exp // MXU_TILE, 1: d // MXU_TILE}
        chunks = MXU_TILE // MM_ROWS
        acc_entries = MXU_TILE // 4

        def rhs_tile(mm, k, q):
            src = nt_ref if mm == 0 else a_r
            return src[k * MXU_TILE:(k + 1) * MXU_TILE, q * MXU_TILE:(q + 1) * MXU_TILE]

        def unit(mm, s, k, c):
            lhs_ref = u_ref if mm == 0 else vt_ref
            base = (2 * mm + s % 2) * acc_entries
            r0 = s * MXU_TILE + c * MM_ROWS
            lhs = lhs_ref[r0:r0 + MM_ROWS, k * MXU_TILE:(k + 1) * MXU_TILE]
            reg = k % 2
            for q in range(MXU_COUNT):
                if c == 0 and k == 0:
                    pltpu.matmul_push_rhs(rhs_tile(mm, 0, q), staging_register=reg, mxu_index=q)
                pltpu.matmul_acc_lhs(base + c * (MM_ROWS // 4), lhs, mxu_index=q,
                                     load_staged_rhs=reg if c == 0 else None)
                if c == 0 and k + 1 < k_tiles[mm]:
                    pltpu.matmul_push_rhs(rhs_tile(mm, k + 1, q), staging_register=1 - reg,
                                          mxu_index=q)

        def drain(mm, s):
            base = (2 * mm + s % 2) * acc_entries
            rows = slice(s * MXU_TILE, (s + 1) * MXU_TILE)
            for q in range(MXU_COUNT):
                cols = slice(q * MXU_TILE, (q + 1) * MXU_TILE)
                res = pltpu.matmul_pop(base, (MXU_TILE, MXU_TILE), F32, q)
                if mm == 0:
                    act_w[rows, cols] = res
                else:
                    out_ref[rows, cols] += res

        units = [(mm, s, k, c) for s in range(max(m_slices.values())) for mm in (0, 1)
                 if s < m_slices[mm] for k in range(k_tiles[mm]) for c in range(chunks)]
        pending = []

        def issue(idx):
            if idx < len(units):
                mm, s, k, c = units[idx]
                unit(mm, s, k, c)
                if k == k_tiles[mm] - 1 and c == chunks - 1:
                    pending.append((idx + POP_LAG_UNITS, mm, s))
            while pending and pending[0][0] <= idx:
                _, mm, s = pending.pop(0)
                drain(mm, s)

        slot = 0
        for ii in range(rows_per_step):
            for lc in range(tok // LANES):
                cols = slice(lc * LANES, (lc + 1) * LANES)
                g1b = [jnp.broadcast_to(
                    g1_ref[h, pl.ds(i0, rows_per_step), cols][ii:ii + 1, :], (W_ROWS, LANES))
                    for h in range(PEER_HEADS)]
                thrb = [jnp.broadcast_to(thr_ref[h:h + 1, cols], (W_ROWS, LANES))
                        for h in range(PEER_HEADS)]
                for jt in range(PEER_N_KEYS // W_ROWS):
                    keys = slice(jt * W_ROWS, (jt + 1) * W_ROWS)
                    rows = slice(ii * PEER_N_KEYS + jt * W_ROWS,
                                 ii * PEER_N_KEYS + (jt + 1) * W_ROWS)
                    parts = []
                    for h in range(PEER_HEADS):
                        p = e2_ref[h, keys, cols] * g1b[h]
                        parts.append(jnp.where(p >= thrb[h], p, 0.0))
                    while len(parts) > 1:
                        parts = [parts[i] + parts[i + 1] for i in range(0, len(parts), 2)]
                    a_w[rows, cols] = _gelu_times(act_r[rows, cols], parts[0]).astype(BF16)
                    issue(slot)
                    slot += 1
        while slot < len(units) or pending:
            issue(slot)
            slot += 1

    @pl.when(lax.rem(g, 2) == 0)
    def _():
        body(act0_ref, act1_ref, a1_ref, a0_ref)

    @pl.when(lax.rem(g, 2) == 1)
    def _():
        body(act1_ref, act0_ref, a0_ref, a1_ref)


def _peer_dense(nt_bf, u, vt, g1, e2, thr):
    d, t = nt_bf.shape
    n_experts = u.shape[0]
    tok, n_exp = PEER_TOK, PEER_EXP
    n_tok = t // tok
    k_steps = n_experts // n_exp
    kern = functools.partial(_peer_kernel, tok=tok, n_exp=n_exp, k_steps=k_steps)

    def tok_tile(lag):
        return lambda g: jnp.clip((g - lag) // k_steps, 0, n_tok - 1)

    def exp_tile(lag):
        return lambda g: jnp.maximum(g - lag, 0) % k_steps

    big_spec = pl.BlockSpec((PEER_HEADS, PEER_N_KEYS, tok), lambda g: (0, 0, tok_tile(1)(g)))
    return pl.pallas_call(
        kern,
        grid=(n_tok * k_steps + 2,),
        in_specs=[pl.BlockSpec((d, tok), lambda g: (0, tok_tile(0)(g))),
                  pl.BlockSpec((n_exp, d), lambda g: (exp_tile(0)(g), 0)),
                  pl.BlockSpec((d, n_exp), lambda g: (0, exp_tile(2)(g))),
                  big_spec, big_spec,
                  pl.BlockSpec((PEER_HEADS, tok), lambda g: (0, tok_tile(1)(g)))],
        out_specs=pl.BlockSpec((d, tok), lambda g: (0, tok_tile(2)(g))),
        out_shape=jax.ShapeDtypeStruct((d, t), F32),
        scratch_shapes=[pltpu.VMEM((n_exp, tok), F32), pltpu.VMEM((n_exp, tok), F32),
                        pltpu.VMEM((n_exp, tok), BF16), pltpu.VMEM((n_exp, tok), BF16)],
        compiler_params=_params(("arbitrary",), 48),
        name="peer_dense",
    )(nt_bf, u, vt, g1, e2, thr)


def _final_kernel(h_ref, pt_ref, g_ref, o_ref):
    o_ref[...] = _rms(h_ref[...] + pt_ref[...].T, g_ref[...])


def _final(h, peer_t, g):
    t, d = h.shape
    return pl.pallas_call(
        _final_kernel,
        grid=(t // ROW_TILE,),
        in_specs=[pl.BlockSpec((ROW_TILE, d), lambda i: (i, 0)),
                  pl.BlockSpec((d, ROW_TILE), lambda i: (0, i)),
                  pl.BlockSpec((1, d), lambda i: (0, 0))],
        out_specs=pl.BlockSpec((ROW_TILE, d), lambda i: (i, 0)),
        out_shape=jax.ShapeDtypeStruct((t, d), F32),
        compiler_params=_params(("parallel",), 40),
        name="final_norm",
    )(h, peer_t, g)


def _block_diag_chunks(w, ch):
    heads, hd, _ = w.shape
    per = ch // hd
    w4 = w.reshape(heads // per, per, hd, hd)
    eye = jnp.eye(per, dtype=w.dtype)
    return jnp.einsum("chij,hg->chigj", w4, eye).reshape(heads // per, ch, ch)


def _layer(h2, batch, seq, mix_norm_g, w_in, lru_conv_w, lru_conv_b, lru_w_rg, lru_b_rg,
           lru_w_ig, lru_b_ig, lru_lambda, conf_conv_w, conf_conv_b, conf_norm_g,
           conf_norm_b, beta_lru, beta_conv, w_out, ffn_norm_g, peer_w_q, peer_sub_keys,
           peer_u, peer_v):
    t, d = h2.shape
    d_lru = lru_conv_w.shape[1]
    d_conv = conf_conv_w.shape[1]
    row = lambda v: v.reshape(1, -1)

    z = _inproj(h2, row(mix_norm_g), w_in.astype(BF16))
    z3 = z.reshape(batch, seq, z.shape[1])

    ch = MIX_CH
    nc = d_lru // ch
    wg = jnp.concatenate([_block_diag_chunks(lru_w_rg[0], ch), _block_diag_chunks(lru_w_ig[0], ch),
                          _block_diag_chunks(lru_w_rg[1], ch), _block_diag_chunks(lru_w_ig[1], ch)],
                         axis=-1)
    bg = jnp.concatenate([lru_b_rg[0].reshape(nc, 1, ch), lru_b_ig[0].reshape(nc, 1, ch),
                          lru_b_rg[1].reshape(nc, 1, ch), lru_b_ig[1].reshape(nc, 1, ch)],
                         axis=-1)
    y_lru = _lru_branch(z3, lru_conv_w, row(lru_conv_b), wg.astype(BF16), bg, lru_lambda, d_lru)

    gdim = d_conv // CONV_GROUPS
    grp = jnp.arange(LANES) // gdim
    avg = ((grp[:, None] == grp[None, :]).astype(F32) / gdim).astype(BF16)
    y_conv = _conf_branch(z3, conf_conv_w, row(conf_conv_b), row(conf_norm_g),
                          row(conf_norm_b), avg, d_lru, d_conv)

    w_out_bf = w_out.astype(BF16)
    h2, nt_bf = _outproj(y_lru.reshape(t, d_lru), y_conv.reshape(t, d_conv), h2,
                             row(beta_lru), row(beta_conv), w_out_bf[:d_lru], w_out_bf[d_lru:],
                             row(ffn_norm_g))

    keys = peer_sub_keys.reshape(2 * PEER_HEADS, PEER_N_KEYS, -1)
    g1, e2, thr = _peer_scores(nt_bf, peer_w_q.T.astype(BF16), keys)
    peer_t = _peer_dense(nt_bf, peer_u, _transpose(peer_v), g1, e2, thr)
    return h2, peer_t


def kernel(x, mix_norm_g, w_in, lru_conv_w, lru_conv_b, lru_w_rg, lru_b_rg, lru_w_ig, lru_b_ig, lru_lambda, conf_conv_w, conf_conv_b, conf_norm_g, conf_norm_b, beta_lru, beta_conv, w_out, ffn_norm_g, peer_w_q, peer_sub_keys, peer_u, peer_v, final_norm_g):
    batch, seq, d = x.shape
    assert w_in.shape[0] == 1, "one layer: the final norm is fused with the layer's PEER residual"
    h2, peer_t = _layer(
        x.reshape(batch * seq, d), batch, seq, mix_norm_g[0], w_in[0], lru_conv_w[0],
        lru_conv_b[0], lru_w_rg[0], lru_b_rg[0], lru_w_ig[0], lru_b_ig[0], lru_lambda[0],
        conf_conv_w[0], conf_conv_b[0], conf_norm_g[0], conf_norm_b[0], beta_lru[0], beta_conv[0],
        w_out[0], ffn_norm_g[0], peer_w_q[0], peer_sub_keys[0], peer_u[0], peer_v[0])
    out = _final(h2, peer_t, final_norm_g.reshape(1, -1))
    return out.reshape(batch, seq, d)
```
